```python
import jax, jax.numpy as jnp
from jax import lax
import numpy as np

D_MODEL = 1024
BATCH = 2
SEQ = 8192
DEPTH = 2
DEC_BATCH = 8
DEC_SEQ = 32
PAST_LEN = 2048

CHUNK = 64
N_MIXERS = 2
N_HEADS = 16
HEAD_DIM = D_MODEL // N_HEADS
PAST_CHUNKS = 8
REACH = PAST_CHUNKS * CHUNK
BAND = (PAST_CHUNKS + 1) * CHUNK
MAX_REL = 128
N_REL = 2 * MAX_REL + 1
MLP_BLOCK = 128
GMLP_WIDTH = D_MODEL
N_GROUPS = 16
GROUP_DIM = GMLP_WIDTH // N_GROUPS
D_FF = 2816
CONV_W = 3
N_A = (DEPTH + 1) // 2
N_B = DEPTH // 2
EPS = 1e-6
NEG = -1e30

kernel_name = "hybrid_chunk_band_attn_gmlp_convffn_step"


def rms_norm(x, g):
    xf = x.astype(jnp.float32)
    y = xf * lax.rsqrt(jnp.mean(xf * xf, axis=-1, keepdims=True) + EPS)
    return (y * g.astype(jnp.float32)).astype(x.dtype)


def rel_bias(table, rel):
    idx = jnp.clip(rel, -MAX_REL, MAX_REL) + MAX_REL
    return table[:, idx].astype(jnp.float32)


def qkv_heads(h, w_qkv):
    B, T, _ = h.shape
    q, k, v = jnp.split(h @ w_qkv, 3, axis=-1)
    shp = (B, T, N_HEADS, HEAD_DIM)
    return q.reshape(shp), k.reshape(shp), v.reshape(shp)


def band_attn_prompt(h, w_qkv, table, w_o):
    B, S, _ = h.shape
    nc = S // CHUNK
    q, k, v = qkv_heads(h, w_qkv)
    pad = ((0, 0), (REACH, 0), (0, 0), (0, 0))
    kpad, vpad = jnp.pad(k, pad), jnp.pad(v, pad)
    rel = jnp.arange(CHUNK)[:, None] - jnp.arange(BAND)[None, :] + REACH
    bias = rel_bias(table, rel)
    scale = HEAD_DIM ** -0.5

    def one_chunk(c):
        start = c * CHUNK
        q_c = lax.dynamic_slice_in_dim(q, start, CHUNK, axis=1)
        k_c = lax.dynamic_slice_in_dim(kpad, start, BAND, axis=1)
        v_c = lax.dynamic_slice_in_dim(vpad, start, BAND, axis=1)
        s = jnp.einsum('bqhd,bkhd->bhqk', q_c, k_c).astype(jnp.float32) * scale + bias
        valid = (start - REACH + jnp.arange(BAND)) >= 0
        s = jnp.where(valid, s, NEG)
        p = jax.nn.softmax(s, axis=-1).astype(v.dtype)
        return jnp.einsum('bhqk,bkhd->bqhd', p, v_c)

    o = lax.map(one_chunk, jnp.arange(nc))
    o = jnp.moveaxis(o, 0, 1).reshape(B, S, D_MODEL)
    keep = min(REACH, S)
    return o @ w_o, k[:, S - keep:], v[:, S - keep:]


def band_attn_sample(h, cache_k, cache_v, w_qkv, table, w_o):
    B, T, _ = h.shape
    ca = cache_k.shape[1]
    q, k, v = qkv_heads(h, w_qkv)
    k_all = jnp.concatenate([cache_k.astype(k.dtype), k], axis=1)
    v_all = jnp.concatenate([cache_v.astype(v.dtype), v], axis=1)
    qpos = PAST_LEN + jnp.arange(T)
    kpos = jnp.concatenate([PAST_LEN - ca + jnp.arange(ca), PAST_LEN + jnp.arange(T)])
    bias = rel_bias(table, qpos[:, None] - kpos[None, :])
    s = jnp.einsum('bqhd,bkhd->bhqk', q, k_all).astype(jnp.float32) * (HEAD_DIM ** -0.5) + bias
    p = jax.nn.softmax(s, axis=-1).astype(v.dtype)
    o = jnp.einsum('bhqk,bkhd->bqhd', p, v_all).reshape(B, T, D_MODEL)
    return o @ w_o, k, v


def chunk_causal_mask(n):
    pos = jnp.arange(n)
    return (pos[None, :] // CHUNK) <= (pos[:, None] // CHUNK)


def gmlp_proj(h, w_in, v_gain):
    z = jax.nn.gelu(h @ w_in, approximate=False)
    u, v = jnp.split(z, 2, axis=-1)
    return u, rms_norm(v, v_gain)


def gmlp_prompt(h, w_in, v_gain, w_s, b_s, w_out):
    B, S, _ = h.shape
    nb = S // MLP_BLOCK
    u, v = gmlp_proj(h, w_in, v_gain)
    ws = jnp.where(chunk_causal_mask(MLP_BLOCK), w_s, 0.0)
    vb = v.reshape(B, nb, MLP_BLOCK, N_GROUPS, GROUP_DIM)
    mixed = jnp.einsum('gpq,bnqgd->bnpgd', ws.astype(v.dtype), vb) + b_s.T[None, None, :, :, None].astype(v.dtype)
    return (u * mixed.reshape(B, S, GMLP_WIDTH)) @ w_out


def gmlp_sample(h, w_in, v_gain, w_s, b_s, w_out):
    B, T, _ = h.shape
    u, v = gmlp_proj(h, w_in, v_gain)
    ws = jnp.where(chunk_causal_mask(MLP_BLOCK), w_s, 0.0)[:, :T, :T]
    vb = v.reshape(B, T, N_GROUPS, GROUP_DIM)
    mixed = jnp.einsum('gpq,bqgd->bpgd', ws.astype(v.dtype), vb) + b_s[:, :T].T[None, :, :, None].astype(v.dtype)
    return (u * mixed.reshape(B, T, GMLP_WIDTH)) @ w_out, v


def conv_ffn(h, hist, w_up, conv_w, conv_b, w_down):
    a = h @ w_up
    T = a.shape[1]
    full = jnp.concatenate([hist.astype(a.dtype), a], axis=1)
    c = conv_b.astype(a.dtype)
    for j in range(CONV_W):
        c = c + full[:, j:j + T] * conv_w[j].astype(a.dtype)
    gate, val = jnp.split(c, 2, axis=-1)
    return (jax.nn.silu(gate) * val) @ w_down, full[:, -(CONV_W - 1):]


def setup_inputs(seed: int = 0) -> dict:
    key = jax.random.key(seed)
    ks = jax.random.split(key, 24)
    nrm = lambda k, shp, s: jax.random.normal(k, shp, jnp.float32) * s
    ca = min(REACH, PAST_LEN)
    return {
        "x_prompt": nrm(ks[0], (BATCH, SEQ, D_MODEL), 1.0),
        "x_sample": nrm(ks[1], (DEC_BATCH, DEC_SEQ, D_MODEL), 1.0),
        "cache_a_k": nrm(ks[2], (N_A, DEC_BATCH, ca, N_HEADS, HEAD_DIM), 1.0),
        "cache_a_v": nrm(ks[3], (N_A, DEC_BATCH, ca, N_HEADS, HEAD_DIM), 1.0),
        "state_ffn_conv": nrm(ks[4], (DEPTH, DEC_BATCH, CONV_W - 1, 2 * D_FF), 1.0),
        "ln_mix": 1.0 + nrm(ks[5], (DEPTH, D_MODEL), 0.05),
        "ln_ffn": 1.0 + nrm(ks[6], (DEPTH, D_MODEL), 0.05),
        "ln_final": 1.0 + nrm(ks[7], (D_MODEL,), 0.05),
        "a_w_qkv": nrm(ks[8], (N_A, D_MODEL, 3 * D_MODEL), D_MODEL ** -0.5),
        "a_rel_bias": nrm(ks[9], (N_A, N_HEADS, N_REL), 0.5),
        "a_w_o": nrm(ks[10], (N_A, D_MODEL, D_MODEL), D_MODEL ** -0.5),
        "b_w_in": nrm(ks[11], (N_B, D_MODEL, 2 * GMLP_WIDTH), D_MODEL ** -0.5),
        "b_v_norm": 1.0 + nrm(ks[12], (N_B, GMLP_WIDTH), 0.05),
        "b_w_s": nrm(ks[13], (N_B, N_GROUPS, MLP_BLOCK, MLP_BLOCK), MLP_BLOCK ** -0.5),
        "b_bias_s": 1.0 + nrm(ks[14], (N_B, N_GROUPS, MLP_BLOCK), 0.1),
        "b_w_out": nrm(ks[15], (N_B, GMLP_WIDTH, D_MODEL), GMLP_WIDTH ** -0.5),
        "f_w_up": nrm(ks[16], (DEPTH, D_MODEL, 2 * D_FF), D_MODEL ** -0.5),
        "f_conv_w": nrm(ks[17], (DEPTH, CONV_W, 2 * D_FF), CONV_W ** -0.5),
        "f_conv_b": nrm(ks[18], (DEPTH, 2 * D_FF), 0.02),
        "f_w_down": nrm(ks[19], (DEPTH, D_FF, D_MODEL), D_FF ** -0.5),
    }


def reference(x_prompt, x_sample, cache_a_k, cache_a_v, state_ffn_conv,
              ln_mix, ln_ffn, ln_final,
              a_w_qkv, a_rel_bias, a_w_o,
              b_w_in, b_v_norm, b_w_s, b_bias_s, b_w_out,
              f_w_up, f_conv_w, f_conv_b, f_w_down):
    xp, xs = x_prompt, x_sample
    kp_l, vp_l, ks_l, vs_l, gv_l, cp_l, cs_l = [], [], [], [], [], [], []
    for i in range(DEPTH):
        j = i // N_MIXERS
        hp = rms_norm(xp, ln_mix[i])
        hs = rms_norm(xs, ln_mix[i])
        if i % N_MIXERS == 0:
            op, kp, vp = band_attn_prompt(hp, a_w_qkv[j], a_rel_bias[j], a_w_o[j])
            os_, kn, vn = band_attn_sample(hs, cache_a_k[j], cache_a_v[j], a_w_qkv[j], a_rel_bias[j], a_w_o[j])
            kp_l.append(kp); vp_l.append(vp); ks_l.append(kn); vs_l.append(vn)
        else:
            op = gmlp_prompt(hp, b_w_in[j], b_v_norm[j], b_w_s[j], b_bias_s[j], b_w_out[j])
            os_, gv = gmlp_sample(hs, b_w_in[j], b_v_norm[j], b_w_s[j], b_bias_s[j], b_w_out[j])
            gv_l.append(gv)
        xp = xp + op
        xs = xs + os_
        hp = rms_norm(xp, ln_ffn[i])
        hs = rms_norm(xs, ln_ffn[i])
        zero_hist = jnp.zeros((xp.shape[0], CONV_W - 1, 2 * D_FF), xp.dtype)
        fp, cp = conv_ffn(hp, zero_hist, f_w_up[i], f_conv_w[i], f_conv_b[i], f_w_down[i])
        fs, cs = conv_ffn(hs, state_ffn_conv[i], f_w_up[i], f_conv_w[i], f_conv_b[i], f_w_down[i])
        cp_l.append(cp); cs_l.append(cs)
        xp = xp + fp
        xs = xs + fs
    y_prompt = rms_norm(xp, ln_final)
    y_sample = rms_norm(xs, ln_final)
    return (y_prompt, y_sample, jnp.stack(kp_l), jnp.stack(vp_l), jnp.stack(ks_l), jnp.stack(vs_l),
            jnp.stack(gv_l), jnp.stack(cp_l), jnp.stack(cs_l))
```

```python
import functools

import jax
import jax.numpy as jnp
import numpy as np
from jax import lax
from jax.experimental import pallas as pl
from jax.experimental.pallas import tpu as pltpu

F32 = jnp.float32
BF16 = jnp.bfloat16

CHUNK = 64
PAST_CHUNKS = 8
REACH = PAST_CHUNKS * CHUNK
BAND = REACH + CHUNK
MAX_REL = 128
N_HEADS = 16
HEAD_DIM = 64
MLP_BLOCK = 128
N_GROUPS = 16
CONV_W = 3
EPS = 1e-6
NEG = -1e30

LANES = 128
SUBLANES = 8
MXU_DIM = 256
HEADS_PER_VREG = LANES // HEAD_DIM
VMEM_LIMIT = 56 * 1024 * 1024

ROW_TILE = 512
FF_CHUNK = MXU_DIM


def _rms(x, g):
    return x * lax.rsqrt(jnp.mean(x * x, axis=-1, keepdims=True) + EPS) * g


def _const_spec(shape):
    nd = len(shape)
    return pl.BlockSpec(shape, lambda *_: (0,) * nd, pipeline_mode=pl.Buffered(1))


def _params(n_grid):
    return pltpu.CompilerParams(
        dimension_semantics=("arbitrary",) * n_grid, vmem_limit_bytes=VMEM_LIMIT)


def _qkv_kernel(x_ref, g_ref, w_ref, q_ref, k_ref, v_ref, kt_ref, vt_ref, *, d, tail_tile0):
    t = pl.program_id(1)
    h = _rms(x_ref[...], g_ref[...]).astype(BF16)
    q = jnp.dot(h, w_ref[:, 0:d], preferred_element_type=F32)
    q_ref[...] = (q * (HEAD_DIM ** -0.5)).astype(BF16)
    k = jnp.dot(h, w_ref[:, d:2 * d], preferred_element_type=F32)
    k_ref[...] = k.astype(BF16)
    v = jnp.dot(h, w_ref[:, 2 * d:3 * d], preferred_element_type=F32)
    v_ref[...] = v.astype(BF16)

    @pl.when(t >= tail_tile0)
    def _():
        kt_ref[...] = k
        vt_ref[...] = v


def _qkv_proj(x, gain, w_bf, keep, tm):
    b, s, d = x.shape
    nt = s // tm
    assert s % tm == 0 and keep % tm == 0
    tail_tile0 = nt - keep // tm
    row = pl.BlockSpec((None, tm, d), lambda i, j: (i, j, 0))
    tail = pl.BlockSpec((None, tm, d), lambda i, j: (i, jnp.maximum(j - tail_tile0, 0), 0))
    return pl.pallas_call(
        functools.partial(_qkv_kernel, d=d, tail_tile0=tail_tile0),
        grid=(b, nt),
        in_specs=[row, _const_spec((1, d)), _const_spec((d, 3 * d))],
        out_specs=[row, row, row, tail, tail],
        out_shape=[jax.ShapeDtypeStruct((b, s, d), BF16)] * 3
        + [jax.ShapeDtypeStruct((b, keep, d), F32)] * 2,
        compiler_params=_params(2),
        name="qkv_proj",
    )(x, gain.reshape(1, d), w_bf)


def _head_pair_attention(qp, kb, vb, bias, lim):
    nq = qp.shape[0]
    lane = lax.broadcasted_iota(jnp.int32, (nq, LANES), 1)
    qf = qp.astype(F32)
    zero = jnp.zeros_like(qf)
    q_bd = jnp.concatenate(
        [jnp.where(lane < HEAD_DIM, qf, zero), jnp.where(lane >= HEAD_DIM, qf, zero)],
        axis=0).astype(BF16)
    s = lax.dot_general(q_bd, kb, (((1,), (1,)), ((), ())), preferred_element_type=F32)
    s = s + bias
    if lim is not None:
        col = lax.broadcasted_iota(jnp.int32, s.shape, 1)
        s = jnp.where(col >= lim, s, NEG)
    m = jnp.max(s, axis=-1, keepdims=True)
    e = jnp.exp(s - m)
    l = jnp.sum(e, axis=-1, keepdims=True)
    pv = jnp.dot(e.astype(BF16), vb, preferred_element_type=F32) * (1.0 / l)
    return jnp.where(lane < HEAD_DIM, pv[:nq], pv[nq:])


def _attn_prompt_kernel(q_ref, k_ref, v_ref, x_ref, wo_ref, bias_ref, y_ref,
                        kbuf, vbuf, obuf, *, tq):
    t = pl.program_id(1)
    n_pairs = q_ref.shape[1] // LANES

    @pl.when(t == 0)
    def _():
        kbuf[0:tq, :] = jnp.zeros((tq, kbuf.shape[1]), BF16)
        vbuf[0:tq, :] = jnp.zeros((tq, vbuf.shape[1]), BF16)

    @pl.when(t > 0)
    def _():
        kbuf[0:tq, :] = kbuf[tq:2 * tq, :]
        vbuf[0:tq, :] = vbuf[tq:2 * tq, :]

    kbuf[tq:2 * tq, :] = k_ref[...]
    vbuf[tq:2 * tq, :] = v_ref[...]

    for p in range(n_pairs):
        cols = slice(p * LANES, (p + 1) * LANES)
        bias = bias_ref[p]

        def chunk_body(i, carry, cols=cols, bias=bias):
            r0 = pl.multiple_of(i * CHUNK, CHUNK)
            lim = REACH - (t * tq + r0)
            o = _head_pair_attention(
                q_ref[pl.ds(r0, CHUNK), cols],
                kbuf[pl.ds(r0, BAND), cols],
                vbuf[pl.ds(r0, BAND), cols],
                bias, lim)
            obuf[pl.ds(r0, CHUNK), cols] = o.astype(BF16)
            return carry

        lax.fori_loop(0, tq // CHUNK, chunk_body, 0)

    y_ref[...] = x_ref[...] + jnp.dot(obuf[...], wo_ref[...], preferred_element_type=F32)


def _attn_prompt(q, k, v, x, wo_bf, bias_pairs):
    b, s, d = x.shape
    tq = REACH
    assert s % tq == 0
    row = pl.BlockSpec((None, tq, d), lambda i, j: (i, j, 0))
    return pl.pallas_call(
        functools.partial(_attn_prompt_kernel, tq=tq),
        grid=(b, s // tq),
        in_specs=[row, row, row, row, _const_spec((d, d)), _const_spec(bias_pairs.shape)],
        out_specs=row,
        out_shape=jax.ShapeDtypeStruct((b, s, d), F32),
        scratch_shapes=[pltpu.VMEM((2 * tq, d), BF16), pltpu.VMEM((2 * tq, d), BF16),
                        pltpu.VMEM((tq, d), BF16)],
        compiler_params=_params(2),
        name="attn_prompt",
    )(q, k, v, x, wo_bf, bias_pairs)


def _attn_sample_kernel(q_ref, k_ref, v_ref, ck_ref, cv_ref, x_ref, wo_ref, bias_ref, y_ref,
                        obuf):
    n_pairs = q_ref.shape[1] // LANES
    k_all = jnp.concatenate([ck_ref[...].astype(BF16), k_ref[...]], axis=0)
    v_all = jnp.concatenate([cv_ref[...].astype(BF16), v_ref[...]], axis=0)
    for p in range(n_pairs):
        cols = slice(p * LANES, (p + 1) * LANES)
        o = _head_pair_attention(q_ref[:, cols], k_all[:, cols], v_all[:, cols],
                                 bias_ref[p], None)
        obuf[:, cols] = o.astype(BF16)
    y_ref[...] = x_ref[...] + jnp.dot(obuf[...], wo_ref[...], preferred_element_type=F32)


def _attn_sample(q, k, v, cache_k, cache_v, x, wo_bf, bias_pairs):
    b, tn, d = x.shape
    ca = cache_k.shape[1]
    new = pl.BlockSpec((None, tn, d), lambda i: (i, 0, 0))
    old = pl.BlockSpec((None, ca, d), lambda i: (i, 0, 0))
    return pl.pallas_call(
        _attn_sample_kernel,
        grid=(b,),
        in_specs=[new, new, new, old, old, new, _const_spec((d, d)),
                  _const_spec(bias_pairs.shape)],
        out_specs=new,
        out_shape=jax.ShapeDtypeStruct((b, tn, d), F32),
        scratch_shapes=[pltpu.VMEM((tn, d), BF16)],
        compiler_params=_params(1),
        name="attn_sample",
    )(q, k, v, cache_k, cache_v, x, wo_bf, bias_pairs)


def _gmlp_kernel(x_ref, g_ref, win_ref, vg_ref, ws_ref, bs_ref, wout_ref, *rest,
                 blk, emit_v):
    if emit_v:
        y_ref, vout_ref, gated = rest
    else:
        y_ref, gated = rest
    rows, d = x_ref.shape
    n_pairs = d // LANES
    x = x_ref[...]
    h = _rms(x, g_ref[...]).astype(BF16)
    z = jnp.dot(h, win_ref[...], preferred_element_type=F32)
    z = 0.5 * z * (1.0 + lax.erf(z * np.float32(np.sqrt(0.5))))
    u = z[:, :d]
    v = _rms(z[:, d:], vg_ref[...])
    if emit_v:
        vout_ref[...] = v
    v_bf = v.astype(BF16)
    lane = lax.broadcasted_iota(jnp.int32, (blk, LANES), 1)
    for r in range(rows // blk):
        rs = slice(r * blk, (r + 1) * blk)
        for p in range(n_pairs):
            cols = slice(p * LANES, (p + 1) * LANES)
            m = jnp.dot(ws_ref[p], v_bf[rs, cols], preferred_element_type=F32)
            mixed = jnp.where(lane < HEAD_DIM, m[:blk], m[blk:]) + bs_ref[:, cols]
            gated[rs, cols] = (u[rs, cols] * mixed).astype(BF16)
    y_ref[...] = x + jnp.dot(gated[...], wout_ref[...], preferred_element_type=F32)


def _gmlp(x, gain, win_bf, v_gain, ws_pairs, bs_rows, wout_bf, rows, blk, emit_v):
    b, s, d = x.shape
    w = win_bf.shape[1] - d
    assert s % rows == 0 and rows % blk == 0 and w == d
    row = pl.BlockSpec((None, rows, d), lambda i, j: (i, j, 0))
    out_specs = [row]
    out_shape = [jax.ShapeDtypeStruct((b, s, d), F32)]
    if emit_v:
        out_specs.append(row)
        out_shape.append(jax.ShapeDtypeStruct((b, s, w), F32))
    return pl.pallas_call(
        functools.partial(_gmlp_kernel, blk=blk, emit_v=emit_v),
        grid=(b, s // rows),
        in_specs=[row, _const_spec((1, d)), _const_spec(win_bf.shape), _const_spec((1, w)),
                  _const_spec(ws_pairs.shape), _const_spec(bs_rows.shape),
                  _const_spec(wout_bf.shape)],
        out_specs=out_specs,
        out_shape=out_shape,
        scratch_shapes=[pltpu.VMEM((rows, w), BF16)],
        compiler_params=_params(2),
        name="gmlp",
    )(x, gain.reshape(1, d), win_bf, v_gain.reshape(1, w), ws_pairs, bs_rows, wout_bf)


def _ffn_kernel(x_ref, g_ref, hist_ref, wup_ref, cw_ref, cb_ref, wdn_ref, gf_ref,
                y_ref, carry_ref, h_scr, acc_scr, *, n_sub, final_norm):
    t = pl.program_id(1)
    rows = x_ref.shape[0]
    seq = rows // n_sub
    n_chunks = wdn_ref.shape[0]
    fc = wdn_ref.shape[1]

    @pl.when(t == 0)
    def _():
        carry_ref[...] = hist_ref[...]

    x = x_ref[...]
    h_scr[...] = _rms(x, g_ref[...]).astype(BF16)

    def conv(a, c):
        w = cw_ref[c]
        outs = []
        for s in range(n_sub):
            a_s = a[s * seq:(s + 1) * seq]
            ext = jnp.concatenate([carry_ref[s, c], a_s], axis=0)
            n = ext.shape[0]
            full = (cb_ref[c] + ext * w[2:3] + pltpu.roll(ext, 1, 0) * w[1:2]
                    + pltpu.roll(ext, 2, 0) * w[0:1])
            outs.append(full[SUBLANES:n])
            carry_ref[s, c] = a_s[seq - SUBLANES:seq]
        return outs[0] if n_sub == 1 else jnp.concatenate(outs, axis=0)

    for j in range(n_chunks):
        a = jnp.dot(h_scr[...], wup_ref[j], preferred_element_type=F32)
        gate = conv(a[:, :fc], j)
        val = conv(a[:, fc:], n_chunks + j)
        act = (jax.nn.silu(gate) * val).astype(BF16)
        contrib = jnp.dot(act, wdn_ref[j], preferred_element_type=F32)
        if j == 0:
            acc_scr[...] = contrib
        else:
            acc_scr[...] += contrib

    y = x + acc_scr[...]
    if final_norm:
        y = _rms(y, gf_ref[...])
    y_ref[...] = y


def _conv_ffn(x, gain, hist, wup_c, cw_c, cb_c, wdn_c, final_gain, rows, n_sub):
    b, s, d = x.shape
    assert s % rows == 0 and (n_sub == 1 or s == rows)
    assert (rows // n_sub) % SUBLANES == 0
    final_norm = final_gain is not None
    gf = (final_gain if final_norm else jnp.ones((d,), F32)).reshape(1, d)
    row = pl.BlockSpec((None, rows, d), lambda i, j: (i, j, 0))
    hist_spec = pl.BlockSpec((None,) + hist.shape[1:], lambda i, j: (i, 0, 0, 0, 0))
    return pl.pallas_call(
        functools.partial(_ffn_kernel, n_sub=n_sub, final_norm=final_norm),
        grid=(b, s // rows),
        in_specs=[row, _const_spec((1, d)), hist_spec, _const_spec(wup_c.shape),
                  _const_spec(cw_c.shape), _const_spec(cb_c.shape), _const_spec(wdn_c.shape),
                  _const_spec((1, d))],
        out_specs=[row, hist_spec],
        out_shape=[jax.ShapeDtypeStruct((b, s, d), F32),
                   jax.ShapeDtypeStruct(hist.shape, F32)],
        scratch_shapes=[pltpu.VMEM((rows, d), BF16), pltpu.VMEM((rows, d), F32)],
        compiler_params=_params(2),
        name="conv_ffn",
    )(x, gain.reshape(1, d), hist, wup_c, cw_c, cb_c, wdn_c, gf)


def _pair_bias(table, rel):
    idx = jnp.clip(rel, -MAX_REL, MAX_REL) + MAX_REL
    bias = table[:, idx].astype(F32)
    h, q, k = bias.shape
    return bias.reshape(h // HEADS_PER_VREG, HEADS_PER_VREG * q, k)


def _ffn_weights(w_up, conv_w, conv_b, w_down):
    d, two_ff = w_up.shape
    ff = two_ff // 2
    c = ff // FF_CHUNK
    assert ff % FF_CHUNK == 0
    up = w_up.astype(BF16).reshape(d, 2, c, FF_CHUNK)
    up = jnp.transpose(up, (2, 0, 1, 3)).reshape(c, d, 2 * FF_CHUNK)
    cw = jnp.transpose(conv_w.reshape(CONV_W, 2 * c, FF_CHUNK), (1, 0, 2))
    cb = conv_b.reshape(2 * c, 1, FF_CHUNK)
    dn = w_down.astype(BF16).reshape(c, FF_CHUNK, w_down.shape[1])
    return up, cw, cb, dn


def _hist_to_chunks(hist):
    b, n, r, w = hist.shape
    h = jnp.pad(hist, ((0, 0), (0, 0), (SUBLANES - r, 0), (0, 0)))
    h = h.reshape(b, n, SUBLANES, w // FF_CHUNK, FF_CHUNK)
    return jnp.transpose(h, (0, 1, 3, 2, 4))


def _chunks_to_hist(carry):
    b, n, c, r, f = carry.shape
    h = jnp.transpose(carry[:, :, :, r - (CONV_W - 1):, :], (0, 1, 3, 2, 4))
    return h.reshape(b, n, CONV_W - 1, c * f)


def kernel(x_prompt, x_sample, cache_a_k, cache_a_v, state_ffn_conv, ln_mix, ln_ffn, ln_final,
           a_w_qkv, a_rel_bias, a_w_o, b_w_in, b_v_norm, b_w_s, b_bias_s, b_w_out,
           f_w_up, f_conv_w, f_conv_b, f_w_down):
    bp, sp, d = x_prompt.shape
    bs, ts, _ = x_sample.shape
    depth = ln_mix.shape[0]
    ca = cache_a_k.shape[2]
    keep = min(REACH, sp)

    xp = x_prompt
    xs = x_sample.reshape(1, bs * ts, d)
    kp_l, vp_l, ks_l, vs_l, gv_l, cp_l, cs_l = [], [], [], [], [], [], []

    for i in range(depth):
        j = i // 2
        if i % 2 == 0:
            wqkv = a_w_qkv[j].astype(BF16)
            wo = a_w_o[j].astype(BF16)
            rel_p = jnp.arange(CHUNK)[:, None] - jnp.arange(BAND)[None, :] + REACH
            q, k, v, kt, vt = _qkv_proj(xp, ln_mix[i], wqkv, keep, ROW_TILE)
            xp = _attn_prompt(q, k, v, xp, wo, _pair_bias(a_rel_bias[j], rel_p))
            kp_l.append(kt.reshape(bp, keep, N_HEADS, HEAD_DIM))
            vp_l.append(vt.reshape(bp, keep, N_HEADS, HEAD_DIM))
            kpos = jnp.concatenate([jnp.arange(ca) - ca, jnp.arange(ts)])
            rel_s = jnp.arange(ts)[:, None] - kpos[None, :]
            q, k, v, kt, vt = _qkv_proj(xs, ln_mix[i], wqkv, bs * ts, bs * ts)
            xs = _attn_sample(
                q.reshape(bs, ts, d), k.reshape(bs, ts, d), v.reshape(bs, ts, d),
                cache_a_k[j].reshape(bs, ca, d), cache_a_v[j].reshape(bs, ca, d),
                xs.reshape(bs, ts, d), wo, _pair_bias(a_rel_bias[j], rel_s),
            ).reshape(1, bs * ts, d)
            ks_l.append(kt.reshape(bs, ts, N_HEADS, HEAD_DIM))
            vs_l.append(vt.reshape(bs, ts, N_HEADS, HEAD_DIM))
        else:
            win = b_w_in[j].astype(BF16)
            wout = b_w_out[j].astype(BF16)
            pos = jnp.arange(MLP_BLOCK)
            causal = (pos[None, :] // CHUNK) <= (pos[:, None] // CHUNK)
            ws = jnp.where(causal, b_w_s[j], 0.0).astype(BF16)
            g = ws.shape[0]

            def pair_rows(w):
                return w.reshape(g // HEADS_PER_VREG, HEADS_PER_VREG * w.shape[1], w.shape[2])

            def bias_rows(bias):
                return jnp.repeat(bias.T.astype(F32), d // g, axis=1)

            xp = _gmlp(xp, ln_mix[i], win, b_v_norm[j], pair_rows(ws), bias_rows(b_bias_s[j]),
                       wout, ROW_TILE, MLP_BLOCK, False)[0]
            xs, gv = _gmlp(xs, ln_mix[i], win, b_v_norm[j], pair_rows(ws[:, :ts, :ts]),
                           bias_rows(b_bias_s[j][:, :ts]), wout, bs * ts, ts, True)
            gv_l.append(gv.reshape(bs, ts, d))

        up, cw, cb, dn = _ffn_weights(f_w_up[i], f_conv_w[i], f_conv_b[i], f_w_down[i])
        fin = ln_final if i == depth - 1 else None
        two_ff = f_w_up.shape[2]
        zero_hist = jnp.zeros((bp, 1, CONV_W - 1, two_ff), F32)
        xp, cp = _conv_ffn(xp, ln_ffn[i], _hist_to_chunks(zero_hist), up, cw, cb, dn, fin,
                           ROW_TILE, 1)
        xs, cs = _conv_ffn(xs, ln_ffn[i], _hist_to_chunks(state_ffn_conv[i][None]), up, cw, cb,
                           dn, fin, bs * ts, bs)
        cp_l.append(_chunks_to_hist(cp)[:, 0])
        cs_l.append(_chunks_to_hist(cs)[0])

    return (xp, xs.reshape(bs, ts, d), jnp.stack(kp_l), jnp.stack(vp_l), jnp.stack(ks_l),
            jnp.stack(vs_l), jnp.stack(gv_l), jnp.stack(cp_l), jnp.stack(cs_l))
```

```python
import functools

import jax
import jax.numpy as jnp
import numpy as np
from jax import lax
from jax.experimental import pallas as pl
from jax.experimental.pallas import tpu as pltpu

F32 = jnp.float32
BF16 = jnp.bfloat16

CHUNK = 64
PAST_CHUNKS = 8
REACH = PAST_CHUNKS * CHUNK
BAND = REACH + CHUNK
MAX_REL = 128
N_HEADS = 16
HEAD_DIM = 64
MLP_BLOCK = 128
N_GROUPS = 16
CONV_W = 3
EPS = 1e-6
NEG = -1e30

LANES = 128
SUBLANES = 8
MXU_DIM = 256
HEADS_PER_VREG = LANES // HEAD_DIM
VMEM_LIMIT = 56 * 1024 * 1024

ROW_TILE = 512
FF_CHUNK = MXU_DIM
GROUP_CHUNKS = LANES // CHUNK
GROUP_ROWS = GROUP_CHUNKS * CHUNK
GROUP_KEYS = REACH + GROUP_ROWS
ATTN_GROUPS_PER_TRIP = 2


def _rms(x, g):
    return x * lax.rsqrt(jnp.mean(x * x, axis=-1, keepdims=True) + EPS) * g


def _const_spec(shape):
    nd = len(shape)
    return pl.BlockSpec(shape, lambda *_: (0,) * nd, pipeline_mode=pl.Buffered(1))


def _params(n_grid):
    return pltpu.CompilerParams(
        dimension_semantics=("arbitrary",) * n_grid, vmem_limit_bytes=VMEM_LIMIT)


def _qkv_kernel(x_ref, g_ref, w_ref, q_ref, k_ref, v_ref, kt_ref, vt_ref, *, d, tail_tile0):
    t = pl.program_id(1)
    h = _rms(x_ref[...], g_ref[...]).astype(BF16)
    q = jnp.dot(h, w_ref[:, 0:d], preferred_element_type=F32)
    q_ref[...] = (q * (HEAD_DIM ** -0.5)).astype(BF16)
    k = jnp.dot(h, w_ref[:, d:2 * d], preferred_element_type=F32)
    k_ref[...] = k.astype(BF16)
    v = jnp.dot(h, w_ref[:, 2 * d:3 * d], preferred_element_type=F32)
    v_ref[...] = v.astype(BF16)

    @pl.when(t >= tail_tile0)
    def _():
        kt_ref[...] = k
        vt_ref[...] = v


def _qkv_proj(x, gain, w_bf, keep, tm):
    b, s, d = x.shape
    nt = s // tm
    assert s % tm == 0 and keep % tm == 0
    tail_tile0 = nt - keep // tm
    row = pl.BlockSpec((None, tm, d), lambda i, j: (i, j, 0))
    tail = pl.BlockSpec((None, tm, d), lambda i, j: (i, jnp.maximum(j - tail_tile0, 0), 0))
    return pl.pallas_call(
        functools.partial(_qkv_kernel, d=d, tail_tile0=tail_tile0),
        grid=(b, nt),
        in_specs=[row, _const_spec((1, d)), _const_spec((d, 3 * d))],
        out_specs=[row, row, row, tail, tail],
        out_shape=[jax.ShapeDtypeStruct((b, s, d), BF16)] * 3
        + [jax.ShapeDtypeStruct((b, keep, d), F32)] * 2,
        compiler_params=_params(2),
        name="qkv_proj",
    )(x, gain.reshape(1, d), w_bf)


def _pair_scores(qp, kb):
    lane = lax.broadcasted_iota(jnp.int32, qp.shape, 1)
    qf = qp.astype(F32)
    zero = jnp.zeros_like(qf)
    q_bd = jnp.concatenate(
        [jnp.where(lane < HEAD_DIM, qf, zero), jnp.where(lane >= HEAD_DIM, qf, zero)],
        axis=0).astype(BF16)
    return lax.dot_general(q_bd, kb, (((1,), (1,)), ((), ())), preferred_element_type=F32)


def _pair_softmax(s, bias, lim):
    s = s + bias
    if lim is not None:
        col = lax.broadcasted_iota(jnp.int32, s.shape, 1)
        s = jnp.where(col >= lim, s, NEG)
    m = jnp.max(s, axis=-1, keepdims=True)
    e = jnp.exp(s - m)
    return e.astype(BF16), 1.0 / jnp.sum(e, axis=-1, keepdims=True)


def _pair_output(e, inv_l, vb):
    nq = e.shape[0] // HEADS_PER_VREG
    lane = lax.broadcasted_iota(jnp.int32, (nq, LANES), 1)
    pv = jnp.dot(e, vb, preferred_element_type=F32) * inv_l
    return jnp.where(lane < HEAD_DIM, pv[:nq], pv[nq:])


def _pipelined_pairs(n, scores, softmax, output):
    s, e = {}, {}
    for step in range(n + 2):
        if step < n:
            s[step] = scores(step)
        if 1 <= step <= n:
            e[step - 1] = softmax(step - 1, s.pop(step - 1))
        if step >= 2:
            output(step - 2, *e.pop(step - 2))


def _attn_prompt_kernel(q_ref, k_ref, v_ref, x_ref, wo_ref, bias_ref, y_ref,
                        kbuf, vbuf, obuf, *, tq):
    t = pl.program_id(1)
    n_pairs = q_ref.shape[1] // LANES

    @pl.when(t == 0)
    def _():
        kbuf[0:tq, :] = jnp.zeros((tq, kbuf.shape[1]), BF16)
        vbuf[0:tq, :] = jnp.zeros((tq, vbuf.shape[1]), BF16)

    @pl.when(t > 0)
    def _():
        kbuf[0:tq, :] = kbuf[tq:2 * tq, :]
        vbuf[0:tq, :] = vbuf[tq:2 * tq, :]

    kbuf[tq:2 * tq, :] = k_ref[...]
    vbuf[tq:2 * tq, :] = v_ref[...]

    def chunk_loop(masked):
        span = ATTN_GROUPS_PER_TRIP * GROUP_ROWS

        def trip_body(i, carry):
            base = pl.multiple_of(i * span, span)

            def item(n):
                g, p = divmod(n, n_pairs)
                return base + g * GROUP_ROWS, slice(p * LANES, (p + 1) * LANES), p

            def scores(n):
                r0, cols, _ = item(n)
                return _pair_scores(q_ref[pl.ds(r0, GROUP_ROWS), cols],
                                    kbuf[pl.ds(r0, GROUP_KEYS), cols])

            def softmax(n, s):
                r0, _, p = item(n)
                lim = (REACH - (t * tq + r0)) if masked else None
                return _pair_softmax(s, bias_ref[p], lim)

            def output(n, e, inv_l):
                r0, cols, _ = item(n)
                o = _pair_output(e, inv_l, vbuf[pl.ds(r0, GROUP_KEYS), cols])
                obuf[pl.ds(r0, GROUP_ROWS), cols] = o.astype(BF16)

            _pipelined_pairs(ATTN_GROUPS_PER_TRIP * n_pairs, scores, softmax, output)
            return carry

        lax.fori_loop(0, tq // span, trip_body, 0)

    @pl.when(t == 0)
    def _():
        chunk_loop(True)

    @pl.when(t > 0)
    def _():
        chunk_loop(False)

    y_ref[...] = x_ref[...] + jnp.dot(obuf[...], wo_ref[...], preferred_element_type=F32)


def _attn_prompt(q, k, v, x, wo_bf, bias_pairs):
    b, s, d = x.shape
    tq = REACH
    assert s % tq == 0
    row = pl.BlockSpec((None, tq, d), lambda i, j: (i, j, 0))
    return pl.pallas_call(
        functools.partial(_attn_prompt_kernel, tq=tq),
        grid=(b, s // tq),
        in_specs=[row, row, row, row, _const_spec((d, d)), _const_spec(bias_pairs.shape)],
        out_specs=row,
        out_shape=jax.ShapeDtypeStruct((b, s, d), F32),
        scratch_shapes=[pltpu.VMEM((2 * tq, d), BF16), pltpu.VMEM((2 * tq, d), BF16),
                        pltpu.VMEM((tq, d), BF16)],
        compiler_params=_params(2),
        name="attn_prompt",
    )(q, k, v, x, wo_bf, bias_pairs)


def _attn_sample_kernel(q_ref, k_ref, v_ref, ck_ref, cv_ref, x_ref, wo_ref, bias_ref, y_ref,
                        obuf):
    n_pairs = q_ref.shape[1] // LANES
    k_all = jnp.concatenate([ck_ref[...].astype(BF16), k_ref[...]], axis=0)
    v_all = jnp.concatenate([cv_ref[...].astype(BF16), v_ref[...]], axis=0)
    cols = [slice(p * LANES, (p + 1) * LANES) for p in range(n_pairs)]

    def scores(p):
        return _pair_scores(q_ref[:, cols[p]], k_all[:, cols[p]])

    def softmax(p, s):
        return _pair_softmax(s, bias_ref[p], None)

    def output(p, e, inv_l):
        obuf[:, cols[p]] = _pair_output(e, inv_l, v_all[:, cols[p]]).astype(BF16)

    _pipelined_pairs(n_pairs, scores, softmax, output)
    y_ref[...] = x_ref[...] + jnp.dot(obuf[...], wo_ref[...], preferred_element_type=F32)


def _attn_sample(q, k, v, cache_k, cache_v, x, wo_bf, bias_pairs):
    b, tn, d = x.shape
    ca = cache_k.shape[1]
    new = pl.BlockSpec((None, tn, d), lambda i: (i, 0, 0))
    old = pl.BlockSpec((None, ca, d), lambda i: (i, 0, 0))
    return pl.pallas_call(
        _attn_sample_kernel,
        grid=(b,),
        in_specs=[new, new, new, old, old, new, _const_spec((d, d)),
                  _const_spec(bias_pairs.shape)],
        out_specs=new,
        out_shape=jax.ShapeDtypeStruct((b, tn, d), F32),
        scratch_shapes=[pltpu.VMEM((tn, d), BF16)],
        compiler_params=_params(1),
        name="attn_sample",
    )(q, k, v, cache_k, cache_v, x, wo_bf, bias_pairs)


def _gmlp_kernel(x_ref, g_ref, win_ref, vg_ref, ws_ref, bs_ref, wout_ref, *rest,
                 blk, emit_v):
    if emit_v:
        y_ref, vout_ref, gated = rest
    else:
        y_ref, gated = rest
    rows, d = x_ref.shape
    n_pairs = d // LANES
    x = x_ref[...]
    h = _rms(x, g_ref[...]).astype(BF16)
    z = jnp.dot(h, win_ref[...], preferred_element_type=F32)
    z = 0.5 * z * (1.0 + lax.erf(z * np.float32(np.sqrt(0.5))))
    u = z[:, :d]
    v = _rms(z[:, d:], vg_ref[...])
    if emit_v:
        vout_ref[...] = v
    v_bf = v.astype(BF16)
    lane = lax.broadcasted_iota(jnp.int32, (blk, LANES), 1)
    for r in range(rows // blk):
        rs = slice(r * blk, (r + 1) * blk)
        for p in range(n_pairs):
            cols = slice(p * LANES, (p + 1) * LANES)
            m = jnp.dot(ws_ref[p], v_bf[rs, cols], preferred_element_type=F32)
            mixed = jnp.where(lane < HEAD_DIM, m[:blk], m[blk:]) + bs_ref[:, cols]
            gated[rs, cols] = (u[rs, cols] * mixed).astype(BF16)
    y_ref[...] = x + jnp.dot(gated[...], wout_ref[...], preferred_element_type=F32)


def _gmlp(x, gain, win_bf, v_gain, ws_pairs, bs_rows, wout_bf, rows, blk, emit_v):
    b, s, d = x.shape
    w = win_bf.shape[1] - d
    assert s % rows == 0 and rows % blk == 0 and w == d
    row = pl.BlockSpec((None, rows, d), lambda i, j: (i, j, 0))
    out_specs = [row]
    out_shape = [jax.ShapeDtypeStruct((b, s, d), F32)]
    if emit_v:
        out_specs.append(row)
        out_shape.append(jax.ShapeDtypeStruct((b, s, w), F32))
    return pl.pallas_call(
        functools.partial(_gmlp_kernel, blk=blk, emit_v=emit_v),
        grid=(b, s // rows),
        in_specs=[row, _const_spec((1, d)), _const_spec(win_bf.shape), _const_spec((1, w)),
                  _const_spec(ws_pairs.shape), _const_spec(bs_rows.shape),
                  _const_spec(wout_bf.shape)],
        out_specs=out_specs,
        out_shape=out_shape,
        scratch_shapes=[pltpu.VMEM((rows, w), BF16)],
        compiler_params=_params(2),
        name="gmlp",
    )(x, gain.reshape(1, d), win_bf, v_gain.reshape(1, w), ws_pairs, bs_rows, wout_bf)


def _ffn_kernel(x_ref, g_ref, hist_ref, wup_ref, cw_ref, cb_ref, wdn_ref, gf_ref,
                y_ref, carry_ref, h_scr, a_scr, act_scr, *, n_sub, final_norm):
    t = pl.program_id(1)
    rows = x_ref.shape[0]
    seq = rows // n_sub
    ff = wdn_ref.shape[0]
    fc = FF_CHUNK

    @pl.when(t == 0)
    def _():
        carry_ref[...] = hist_ref[...]

    x = x_ref[...]
    h_scr[...] = _rms(x, g_ref[...]).astype(BF16)

    def conv(a, cols, slot):
        w = cw_ref[:, cols]
        outs = []
        for s in range(n_sub):
            a_scr[slot, s, 0:SUBLANES, :] = carry_ref[s, :, cols]
            a_scr[slot, s, SUBLANES:SUBLANES + seq, :] = a[s * seq:(s + 1) * seq]
            carry_ref[s, :, cols] = a[(s + 1) * seq - SUBLANES:(s + 1) * seq]
            c = cb_ref[:, cols] + a_scr[slot, s, SUBLANES:SUBLANES + seq, :] * w[2:3]
            c = c + a_scr[slot, s, SUBLANES - 1:SUBLANES - 1 + seq, :] * w[1:2]
            c = c + a_scr[slot, s, SUBLANES - 2:SUBLANES - 2 + seq, :] * w[0:1]
            outs.append(c)
        return outs[0] if n_sub == 1 else jnp.concatenate(outs, axis=0)

    for j in range(ff // fc):
        gcols = slice(j * fc, (j + 1) * fc)
        vcols = slice(ff + j * fc, ff + (j + 1) * fc)
        slot = 2 * (j % 2)
        gate = conv(jnp.dot(h_scr[...], wup_ref[:, gcols], preferred_element_type=F32), gcols,
                    slot)
        val = conv(jnp.dot(h_scr[...], wup_ref[:, vcols], preferred_element_type=F32), vcols,
                   slot + 1)
        act_scr[:, gcols] = (jax.nn.silu(gate) * val).astype(BF16)

    y = x + jnp.dot(act_scr[...], wdn_ref[...], preferred_element_type=F32)
    if final_norm:
        y = _rms(y, gf_ref[...])
    y_ref[...] = y


def _conv_ffn(x, gain, hist, wup_bf, conv_w, conv_b, wdn_bf, final_gain, rows, n_sub):
    b, s, d = x.shape
    two_ff = wup_bf.shape[1]
    ff = two_ff // 2
    seq = rows // n_sub
    assert s % rows == 0 and (n_sub == 1 or s == rows)
    assert seq % SUBLANES == 0 and ff % FF_CHUNK == 0
    final_norm = final_gain is not None
    gf = (final_gain if final_norm else jnp.ones((d,), F32)).reshape(1, d)
    row = pl.BlockSpec((None, rows, d), lambda i, j: (i, j, 0))
    hist_spec = pl.BlockSpec((None,) + hist.shape[1:], lambda i, j: (i, 0, 0, 0))
    return pl.pallas_call(
        functools.partial(_ffn_kernel, n_sub=n_sub, final_norm=final_norm),
        grid=(b, s // rows),
        in_specs=[row, _const_spec((1, d)), hist_spec, _const_spec(wup_bf.shape),
                  _const_spec(conv_w.shape), _const_spec((1, two_ff)),
                  _const_spec(wdn_bf.shape), _const_spec((1, d))],
        out_specs=[row, hist_spec],
        out_shape=[jax.ShapeDtypeStruct((b, s, d), F32),
                   jax.ShapeDtypeStruct(hist.shape, F32)],
        scratch_shapes=[pltpu.VMEM((rows, d), BF16),
                        pltpu.VMEM((4, n_sub, seq + SUBLANES, FF_CHUNK), F32),
                        pltpu.VMEM((rows, ff), BF16)],
        compiler_params=_params(2),
        name="conv_ffn",
    )(x, gain.reshape(1, d), hist, wup_bf, conv_w, conv_b.reshape(1, two_ff), wdn_bf, gf)


def _pair_bias(table, n_q, n_k, reach, band=None):
    diag = jnp.arange(n_q + n_k - 1) - (n_q - 1)
    idx = jnp.clip(reach - diag, -MAX_REL, MAX_REL) + MAX_REL
    r = table[:, idx].astype(F32)
    bias = jnp.stack([r[:, n_q - 1 - q:n_q - 1 - q + n_k] for q in range(n_q)], axis=1)
    if band is not None:
        first = (jnp.arange(n_q) // CHUNK * CHUNK)[:, None]
        key = jnp.arange(n_k)[None, :]
        bias = jnp.where((key >= first) & (key < first + band), bias, NEG)
    h = bias.shape[0]
    return bias.reshape(h // HEADS_PER_VREG, HEADS_PER_VREG * n_q, n_k)


def _pad_hist(hist):
    pad = [(0, 0)] * (hist.ndim - 2) + [(SUBLANES - hist.shape[-2], 0), (0, 0)]
    return jnp.pad(hist, pad)


def kernel(x_prompt, x_sample, cache_a_k, cache_a_v, state_ffn_conv, ln_mix, ln_ffn, ln_final,
           a_w_qkv, a_rel_bias, a_w_o, b_w_in, b_v_norm, b_w_s, b_bias_s, b_w_out,
           f_w_up, f_conv_w, f_conv_b, f_w_down):
    bp, sp, d = x_prompt.shape
    bs, ts, _ = x_sample.shape
    depth = ln_mix.shape[0]
    ca = cache_a_k.shape[2]
    keep = min(REACH, sp)
    two_ff = f_w_up.shape[2]
    hist_rows = CONV_W - 1

    xp = x_prompt
    xs = x_sample.reshape(1, bs * ts, d)
    kp_l, vp_l, ks_l, vs_l, gv_l, cp_l, cs_l = [], [], [], [], [], [], []

    for i in range(depth):
        j = i // 2
        if i % 2 == 0:
            wqkv = a_w_qkv[j].astype(BF16)
            wo = a_w_o[j].astype(BF16)
            q, k, v, kt, vt = _qkv_proj(xp, ln_mix[i], wqkv, keep, ROW_TILE)
            xp = _attn_prompt(q, k, v, xp, wo,
                              _pair_bias(a_rel_bias[j], GROUP_ROWS, GROUP_KEYS, REACH, BAND))
            kp_l.append(kt.reshape(bp, keep, N_HEADS, HEAD_DIM))
            vp_l.append(vt.reshape(bp, keep, N_HEADS, HEAD_DIM))
            q, k, v, kt, vt = _qkv_proj(xs, ln_mix[i], wqkv, bs * ts, bs * ts)
            xs = _attn_sample(
                q.reshape(bs, ts, d), k.reshape(bs, ts, d), v.reshape(bs, ts, d),
                cache_a_k[j].reshape(bs, ca, d), cache_a_v[j].reshape(bs, ca, d),
                xs.reshape(bs, ts, d), wo, _pair_bias(a_rel_bias[j], ts, ca + ts, ca),
            ).reshape(1, bs * ts, d)
            ks_l.append(kt.reshape(bs, ts, N_HEADS, HEAD_DIM))
            vs_l.append(vt.reshape(bs, ts, N_HEADS, HEAD_DIM))
        else:
            win = b_w_in[j].astype(BF16)
            wout = b_w_out[j].astype(BF16)
            pos = jnp.arange(MLP_BLOCK)
            causal = (pos[None, :] // CHUNK) <= (pos[:, None] // CHUNK)
            ws = jnp.where(causal, b_w_s[j], 0.0).astype(BF16)
            g = ws.shape[0]

            def pair_rows(w):
                return w.reshape(g // HEADS_PER_VREG, HEADS_PER_VREG * w.shape[1], w.shape[2])

            def bias_rows(bias):
                return jnp.repeat(bias.T.astype(F32), d // g, axis=1)

            xp = _gmlp(xp, ln_mix[i], win, b_v_norm[j], pair_rows(ws), bias_rows(b_bias_s[j]),
                       wout, ROW_TILE, MLP_BLOCK, False)[0]
            xs, gv = _gmlp(xs, ln_mix[i], win, b_v_norm[j], pair_rows(ws[:, :ts, :ts]),
                           bias_rows(b_bias_s[j][:, :ts]), wout, bs * ts, ts, True)
            gv_l.append(gv.reshape(bs, ts, d))

        wup = f_w_up[i].astype(BF16)
        wdn = f_w_down[i].astype(BF16)
        fin = ln_final if i == depth - 1 else None
        zero_hist = jnp.zeros((bp, 1, SUBLANES, two_ff), F32)
        xp, cp = _conv_ffn(xp, ln_ffn[i], zero_hist, wup, f_conv_w[i], f_conv_b[i], wdn, fin,
                           ROW_TILE, 1)
        xs, cs = _conv_ffn(xs, ln_ffn[i], _pad_hist(state_ffn_conv[i])[None], wup, f_conv_w[i],
                           f_conv_b[i], wdn, fin, bs * ts, bs)
        cp_l.append(cp[:, 0, SUBLANES - hist_rows:, :])
        cs_l.append(cs[0, :, SUBLANES - hist_rows:, :])

    return (xp, xs.reshape(bs, ts, d), jnp.stack(kp_l), jnp.stack(vp_l), jnp.stack(ks_l),
            jnp.stack(vs_l), jnp.stack(gv_l), jnp.stack(cp_l), jnp.stack(cs_l))
```

```python
import functools

import jax
import jax.numpy as jnp
import numpy as np
from jax import lax
from jax.experimental import pallas as pl
from jax.experimental.pallas import tpu as pltpu

F32 = jnp.float32
BF16 = jnp.bfloat16

CHUNK = 64
PAST_CHUNKS = 8
REACH = PAST_CHUNKS * CHUNK
BAND = REACH + CHUNK
MAX_REL = 128
N_HEADS = 16
HEAD_DIM = 64
MLP_BLOCK = 128
N_GROUPS = 16
CONV_W = 3
EPS = 1e-6
NEG = -1e30
LOG2_E = float(np.log2(np.e))
Q_SCALE = HEAD_DIM ** -0.5 * LOG2_E

LANES = 128
SUBLANES = 8
MXU_DIM = 256
HEADS_PER_VREG = LANES // HEAD_DIM
VMEM_LIMIT = 56 * 1024 * 1024

ROW_TILE = 512
FF_CHUNK = MXU_DIM
GROUP_CHUNKS = LANES // CHUNK
GROUP_ROWS = GROUP_CHUNKS * CHUNK
GROUP_KEYS = REACH + GROUP_ROWS
ATTN_GROUPS_PER_TRIP = 2


def _rms(x, g):
    return x * lax.rsqrt(jnp.mean(x * x, axis=-1, keepdims=True) + EPS) * g


def _const_spec(shape):
    nd = len(shape)
    return pl.BlockSpec(shape, lambda *_: (0,) * nd, pipeline_mode=pl.Buffered(1))


def _params(n_grid):
    return pltpu.CompilerParams(
        dimension_semantics=("arbitrary",) * n_grid, vmem_limit_bytes=VMEM_LIMIT)


def _qkv_kernel(x_ref, g_ref, w_ref, q_ref, k_ref, v_ref, kt_ref, vt_ref, *, d, tail_tile0):
    t = pl.program_id(1)
    h = _rms(x_ref[...], g_ref[...]).astype(BF16)
    q = jnp.dot(h, w_ref[:, 0:d], preferred_element_type=F32)
    q_ref[...] = (q * Q_SCALE).astype(BF16)
    k = jnp.dot(h, w_ref[:, d:2 * d], preferred_element_type=F32)
    k_ref[...] = k.astype(BF16)
    v = jnp.dot(h, w_ref[:, 2 * d:3 * d], preferred_element_type=F32)
    v_ref[...] = v.astype(BF16)

    @pl.when(t >= tail_tile0)
    def _():
        kt_ref[...] = k
        vt_ref[...] = v


def _qkv_proj(x, gain, w_bf, keep, tm):
    b, s, d = x.shape
    nt = s // tm
    assert s % tm == 0 and keep % tm == 0
    tail_tile0 = nt - keep // tm
    row = pl.BlockSpec((None, tm, d), lambda i, j: (i, j, 0))
    tail = pl.BlockSpec((None, tm, d), lambda i, j: (i, jnp.maximum(j - tail_tile0, 0), 0))
    return pl.pallas_call(
        functools.partial(_qkv_kernel, d=d, tail_tile0=tail_tile0),
        grid=(b, nt),
        in_specs=[row, _const_spec((1, d)), _const_spec((d, 3 * d))],
        out_specs=[row, row, row, tail, tail],
        out_shape=[jax.ShapeDtypeStruct((b, s, d), BF16)] * 3
        + [jax.ShapeDtypeStruct((b, keep, d), F32)] * 2,
        compiler_params=_params(2),
        name="qkv_proj",
    )(x, gain.reshape(1, d), w_bf)


def _pair_scores(qp, kb):
    lane = lax.broadcasted_iota(jnp.int32, qp.shape, 1)
    qf = qp.astype(F32)
    zero = jnp.zeros_like(qf)
    q_bd = jnp.concatenate(
        [jnp.where(lane < HEAD_DIM, qf, zero), jnp.where(lane >= HEAD_DIM, qf, zero)],
        axis=0).astype(BF16)
    return lax.dot_general(q_bd, kb, (((1,), (1,)), ((), ())), preferred_element_type=F32)


def _pair_softmax(s, bias, lim):
    s = s + bias
    if lim is not None:
        col = lax.broadcasted_iota(jnp.int32, s.shape, 1)
        s = jnp.where(col >= lim, s, NEG)
    m = jnp.max(s, axis=-1, keepdims=True)
    e = jnp.exp2(s - m)
    return e.astype(BF16), 1.0 / jnp.sum(e, axis=-1, keepdims=True)


def _pair_output(e, inv_l, vb):
    nq = e.shape[0] // HEADS_PER_VREG
    lane = lax.broadcasted_iota(jnp.int32, (nq, LANES), 1)
    pv = jnp.dot(e, vb, preferred_element_type=F32) * inv_l
    return jnp.where(lane < HEAD_DIM, pv[:nq], pv[nq:])


def _pipelined_pairs(n, scores, softmax, output, lead_scores=1, lead_softmax=1):
    s, e = {}, {}
    for step in range(n + lead_scores + lead_softmax):
        i_sm = step - lead_scores
        i_out = i_sm - lead_softmax
        if step < n:
            s[step] = scores(step)
        if 0 <= i_sm < n:
            e[i_sm] = softmax(i_sm, s.pop(i_sm))
        if i_out >= 0:
            output(i_out, *e.pop(i_out))


def _attn_prompt_kernel(q_ref, k_ref, v_ref, x_ref, wo_ref, bias_ref, y_ref,
                        kbuf, vbuf, obuf, *, tq):
    t = pl.program_id(1)
    n_pairs = q_ref.shape[1] // LANES

    @pl.when(t == 0)
    def _():
        kbuf[0:tq, :] = jnp.zeros((tq, kbuf.shape[1]), BF16)
        vbuf[0:tq, :] = jnp.zeros((tq, vbuf.shape[1]), BF16)

    @pl.when(t > 0)
    def _():
        kbuf[0:tq, :] = kbuf[tq:2 * tq, :]
        vbuf[0:tq, :] = vbuf[tq:2 * tq, :]

    kbuf[tq:2 * tq, :] = k_ref[...]
    vbuf[tq:2 * tq, :] = v_ref[...]

    def chunk_loop(masked):
        span = ATTN_GROUPS_PER_TRIP * GROUP_ROWS

        def trip_body(i, carry):
            base = pl.multiple_of(i * span, span)

            def item(n):
                g, p = divmod(n, n_pairs)
                return base + g * GROUP_ROWS, slice(p * LANES, (p + 1) * LANES), p

            def scores(n):
                r0, cols, _ = item(n)
                return _pair_scores(q_ref[pl.ds(r0, GROUP_ROWS), cols],
                                    kbuf[pl.ds(r0, GROUP_KEYS), cols])

            def softmax(n, s):
                r0, _, p = item(n)
                lim = (REACH - (t * tq + r0)) if masked else None
                return _pair_softmax(s, bias_ref[p], lim)

            def output(n, e, inv_l):
                r0, cols, _ = item(n)
                o = _pair_output(e, inv_l, vbuf[pl.ds(r0, GROUP_KEYS), cols])
                obuf[pl.ds(r0, GROUP_ROWS), cols] = o.astype(BF16)

            _pipelined_pairs(ATTN_GROUPS_PER_TRIP * n_pairs, scores, softmax, output, 1, 2)
            return carry

        lax.fori_loop(0, tq // span, trip_body, 0)

    @pl.when(t == 0)
    def _():
        chunk_loop(True)

    @pl.when(t > 0)
    def _():
        chunk_loop(False)

    y_ref[...] = x_ref[...] + jnp.dot(obuf[...], wo_ref[...], preferred_element_type=F32)


def _attn_prompt(q, k, v, x, wo_bf, bias_pairs):
    b, s, d = x.shape
    tq = REACH
    assert s % tq == 0
    row = pl.BlockSpec((None, tq, d), lambda i, j: (i, j, 0))
    return pl.pallas_call(
        functools.partial(_attn_prompt_kernel, tq=tq),
        grid=(b, s // tq),
        in_specs=[row, row, row, row, _const_spec((d, d)), _const_spec(bias_pairs.shape)],
        out_specs=row,
        out_shape=jax.ShapeDtypeStruct((b, s, d), F32),
        scratch_shapes=[pltpu.VMEM((2 * tq, d), BF16), pltpu.VMEM((2 * tq, d), BF16),
                        pltpu.VMEM((tq, d), BF16)],
        compiler_params=_params(2),
        name="attn_prompt",
    )(q, k, v, x, wo_bf, bias_pairs)


def _attn_sample_kernel(q_ref, k_ref, v_ref, ck_ref, cv_ref, x_ref, wo_ref, bias_ref, y_ref,
                        obuf):
    n_pairs = q_ref.shape[1] // LANES
    k_all = jnp.concatenate([ck_ref[...].astype(BF16), k_ref[...]], axis=0)
    v_all = jnp.concatenate([cv_ref[...].astype(BF16), v_ref[...]], axis=0)
    cols = [slice(p * LANES, (p + 1) * LANES) for p in range(n_pairs)]

    def scores(p):
        return _pair_scores(q_ref[:, cols[p]], k_all[:, cols[p]])

    def softmax(p, s):
        return _pair_softmax(s, bias_ref[p], None)

    def output(p, e, inv_l):
        obuf[:, cols[p]] = _pair_output(e, inv_l, v_all[:, cols[p]]).astype(BF16)

    _pipelined_pairs(n_pairs, scores, softmax, output)
    y_ref[...] = x_ref[...] + jnp.dot(obuf[...], wo_ref[...], preferred_element_type=F32)


def _attn_sample(q, k, v, cache_k, cache_v, x, wo_bf, bias_pairs):
    b, tn, d = x.shape
    ca = cache_k.shape[1]
    new = pl.BlockSpec((None, tn, d), lambda i: (i, 0, 0))
    old = pl.BlockSpec((None, ca, d), lambda i: (i, 0, 0))
    return pl.pallas_call(
        _attn_sample_kernel,
        grid=(b,),
        in_specs=[new, new, new, old, old, new, _const_spec((d, d)),
                  _const_spec(bias_pairs.shape)],
        out_specs=new,
        out_shape=jax.ShapeDtypeStruct((b, tn, d), F32),
        scratch_shapes=[pltpu.VMEM((tn, d), BF16)],
        compiler_params=_params(1),
        name="attn_sample",
    )(q, k, v, cache_k, cache_v, x, wo_bf, bias_pairs)


def _gmlp_kernel(x_ref, g_ref, win_ref, vg_ref, ws_ref, bs_ref, wout_ref, *rest,
                 blk, emit_v):
    if emit_v:
        y_ref, vout_ref, gated = rest
    else:
        y_ref, gated = rest
    rows, d = x_ref.shape
    n_pairs = d // LANES
    x = x_ref[...]
    h = _rms(x, g_ref[...]).astype(BF16)
    z = jnp.dot(h, win_ref[...], preferred_element_type=F32)
    z = 0.5 * z * (1.0 + lax.erf(z * np.float32(np.sqrt(0.5))))
    u = z[:, :d]
    v = _rms(z[:, d:], vg_ref[...])
    if emit_v:
        vout_ref[...] = v
    v_bf = v.astype(BF16)
    lane = lax.broadcasted_iota(jnp.int32, (blk, LANES), 1)
    for r in range(rows // blk):
        rs = slice(r * blk, (r + 1) * blk)
        for p in range(n_pairs):
            cols = slice(p * LANES, (p + 1) * LANES)
            m = jnp.dot(ws_ref[p], v_bf[rs, cols], preferred_element_type=F32)
            mixed = jnp.where(lane < HEAD_DIM, m[:blk], m[blk:]) + bs_ref[:, cols]
            gated[rs, cols] = (u[rs, cols] * mixed).astype(BF16)
    y_ref[...] = x + jnp.dot(gated[...], wout_ref[...], preferred_element_type=F32)


def _gmlp(x, gain, win_bf, v_gain, ws_pairs, bs_rows, wout_bf, rows, blk, emit_v):
    b, s, d = x.shape
    w = win_bf.shape[1] - d
    assert s % rows == 0 and rows % blk == 0 and w == d
    row = pl.BlockSpec((None, rows, d), lambda i, j: (i, j, 0))
    out_specs = [row]
    out_shape = [jax.ShapeDtypeStruct((b, s, d), F32)]
    if emit_v:
        out_specs.append(row)
        out_shape.append(jax.ShapeDtypeStruct((b, s, w), F32))
    return pl.pallas_call(
        functools.partial(_gmlp_kernel, blk=blk, emit_v=emit_v),
        grid=(b, s // rows),
        in_specs=[row, _const_spec((1, d)), _const_spec(win_bf.shape), _const_spec((1, w)),
                  _const_spec(ws_pairs.shape), _const_spec(bs_rows.shape),
                  _const_spec(wout_bf.shape)],
        out_specs=out_specs,
        out_shape=out_shape,
        scratch_shapes=[pltpu.VMEM((rows, w), BF16)],
        compiler_params=_params(2),
        name="gmlp",
    )(x, gain.reshape(1, d), win_bf, v_gain.reshape(1, w), ws_pairs, bs_rows, wout_bf)


def _ffn_kernel(x_ref, g_ref, hist_ref, wup_ref, cw_ref, cb_ref, wdn_ref, gf_ref,
                y_ref, carry_ref, h_scr, a_scr, act_scr, *, n_sub, final_norm):
    t = pl.program_id(1)
    rows = x_ref.shape[0]
    seq = rows // n_sub
    ff = wdn_ref.shape[0]
    fc = FF_CHUNK

    @pl.when(t == 0)
    def _():
        carry_ref[...] = hist_ref[...]

    x = x_ref[...]
    h_scr[...] = _rms(x, g_ref[...]).astype(BF16)

    def conv(a, cols, slot):
        w = cw_ref[:, cols]
        outs = []
        for s in range(n_sub):
            a_scr[slot, s, 0:SUBLANES, :] = carry_ref[s, :, cols]
            a_scr[slot, s, SUBLANES:SUBLANES + seq, :] = a[s * seq:(s + 1) * seq]
            carry_ref[s, :, cols] = a[(s + 1) * seq - SUBLANES:(s + 1) * seq]
            c = cb_ref[:, cols] + a_scr[slot, s, SUBLANES:SUBLANES + seq, :] * w[2:3]
            c = c + a_scr[slot, s, SUBLANES - 1:SUBLANES - 1 + seq, :] * w[1:2]
            c = c + a_scr[slot, s, SUBLANES - 2:SUBLANES - 2 + seq, :] * w[0:1]
            outs.append(c)
        return outs[0] if n_sub == 1 else jnp.concatenate(outs, axis=0)

    for j in range(ff // fc):
        gcols = slice(j * fc, (j + 1) * fc)
        vcols = slice(ff + j * fc, ff + (j + 1) * fc)
        slot = 2 * (j % 2)
        gate = conv(jnp.dot(h_scr[...], wup_ref[:, gcols], preferred_element_type=F32), gcols,
                    slot)
        val = conv(jnp.dot(h_scr[...], wup_ref[:, vcols], preferred_element_type=F32), vcols,
                   slot + 1)
        act_scr[:, gcols] = (jax.nn.silu(gate) * val).astype(BF16)

    y = x + jnp.dot(act_scr[...], wdn_ref[...], preferred_element_type=F32)
    if final_norm:
        y = _rms(y, gf_ref[...])
    y_ref[...] = y


def _conv_ffn(x, gain, hist, wup_bf, conv_w, conv_b, wdn_bf, final_gain, rows, n_sub):
    b, s, d = x.shape
    two_ff = wup_bf.shape[1]
    ff = two_ff // 2
    seq = rows // n_sub
    assert s % rows == 0 and (n_sub == 1 or s == rows)
    assert seq % SUBLANES == 0 and ff % FF_CHUNK == 0
    final_norm = final_gain is not None
    gf = (final_gain if final_norm else jnp.ones((d,), F32)).reshape(1, d)
    row = pl.BlockSpec((None, rows, d), lambda i, j: (i, j, 0))
    hist_spec = pl.BlockSpec((None,) + hist.shape[1:], lambda i, j: (i, 0, 0, 0))
    return pl.pallas_call(
        functools.partial(_ffn_kernel, n_sub=n_sub, final_norm=final_norm),
        grid=(b, s // rows),
        in_specs=[row, _const_spec((1, d)), hist_spec, _const_spec(wup_bf.shape),
                  _const_spec(conv_w.shape), _const_spec((1, two_ff)),
                  _const_spec(wdn_bf.shape), _const_spec((1, d))],
        out_specs=[row, hist_spec],
        out_shape=[jax.ShapeDtypeStruct((b, s, d), F32),
                   jax.ShapeDtypeStruct(hist.shape, F32)],
        scratch_shapes=[pltpu.VMEM((rows, d), BF16),
                        pltpu.VMEM((4, n_sub, seq + SUBLANES, FF_CHUNK), F32),
                        pltpu.VMEM((rows, ff), BF16)],
        compiler_params=_params(2),
        name="conv_ffn",
    )(x, gain.reshape(1, d), hist, wup_bf, conv_w, conv_b.reshape(1, two_ff), wdn_bf, gf)


def _pair_bias(table, n_q, n_k, reach, band=None):
    diag = jnp.arange(n_q + n_k - 1) - (n_q - 1)
    idx = jnp.clip(reach - diag, -MAX_REL, MAX_REL) + MAX_REL
    r = table[:, idx].astype(F32) * LOG2_E
    h, m = r.shape
    flat = jnp.tile(jnp.pad(r, ((0, 0), (0, 1))), (1, n_q))[:, :n_q * m]
    bias = flat.reshape(h, n_q, m)[:, :, n_q - 1:n_q - 1 + n_k]
    if band is not None:
        first = (jnp.arange(n_q) // CHUNK * CHUNK)[:, None]
        key = jnp.arange(n_k)[None, :]
        bias = jnp.where((key >= first) & (key < first + band), bias, NEG)
    h = bias.shape[0]
    return bias.reshape(h // HEADS_PER_VREG, HEADS_PER_VREG * n_q, n_k)


def _pad_hist(hist):
    pad = [(0, 0)] * (hist.ndim - 2) + [(SUBLANES - hist.shape[-2], 0), (0, 0)]
    return jnp.pad(hist, pad)


def kernel(x_prompt, x_sample, cache_a_k, cache_a_v, state_ffn_conv, ln_mix, ln_ffn, ln_final,
           a_w_qkv, a_rel_bias, a_w_o, b_w_in, b_v_norm, b_w_s, b_bias_s, b_w_out,
           f_w_up, f_conv_w, f_conv_b, f_w_down):
    bp, sp, d = x_prompt.shape
    bs, ts, _ = x_sample.shape
    depth = ln_mix.shape[0]
    ca = cache_a_k.shape[2]
    keep = min(REACH, sp)
    two_ff = f_w_up.shape[2]
    hist_rows = CONV_W - 1

    xp = x_prompt
    xs = x_sample.reshape(1, bs * ts, d)
    kp_l, vp_l, ks_l, vs_l, gv_l, cp_l, cs_l = [], [], [], [], [], [], []

    for i in range(depth):
        j = i // 2
        if i % 2 == 0:
            wqkv = a_w_qkv[j].astype(BF16)
            wo = a_w_o[j].astype(BF16)
            q, k, v, kt, vt = _qkv_proj(xp, ln_mix[i], wqkv, keep, ROW_TILE)
            xp = _attn_prompt(q, k, v, xp, wo,
                              _pair_bias(a_rel_bias[j], GROUP_ROWS, GROUP_KEYS, REACH, BAND))
            kp_l.append(kt.reshape(bp, keep, N_HEADS, HEAD_DIM))
            vp_l.append(vt.reshape(bp, keep, N_HEADS, HEAD_DIM))
            q, k, v, kt, vt = _qkv_proj(xs, ln_mix[i], wqkv, bs * ts, bs * ts)
            xs = _attn_sample(
                q.reshape(bs, ts, d), k.reshape(bs, ts, d), v.reshape(bs, ts, d),
                cache_a_k[j].reshape(bs, ca, d), cache_a_v[j].reshape(bs, ca, d),
                xs.reshape(bs, ts, d), wo, _pair_bias(a_rel_bias[j], ts, ca + ts, ca),
            ).reshape(1, bs * ts, d)
            ks_l.append(kt.reshape(bs, ts, N_HEADS, HEAD_DIM))
            vs_l.append(vt.reshape(bs, ts, N_HEADS, HEAD_DIM))
        else:
            win = b_w_in[j].astype(BF16)
            wout = b_w_out[j].astype(BF16)
            pos = jnp.arange(MLP_BLOCK)
            causal = (pos[None, :] // CHUNK) <= (pos[:, None] // CHUNK)
            ws = jnp.where(causal, b_w_s[j], 0.0).astype(BF16)
            g = ws.shape[0]

            def pair_rows(w):
                return w.reshape(g // HEADS_PER_VREG, HEADS_PER_VREG * w.shape[1], w.shape[2])

            def bias_rows(bias):
                return jnp.repeat(bias.T.astype(F32), d // g, axis=1)

            xp = _gmlp(xp, ln_mix[i], win, b_v_norm[j], pair_rows(ws), bias_rows(b_bias_s[j]),
                       wout, ROW_TILE, MLP_BLOCK, False)[0]
            xs, gv = _gmlp(xs, ln_mix[i], win, b_v_norm[j], pair_rows(ws[:, :ts, :ts]),
                           bias_rows(b_bias_s[j][:, :ts]), wout, bs * ts, ts, True)
            gv_l.append(gv.reshape(bs, ts, d))

        wup = f_w_up[i].astype(BF16)
        wdn = f_w_down[i].astype(BF16)
        fin = ln_final if i == depth - 1 else None
        zero_hist = jnp.zeros((bp, 1, SUBLANES, two_ff), F32)
        xp, cp = _conv_ffn(xp, ln_ffn[i], zero_hist, wup, f_conv_w[i], f_conv_b[i], wdn, fin,
                           ROW_TILE, 1)
        xs, cs = _conv_ffn(xs, ln_ffn[i], _pad_hist(state_ffn_conv[i])[None], wup, f_conv_w[i],
                           f_conv_b[i], wdn, fin, bs * ts, bs)
        cp_l.append(cp[:, 0, SUBLANES - hist_rows:, :])
        cs_l.append(cs[0, :, SUBLANES - hist_rows:, :])

    return (xp, xs.reshape(bs, ts, d), jnp.stack(kp_l), jnp.stack(vp_l), jnp.stack(ks_l),
            jnp.stack(vs_l), jnp.stack(gv_l), jnp.stack(cp_l), jnp.stack(cs_l))
```

```python
import functools

import jax
import jax.numpy as jnp
import numpy as np
from jax import lax
from jax.experimental import pallas as pl
from jax.experimental.pallas import tpu as pltpu

F32 = jnp.float32
BF16 = jnp.bfloat16

CHUNK = 64
PAST_CHUNKS = 8
REACH = PAST_CHUNKS * CHUNK
BAND = REACH + CHUNK
MAX_REL = 128
N_HEADS = 16
HEAD_DIM = 64
MLP_BLOCK = 128
N_GROUPS = 16
CONV_W = 3
EPS = 1e-6
NEG = -1e30
LOG2_E = float(np.log2(np.e))
Q_SCALE = HEAD_DIM ** -0.5 * LOG2_E

LANES = 128
SUBLANES = 8
MXU_DIM = 256
HEADS_PER_VREG = LANES // HEAD_DIM
VMEM_LIMIT = 56 * 1024 * 1024

ROW_TILE = 512
FFN_ROW_TILE = 512
FF_CHUNK = MXU_DIM
GROUP_CHUNKS = LANES // CHUNK
GROUP_ROWS = GROUP_CHUNKS * CHUNK
GROUP_KEYS = REACH + GROUP_ROWS
ATTN_GROUPS_PER_TRIP = 2


def _rms(x, g):
    return x * lax.rsqrt(jnp.mean(x * x, axis=-1, keepdims=True) + EPS) * g


def _const_spec(shape):
    nd = len(shape)
    return pl.BlockSpec(shape, lambda *_: (0,) * nd, pipeline_mode=pl.Buffered(1))


def _params(n_grid):
    return pltpu.CompilerParams(
        dimension_semantics=("arbitrary",) * n_grid, vmem_limit_bytes=VMEM_LIMIT)


def _qkv_kernel(x_ref, g_ref, w_ref, q_ref, k_ref, v_ref, kt_ref, vt_ref, *, d, tail_tile0):
    t = pl.program_id(1)
    h = _rms(x_ref[...], g_ref[...]).astype(BF16)
    q = jnp.dot(h, w_ref[:, 0:d].astype(BF16), preferred_element_type=F32)
    q_ref[...] = (q * Q_SCALE).astype(BF16)
    k = jnp.dot(h, w_ref[:, d:2 * d].astype(BF16), preferred_element_type=F32)
    k_ref[...] = k.astype(BF16)
    v = jnp.dot(h, w_ref[:, 2 * d:3 * d].astype(BF16), preferred_element_type=F32)
    v_ref[...] = v.astype(BF16)

    @pl.when(t >= tail_tile0)
    def _():
        kt_ref[...] = k
        vt_ref[...] = v


def _qkv_proj(x, gain, w_bf, keep, tm):
    b, s, d = x.shape
    nt = s // tm
    assert s % tm == 0 and keep % tm == 0
    tail_tile0 = nt - keep // tm
    row = pl.BlockSpec((None, tm, d), lambda i, j: (i, j, 0))
    tail = pl.BlockSpec((None, tm, d), lambda i, j: (i, jnp.maximum(j - tail_tile0, 0), 0))
    return pl.pallas_call(
        functools.partial(_qkv_kernel, d=d, tail_tile0=tail_tile0),
        grid=(b, nt),
        in_specs=[row, _const_spec((1, d)), _const_spec((d, 3 * d))],
        out_specs=[row, row, row, tail, tail],
        out_shape=[jax.ShapeDtypeStruct((b, s, d), BF16)] * 3
        + [jax.ShapeDtypeStruct((b, keep, d), F32)] * 2,
        compiler_params=_params(2),
        name="qkv_proj",
    )(x, gain.reshape(1, d), w_bf)


def _pair_scores(qp, kb):
    lane = lax.broadcasted_iota(jnp.int32, qp.shape, 1)
    qf = qp.astype(F32)
    zero = jnp.zeros_like(qf)
    q_bd = jnp.concatenate(
        [jnp.where(lane < HEAD_DIM, qf, zero), jnp.where(lane >= HEAD_DIM, qf, zero)],
        axis=0).astype(BF16)
    return lax.dot_general(q_bd, kb, (((1,), (1,)), ((), ())), preferred_element_type=F32)


def _pair_softmax(s, bias, lim):
    s = s + bias
    if lim is not None:
        col = lax.broadcasted_iota(jnp.int32, s.shape, 1)
        s = jnp.where(col >= lim, s, NEG)
    m = jnp.max(s, axis=-1, keepdims=True)
    e = jnp.exp2(s - m)
    return e.astype(BF16), 1.0 / jnp.sum(e, axis=-1, keepdims=True)


def _pair_output(e, inv_l, vb):
    nq = e.shape[0] // HEADS_PER_VREG
    lane = lax.broadcasted_iota(jnp.int32, (nq, LANES), 1)
    pv = jnp.dot(e, vb, preferred_element_type=F32) * inv_l
    return jnp.where(lane < HEAD_DIM, pv[:nq], pv[nq:])


def _pipelined_pairs(n, scores, softmax, output, lead_scores=1, lead_softmax=1):
    s, e = {}, {}
    for step in range(n + lead_scores + lead_softmax):
        i_sm = step - lead_scores
        i_out = i_sm - lead_softmax
        if step < n:
            s[step] = scores(step)
        if 0 <= i_sm < n:
            e[i_sm] = softmax(i_sm, s.pop(i_sm))
        if i_out >= 0:
            output(i_out, *e.pop(i_out))


def _attn_prompt_kernel(q_ref, k_ref, v_ref, x_ref, wo_ref, bias_ref, y_ref,
                        kbuf, vbuf, obuf, *, tq):
    t = pl.program_id(1)
    n_pairs = q_ref.shape[1] // LANES

    @pl.when(t == 0)
    def _():
        kbuf[0:tq, :] = jnp.zeros((tq, kbuf.shape[1]), BF16)
        vbuf[0:tq, :] = jnp.zeros((tq, vbuf.shape[1]), BF16)

    @pl.when(t > 0)
    def _():
        kbuf[0:tq, :] = kbuf[tq:2 * tq, :]
        vbuf[0:tq, :] = vbuf[tq:2 * tq, :]

    kbuf[tq:2 * tq, :] = k_ref[...]
    vbuf[tq:2 * tq, :] = v_ref[...]

    def chunk_loop(masked):
        span = ATTN_GROUPS_PER_TRIP * GROUP_ROWS

        def trip_body(i, carry):
            base = pl.multiple_of(i * span, span)

            def item(n):
                g, p = divmod(n, n_pairs)
                return base + g * GROUP_ROWS, slice(p * LANES, (p + 1) * LANES), p

            def scores(n):
                r0, cols, _ = item(n)
                return _pair_scores(q_ref[pl.ds(r0, GROUP_ROWS), cols],
                                    kbuf[pl.ds(r0, GROUP_KEYS), cols])

            def softmax(n, s):
                r0, _, p = item(n)
                lim = (REACH - (t * tq + r0)) if masked else None
                return _pair_softmax(s, bias_ref[p], lim)

            def output(n, e, inv_l):
                r0, cols, _ = item(n)
                o = _pair_output(e, inv_l, vbuf[pl.ds(r0, GROUP_KEYS), cols])
                obuf[pl.ds(r0, GROUP_ROWS), cols] = o.astype(BF16)

            _pipelined_pairs(ATTN_GROUPS_PER_TRIP * n_pairs, scores, softmax, output, 1, 2)
            return carry

        lax.fori_loop(0, tq // span, trip_body, 0)

    @pl.when(t == 0)
    def _():
        chunk_loop(True)

    @pl.when(t > 0)
    def _():
        chunk_loop(False)

    y_ref[...] = x_ref[...] + jnp.dot(obuf[...], wo_ref[...].astype(BF16),
                                      preferred_element_type=F32)


def _attn_prompt(q, k, v, x, wo_bf, bias_pairs):
    b, s, d = x.shape
    tq = REACH
    assert s % tq == 0
    row = pl.BlockSpec((None, tq, d), lambda i, j: (i, j, 0))
    return pl.pallas_call(
        functools.partial(_attn_prompt_kernel, tq=tq),
        grid=(b, s // tq),
        in_specs=[row, row, row, row, _const_spec((d, d)), _const_spec(bias_pairs.shape)],
        out_specs=row,
        out_shape=jax.ShapeDtypeStruct((b, s, d), F32),
        scratch_shapes=[pltpu.VMEM((2 * tq, d), BF16), pltpu.VMEM((2 * tq, d), BF16),
                        pltpu.VMEM((tq, d), BF16)],
        compiler_params=_params(2),
        name="attn_prompt",
    )(q, k, v, x, wo_bf, bias_pairs)


def _attn_sample_kernel(q_ref, k_ref, v_ref, ck_ref, cv_ref, x_ref, wo_ref, bias_ref, y_ref,
                        obuf):
    n_pairs = q_ref.shape[1] // LANES
    k_all = jnp.concatenate([ck_ref[...].astype(BF16), k_ref[...]], axis=0)
    v_all = jnp.concatenate([cv_ref[...].astype(BF16), v_ref[...]], axis=0)
    cols = [slice(p * LANES, (p + 1) * LANES) for p in range(n_pairs)]

    def scores(p):
        return _pair_scores(q_ref[:, cols[p]], k_all[:, cols[p]])

    def softmax(p, s):
        return _pair_softmax(s, bias_ref[p], None)

    def output(p, e, inv_l):
        obuf[:, cols[p]] = _pair_output(e, inv_l, v_all[:, cols[p]]).astype(BF16)

    _pipelined_pairs(n_pairs, scores, softmax, output)
    y_ref[...] = x_ref[...] + jnp.dot(obuf[...], wo_ref[...].astype(BF16),
                                      preferred_element_type=F32)


def _attn_sample(q, k, v, cache_k, cache_v, x, wo_bf, bias_pairs):
    b, tn, d = x.shape
    ca = cache_k.shape[1]
    new = pl.BlockSpec((None, tn, d), lambda i: (i, 0, 0))
    old = pl.BlockSpec((None, ca, d), lambda i: (i, 0, 0))
    return pl.pallas_call(
        _attn_sample_kernel,
        grid=(b,),
        in_specs=[new, new, new, old, old, new, _const_spec((d, d)),
                  _const_spec(bias_pairs.shape)],
        out_specs=new,
        out_shape=jax.ShapeDtypeStruct((b, tn, d), F32),
        scratch_shapes=[pltpu.VMEM((tn, d), BF16)],
        compiler_params=_params(1),
        name="attn_sample",
    )(q, k, v, cache_k, cache_v, x, wo_bf, bias_pairs)


def _gmlp_kernel(x_ref, g_ref, win_ref, vg_ref, ws_ref, bs_ref, wout_ref, *rest,
                 blk, emit_v):
    if emit_v:
        y_ref, vout_ref, gated = rest
    else:
        y_ref, gated = rest
    rows, d = x_ref.shape
    n_pairs = d // LANES
    x = x_ref[...]
    h = _rms(x, g_ref[...]).astype(BF16)
    z = jnp.dot(h, win_ref[...].astype(BF16), preferred_element_type=F32)
    z = 0.5 * z * (1.0 + lax.erf(z * np.float32(np.sqrt(0.5))))
    u = z[:, :d]
    v = _rms(z[:, d:], vg_ref[...])
    if emit_v:
        vout_ref[...] = v
    v_bf = v.astype(BF16)
    lane = lax.broadcasted_iota(jnp.int32, (blk, LANES), 1)
    for r in range(rows // blk):
        rs = slice(r * blk, (r + 1) * blk)
        for p in range(n_pairs):
            cols = slice(p * LANES, (p + 1) * LANES)
            m = jnp.dot(ws_ref[p], v_bf[rs, cols], preferred_element_type=F32)
            mixed = jnp.where(lane < HEAD_DIM, m[:blk], m[blk:]) + bs_ref[:, cols]
            gated[rs, cols] = (u[rs, cols] * mixed).astype(BF16)
    y_ref[...] = x + jnp.dot(gated[...], wout_ref[...].astype(BF16), preferred_element_type=F32)


def _gmlp(x, gain, win_bf, v_gain, ws_pairs, bs_rows, wout_bf, rows, blk, emit_v):
    b, s, d = x.shape
    w = win_bf.shape[1] - d
    assert s % rows == 0 and rows % blk == 0 and w == d
    row = pl.BlockSpec((None, rows, d), lambda i, j: (i, j, 0))
    out_specs = [row]
    out_shape = [jax.ShapeDtypeStruct((b, s, d), F32)]
    if emit_v:
        out_specs.append(row)
        out_shape.append(jax.ShapeDtypeStruct((b, s, w), F32))
    return pl.pallas_call(
        functools.partial(_gmlp_kernel, blk=blk, emit_v=emit_v),
        grid=(b, s // rows),
        in_specs=[row, _const_spec((1, d)), _const_spec(win_bf.shape), _const_spec((1, w)),
                  _const_spec(ws_pairs.shape), _const_spec(bs_rows.shape),
                  _const_spec(wout_bf.shape)],
        out_specs=out_specs,
        out_shape=out_shape,
        scratch_shapes=[pltpu.VMEM((rows, w), BF16)],
        compiler_params=_params(2),
        name="gmlp",
    )(x, gain.reshape(1, d), win_bf, v_gain.reshape(1, w), ws_pairs, bs_rows, wout_bf)


def _ffn_kernel(x_ref, g_ref, hist_ref, wup_ref, cw_ref, cb_ref, wdn_ref, gf_ref,
                y_ref, carry_ref, h_scr, a_scr, act_scr, *, n_sub, final_norm):
    t = pl.program_id(1)
    rows = x_ref.shape[0]
    seq = rows // n_sub
    ff = wdn_ref.shape[0]
    fc = FF_CHUNK

    @pl.when(t == 0)
    def _():
        carry_ref[...] = hist_ref[...]

    x = x_ref[...]
    h_scr[...] = _rms(x, g_ref[...]).astype(BF16)

    def cols_of(j):
        return (slice(j * fc, (j + 1) * fc), slice(ff + j * fc, ff + (j + 1) * fc))

    def up(j):
        for half, cols in enumerate(cols_of(j)):
            slot = 2 * (j % 2) + half
            a = jnp.dot(h_scr[...], wup_ref[:, cols], preferred_element_type=F32)
            for s in range(n_sub):
                a_scr[slot, s, 0:SUBLANES, :] = carry_ref[s, :, cols]
                a_scr[slot, s, SUBLANES:SUBLANES + seq, :] = a[s * seq:(s + 1) * seq]
                carry_ref[s, :, cols] = a[(s + 1) * seq - SUBLANES:(s + 1) * seq]
        return None

    def conv(j, half):
        cols = cols_of(j)[half]
        slot = 2 * (j % 2) + half
        w = cw_ref[:, cols]
        outs = []
        for s in range(n_sub):
            c = cb_ref[:, cols] + a_scr[slot, s, SUBLANES:SUBLANES + seq, :] * w[2:3]
            c = c + a_scr[slot, s, SUBLANES - 1:SUBLANES - 1 + seq, :] * w[1:2]
            c = c + a_scr[slot, s, SUBLANES - 2:SUBLANES - 2 + seq, :] * w[0:1]
            outs.append(c)
        return outs[0] if n_sub == 1 else jnp.concatenate(outs, axis=0)

    for j in range(ff // fc):
        up(j)
        act_scr[:, cols_of(j)[0]] = (jax.nn.silu(conv(j, 0)) * conv(j, 1)).astype(BF16)

    y = x + jnp.dot(act_scr[...], wdn_ref[...], preferred_element_type=F32)
    if final_norm:
        y = _rms(y, gf_ref[...])
    y_ref[...] = y


def _layer_spec(shape, layer):
    nd = len(shape)
    return pl.BlockSpec((None,) + tuple(shape[1:]), lambda *_: (layer,) + (0,) * (nd - 1),
                        pipeline_mode=pl.Buffered(1))


def _conv_ffn(x, gain, hist, wup_bf, conv_w, conv_b, wdn_bf, layer, final_gain, rows, n_sub):
    b, s, d = x.shape
    two_ff = wup_bf.shape[2]
    seq = rows // n_sub
    assert s % rows == 0 and (n_sub == 1 or s == rows)
    assert seq % SUBLANES == 0 and (two_ff // 2) % FF_CHUNK == 0
    final_norm = final_gain is not None
    gf = (final_gain if final_norm else jnp.ones((d,), F32)).reshape(1, d)
    row = pl.BlockSpec((None, rows, d), lambda i, j: (i, j, 0))
    hist_spec = pl.BlockSpec((None,) + hist.shape[1:], lambda i, j: (i, 0, 0, 0))
    return pl.pallas_call(
        functools.partial(_ffn_kernel, n_sub=n_sub, final_norm=final_norm),
        grid=(b, s // rows),
        in_specs=[row, _const_spec((1, d)), hist_spec, _layer_spec(wup_bf.shape, layer),
                  _layer_spec(conv_w.shape, layer), _layer_spec(conv_b.shape, layer),
                  _layer_spec(wdn_bf.shape, layer), _const_spec((1, d))],
        out_specs=[row, hist_spec],
        out_shape=[jax.ShapeDtypeStruct((b, s, d), F32),
                   jax.ShapeDtypeStruct(hist.shape, F32)],
        scratch_shapes=[pltpu.VMEM((rows, d), BF16),
                        pltpu.VMEM((4, n_sub, seq + SUBLANES, FF_CHUNK), F32),
                        pltpu.VMEM((rows, two_ff // 2), BF16)],
        compiler_params=_params(2),
        name="conv_ffn",
    )(x, gain.reshape(1, d), hist, wup_bf, conv_w, conv_b, wdn_bf, gf)


def _pair_bias(table, n_q, n_k, reach, band=None):
    diag = jnp.arange(n_q + n_k - 1) - (n_q - 1)
    idx = jnp.clip(reach - diag, -MAX_REL, MAX_REL) + MAX_REL
    r = table[:, idx].astype(F32) * LOG2_E
    h, m = r.shape
    flat = jnp.tile(jnp.pad(r, ((0, 0), (0, 1))), (1, n_q))[:, :n_q * m]
    bias = flat.reshape(h, n_q, m)[:, :, n_q - 1:n_q - 1 + n_k]
    if band is not None:
        first = (jnp.arange(n_q) // CHUNK * CHUNK)[:, None]
        key = jnp.arange(n_k)[None, :]
        bias = jnp.where((key >= first) & (key < first + band), bias, NEG)
    h = bias.shape[0]
    return bias.reshape(h // HEADS_PER_VREG, HEADS_PER_VREG * n_q, n_k)


def _pad_hist(hist):
    pad = [(0, 0)] * (hist.ndim - 2) + [(SUBLANES - hist.shape[-2], 0), (0, 0)]
    return jnp.pad(hist, pad)


def kernel(x_prompt, x_sample, cache_a_k, cache_a_v, state_ffn_conv, ln_mix, ln_ffn, ln_final,
           a_w_qkv, a_rel_bias, a_w_o, b_w_in, b_v_norm, b_w_s, b_bias_s, b_w_out,
           f_w_up, f_conv_w, f_conv_b, f_w_down):
    bp, sp, d = x_prompt.shape
    bs, ts, _ = x_sample.shape
    depth = ln_mix.shape[0]
    ca = cache_a_k.shape[2]
    keep = min(REACH, sp)
    two_ff = f_w_up.shape[2]
    hist_rows = CONV_W - 1
    wup_all = f_w_up.astype(BF16)
    wdn_all = f_w_down.astype(BF16)
    conv_b_all = f_conv_b.reshape(depth, 1, two_ff)

    xp = x_prompt
    xs = x_sample.reshape(1, bs * ts, d)
    kp_l, vp_l, ks_l, vs_l, gv_l, cp_l, cs_l = [], [], [], [], [], [], []

    for i in range(depth):
        j = i // 2
        if i % 2 == 0:
            wqkv = a_w_qkv[j]
            wo = a_w_o[j]
            q, k, v, kt, vt = _qkv_proj(xp, ln_mix[i], wqkv, keep, ROW_TILE)
            xp = _attn_prompt(q, k, v, xp, wo,
                              _pair_bias(a_rel_bias[j], GROUP_ROWS, GROUP_KEYS, REACH, BAND))
            kp_l.append(kt.reshape(bp, keep, N_HEADS, HEAD_DIM))
            vp_l.append(vt.reshape(bp, keep, N_HEADS, HEAD_DIM))
            q, k, v, kt, vt = _qkv_proj(xs, ln_mix[i], wqkv, bs * ts, bs * ts)
            xs = _attn_sample(
                q.reshape(bs, ts, d), k.reshape(bs, ts, d), v.reshape(bs, ts, d),
                cache_a_k[j].reshape(bs, ca, d), cache_a_v[j].reshape(bs, ca, d),
                xs.reshape(bs, ts, d), wo, _pair_bias(a_rel_bias[j], ts, ca + ts, ca),
            ).reshape(1, bs * ts, d)
            ks_l.append(kt.reshape(bs, ts, N_HEADS, HEAD_DIM))
            vs_l.append(vt.reshape(bs, ts, N_HEADS, HEAD_DIM))
        else:
            win = b_w_in[j]
            wout = b_w_out[j]
            pos = jnp.arange(MLP_BLOCK)
            causal = (pos[None, :] // CHUNK) <= (pos[:, None] // CHUNK)
            ws = jnp.where(causal, b_w_s[j], 0.0).astype(BF16)
            g = ws.shape[0]

            def pair_rows(w):
                return w.reshape(g // HEADS_PER_VREG, HEADS_PER_VREG * w.shape[1], w.shape[2])

            def bias_rows(bias):
                return jnp.repeat(bias.T.astype(F32), d // g, axis=1)

            xp = _gmlp(xp, ln_mix[i], win, b_v_norm[j], pair_rows(ws), bias_rows(b_bias_s[j]),
                       wout, ROW_TILE, MLP_BLOCK, False)[0]
            xs, gv = _gmlp(xs, ln_mix[i], win, b_v_norm[j], pair_rows(ws[:, :ts, :ts]),
                           bias_rows(b_bias_s[j][:, :ts]), wout, bs * ts, ts, True)
            gv_l.append(gv.reshape(bs, ts, d))

        fin = ln_final if i == depth - 1 else None
        zero_hist = jnp.zeros((bp, 1, SUBLANES, two_ff), F32)
        xp, cp = _conv_ffn(xp, ln_ffn[i], zero_hist, wup_all, f_conv_w, conv_b_all, wdn_all, i,
                           fin, FFN_ROW_TILE, 1)
        xs, cs = _conv_ffn(xs, ln_ffn[i], _pad_hist(state_ffn_conv[i])[None], wup_all, f_conv_w,
                           conv_b_all, wdn_all, i, fin, bs * ts, bs)
        cp_l.append(cp[:, 0, SUBLANES - hist_rows:, :])
        cs_l.append(cs[0, :, SUBLANES - hist_rows:, :])

    return (xp, xs.reshape(bs, ts, d), jnp.stack(kp_l), jnp.stack(vp_l), jnp.stack(ks_l),
            jnp.stack(vs_l), jnp.stack(gv_l), jnp.stack(cp_l), jnp.stack(cs_l))
```

```python
import functools

import jax
import jax.numpy as jnp
import numpy as np
from jax import lax
from jax.experimental import pallas as pl
from jax.experimental.pallas import tpu as pltpu

F32 = jnp.float32
BF16 = jnp.bfloat16

CHUNK = 64
PAST_CHUNKS = 8
REACH = PAST_CHUNKS * CHUNK
BAND = REACH + CHUNK
MAX_REL = 128
N_HEADS = 16
HEAD_DIM = 64
MLP_BLOCK = 128
N_GROUPS = 16
CONV_W = 3
EPS = 1e-6
NEG = -1e30
LOG2_E = float(np.log2(np.e))
Q_SCALE = HEAD_DIM ** -0.5 * LOG2_E

LANES = 128
SUBLANES = 8
MXU_DIM = 256
HEADS_PER_VREG = LANES // HEAD_DIM
VMEM_LIMIT = 56 * 1024 * 1024

ROW_TILE = 512
FFN_ROW_TILE = 512
FF_CHUNK = MXU_DIM
GROUP_CHUNKS = LANES // CHUNK
GROUP_ROWS = GROUP_CHUNKS * CHUNK
GROUP_KEYS = REACH + GROUP_ROWS
ATTN_GROUPS_PER_TRIP = 2


def _rms(x, g):
    return x * lax.rsqrt(jnp.mean(x * x, axis=-1, keepdims=True) + EPS) * g


def _const_spec(shape):
    nd = len(shape)
    return pl.BlockSpec(shape, lambda *_: (0,) * nd, pipeline_mode=pl.Buffered(1))


def _params(n_grid):
    return pltpu.CompilerParams(
        dimension_semantics=("arbitrary",) * n_grid, vmem_limit_bytes=VMEM_LIMIT)


def _qkv_kernel(x_ref, g_ref, w_ref, q_ref, k_ref, v_ref, kt_ref, vt_ref, *, d, tail_tile0):
    t = pl.program_id(1)
    h = _rms(x_ref[...], g_ref[...]).astype(BF16)
    q = jnp.dot(h, w_ref[:, 0:d].astype(BF16), preferred_element_type=F32)
    q_ref[...] = (q * Q_SCALE).astype(BF16)
    k = jnp.dot(h, w_ref[:, d:2 * d].astype(BF16), preferred_element_type=F32)
    k_ref[...] = k.astype(BF16)
    v = jnp.dot(h, w_ref[:, 2 * d:3 * d].astype(BF16), preferred_element_type=F32)
    v_ref[...] = v.astype(BF16)

    @pl.when(t >= tail_tile0)
    def _():
        kt_ref[...] = k
        vt_ref[...] = v


def _qkv_proj(x, gain, w_bf, keep, tm):
    b, s, d = x.shape
    nt = s // tm
    assert s % tm == 0 and keep % tm == 0
    tail_tile0 = nt - keep // tm
    row = pl.BlockSpec((None, tm, d), lambda i, j: (i, j, 0))
    tail = pl.BlockSpec((None, tm, d), lambda i, j: (i, jnp.maximum(j - tail_tile0, 0), 0))
    return pl.pallas_call(
        functools.partial(_qkv_kernel, d=d, tail_tile0=tail_tile0),
        grid=(b, nt),
        in_specs=[row, _const_spec((1, d)), _const_spec((d, 3 * d))],
        out_specs=[row, row, row, tail, tail],
        out_shape=[jax.ShapeDtypeStruct((b, s, d), BF16)] * 3
        + [jax.ShapeDtypeStruct((b, keep, d), F32)] * 2,
        compiler_params=_params(2),
        name="qkv_proj",
    )(x, gain.reshape(1, d), w_bf)


def _pair_scores(qp, kb):
    lane = lax.broadcasted_iota(jnp.int32, qp.shape, 1)
    qf = qp.astype(F32)
    zero = jnp.zeros_like(qf)
    q_bd = jnp.concatenate(
        [jnp.where(lane < HEAD_DIM, qf, zero), jnp.where(lane >= HEAD_DIM, qf, zero)],
        axis=0).astype(BF16)
    return lax.dot_general(q_bd, kb, (((1,), (1,)), ((), ())), preferred_element_type=F32)


def _pair_softmax(s, bias, lim):
    s = s + bias
    if lim is not None:
        col = lax.broadcasted_iota(jnp.int32, s.shape, 1)
        s = jnp.where(col >= lim, s, NEG)
    m = jnp.max(s, axis=-1, keepdims=True)
    e = jnp.exp2(s - m)
    return e.astype(BF16), 1.0 / jnp.sum(e, axis=-1, keepdims=True)


def _pair_output(e, inv_l, vb):
    nq = e.shape[0] // HEADS_PER_VREG
    lane = lax.broadcasted_iota(jnp.int32, (nq, LANES), 1)
    pv = jnp.dot(e, vb, preferred_element_type=F32) * inv_l
    return jnp.where(lane < HEAD_DIM, pv[:nq], pv[nq:])


def _pipelined_pairs(n, scores, softmax, output, lead_scores=1, lead_softmax=1):
    s, e = {}, {}
    for step in range(n + lead_scores + lead_softmax):
        i_sm = step - lead_scores
        i_out = i_sm - lead_softmax
        if step < n:
            s[step] = scores(step)
        if 0 <= i_sm < n:
            e[i_sm] = softmax(i_sm, s.pop(i_sm))
        if i_out >= 0:
            output(i_out, *e.pop(i_out))


def _attn_prompt_kernel(q_ref, k_ref, v_ref, x_ref, wo_ref, bias_ref, y_ref,
                        kbuf, vbuf, obuf, *, tq):
    t = pl.program_id(1)
    n_pairs = q_ref.shape[1] // LANES

    @pl.when(t == 0)
    def _():
        kbuf[0:tq, :] = jnp.zeros((tq, kbuf.shape[1]), BF16)
        vbuf[0:tq, :] = jnp.zeros((tq, vbuf.shape[1]), BF16)

    @pl.when(t > 0)
    def _():
        kbuf[0:tq, :] = kbuf[tq:2 * tq, :]
        vbuf[0:tq, :] = vbuf[tq:2 * tq, :]

    kbuf[tq:2 * tq, :] = k_ref[...]
    vbuf[tq:2 * tq, :] = v_ref[...]

    def chunk_loop(masked):
        span = ATTN_GROUPS_PER_TRIP * GROUP_ROWS

        def trip_body(i, carry):
            base = pl.multiple_of(i * span, span)

            def item(n):
                g, p = divmod(n, n_pairs)
                return base + g * GROUP_ROWS, slice(p * LANES, (p + 1) * LANES), p

            def scores(n):
                r0, cols, _ = item(n)
                return _pair_scores(q_ref[pl.ds(r0, GROUP_ROWS), cols],
                                    kbuf[pl.ds(r0, GROUP_KEYS), cols])

            def softmax(n, s):
                r0, _, p = item(n)
                lim = (REACH - (t * tq + r0)) if masked else None
                return _pair_softmax(s, bias_ref[p], lim)

            def output(n, e, inv_l):
                r0, cols, _ = item(n)
                o = _pair_output(e, inv_l, vbuf[pl.ds(r0, GROUP_KEYS), cols])
                obuf[pl.ds(r0, GROUP_ROWS), cols] = o.astype(BF16)

            _pipelined_pairs(ATTN_GROUPS_PER_TRIP * n_pairs, scores, softmax, output, 1, 2)
            return carry

        lax.fori_loop(0, tq // span, trip_body, 0)

    @pl.when(t == 0)
    def _():
        chunk_loop(True)

    @pl.when(t > 0)
    def _():
        chunk_loop(False)

    y_ref[...] = x_ref[...] + jnp.dot(obuf[...], wo_ref[...].astype(BF16),
                                      preferred_element_type=F32)


def _attn_prompt(q, k, v, x, wo_bf, bias_pairs):
    b, s, d = x.shape
    tq = REACH
    assert s % tq == 0
    row = pl.BlockSpec((None, tq, d), lambda i, j: (i, j, 0))
    return pl.pallas_call(
        functools.partial(_attn_prompt_kernel, tq=tq),
        grid=(b, s // tq),
        in_specs=[row, row, row, row, _const_spec((d, d)), _const_spec(bias_pairs.shape)],
        out_specs=row,
        out_shape=jax.ShapeDtypeStruct((b, s, d), F32),
        scratch_shapes=[pltpu.VMEM((2 * tq, d), BF16), pltpu.VMEM((2 * tq, d), BF16),
                        pltpu.VMEM((tq, d), BF16)],
        compiler_params=_params(2),
        name="attn_prompt",
    )(q, k, v, x, wo_bf, bias_pairs)


def _attn_sample_kernel(q_ref, k_ref, v_ref, ck_ref, cv_ref, x_ref, wo_ref, bias_ref, y_ref,
                        obuf):
    n_pairs = q_ref.shape[1] // LANES
    k_all = jnp.concatenate([ck_ref[...].astype(BF16), k_ref[...]], axis=0)
    v_all = jnp.concatenate([cv_ref[...].astype(BF16), v_ref[...]], axis=0)
    cols = [slice(p * LANES, (p + 1) * LANES) for p in range(n_pairs)]

    def scores(p):
        return _pair_scores(q_ref[:, cols[p]], k_all[:, cols[p]])

    def softmax(p, s):
        return _pair_softmax(s, bias_ref[p], None)

    def output(p, e, inv_l):
        obuf[:, cols[p]] = _pair_output(e, inv_l, v_all[:, cols[p]]).astype(BF16)

    _pipelined_pairs(n_pairs, scores, softmax, output)
    y_ref[...] = x_ref[...] + jnp.dot(obuf[...], wo_ref[...].astype(BF16),
                                      preferred_element_type=F32)


def _attn_sample(q, k, v, cache_k, cache_v, x, wo_bf, bias_pairs):
    b, tn, d = x.shape
    ca = cache_k.shape[1]
    new = pl.BlockSpec((None, tn, d), lambda i: (i, 0, 0))
    old = pl.BlockSpec((None, ca, d), lambda i: (i, 0, 0))
    return pl.pallas_call(
        _attn_sample_kernel,
        grid=(b,),
        in_specs=[new, new, new, old, old, new, _const_spec((d, d)),
                  _const_spec(bias_pairs.shape)],
        out_specs=new,
        out_shape=jax.ShapeDtypeStruct((b, tn, d), F32),
        scratch_shapes=[pltpu.VMEM((tn, d), BF16)],
        compiler_params=_params(1),
        name="attn_sample",
    )(q, k, v, cache_k, cache_v, x, wo_bf, bias_pairs)


def _gmlp_kernel(x_ref, g_ref, win_ref, vg_ref, ws_ref, bs_ref, wout_ref, *rest,
                 blk, emit_v):
    if emit_v:
        y_ref, vout_ref, gated = rest
    else:
        y_ref, gated = rest
    rows, d = x_ref.shape
    n_pairs = d // LANES
    x = x_ref[...]
    h = _rms(x, g_ref[...]).astype(BF16)
    z = jnp.dot(h, win_ref[...].astype(BF16), preferred_element_type=F32)
    z = 0.5 * z * (1.0 + lax.erf(z * np.float32(np.sqrt(0.5))))
    u = z[:, :d]
    v = _rms(z[:, d:], vg_ref[...])
    if emit_v:
        vout_ref[...] = v
    n_blk = rows // blk
    lane = lax.broadcasted_iota(jnp.int32, (blk, n_blk * LANES), 1) & (LANES - 1)
    zero = jnp.zeros((blk, n_blk * LANES), F32)
    for p in range(n_pairs):
        cols = slice(p * LANES, (p + 1) * LANES)
        vcat = jnp.concatenate([v[r * blk:(r + 1) * blk, cols] for r in range(n_blk)], axis=1)
        v_stack = jnp.concatenate(
            [jnp.where(lane < HEAD_DIM, vcat, zero), jnp.where(lane >= HEAD_DIM, vcat, zero)],
            axis=0).astype(BF16)
        m = jnp.dot(ws_ref[p], v_stack, preferred_element_type=F32)
        for r in range(n_blk):
            rs = slice(r * blk, (r + 1) * blk)
            mixed = m[:, r * LANES:(r + 1) * LANES] + bs_ref[:, cols]
            gated[rs, cols] = (u[rs, cols] * mixed).astype(BF16)
    y_ref[...] = x + jnp.dot(gated[...], wout_ref[...].astype(BF16), preferred_element_type=F32)


def _gmlp(x, gain, win_bf, v_gain, ws_pairs, bs_rows, wout_bf, rows, blk, emit_v):
    b, s, d = x.shape
    w = win_bf.shape[1] - d
    assert s % rows == 0 and rows % blk == 0 and w == d
    row = pl.BlockSpec((None, rows, d), lambda i, j: (i, j, 0))
    out_specs = [row]
    out_shape = [jax.ShapeDtypeStruct((b, s, d), F32)]
    if emit_v:
        out_specs.append(row)
        out_shape.append(jax.ShapeDtypeStruct((b, s, w), F32))
    return pl.pallas_call(
        functools.partial(_gmlp_kernel, blk=blk, emit_v=emit_v),
        grid=(b, s // rows),
        in_specs=[row, _const_spec((1, d)), _const_spec(win_bf.shape), _const_spec((1, w)),
                  _const_spec(ws_pairs.shape), _const_spec(bs_rows.shape),
                  _const_spec(wout_bf.shape)],
        out_specs=out_specs,
        out_shape=out_shape,
        scratch_shapes=[pltpu.VMEM((rows, w), BF16)],
        compiler_params=_params(2),
        name="gmlp",
    )(x, gain.reshape(1, d), win_bf, v_gain.reshape(1, w), ws_pairs, bs_rows, wout_bf)


def _conv_taps(stage, n, w, cb):
    c = cb + stage[SUBLANES:SUBLANES + n, :] * w[2:3]
    c = c + stage[SUBLANES - 1:SUBLANES - 1 + n, :] * w[1:2]
    return c + stage[SUBLANES - 2:SUBLANES - 2 + n, :] * w[0:1]


def _ffn_kernel(x_ref, g_ref, wg_ref, wv_ref, cw_ref, cb_ref, wdn_ref, gf_ref,
                y_ref, carry_ref, h_scr, a_scr, act_scr, *, final_norm):
    t = pl.program_id(1)
    rows = x_ref.shape[0]
    ff = wdn_ref.shape[0]
    fc = FF_CHUNK

    @pl.when(t == 0)
    def _():
        carry_ref[...] = jnp.zeros_like(carry_ref)

    x = x_ref[...]
    h_scr[...] = _rms(x, g_ref[...]).astype(BF16)

    for j in range(ff // fc):
        chunk = slice(j * fc, (j + 1) * fc)
        halves = []
        for half, w_ref in enumerate((wg_ref, wv_ref)):
            cols = slice(half * ff + j * fc, half * ff + (j + 1) * fc)
            stage = a_scr.at[2 * (j % 2) + half]
            a = jnp.dot(h_scr[...], w_ref[:, chunk], preferred_element_type=F32)
            stage[0:SUBLANES, :] = carry_ref[:, cols]
            stage[SUBLANES:SUBLANES + rows, :] = a
            carry_ref[:, cols] = a[rows - SUBLANES:rows]
            halves.append(_conv_taps(stage, rows, cw_ref[:, cols], cb_ref[:, cols]))
        act_scr[:, chunk] = (jax.nn.silu(halves[0]) * halves[1]).astype(BF16)

    y = x + jnp.dot(act_scr[...], wdn_ref[...], preferred_element_type=F32)
    if final_norm:
        y = _rms(y, gf_ref[...])
    y_ref[...] = y


def _ffn_stream_kernel(x_ref, g_ref, hg_ref, hv_ref, wg_ref, wv_ref, cwg_ref, cwv_ref, cbg_ref,
                       cbv_ref, wdn_ref, gf_ref, y_ref, cg_ref, cv_ref, wg_out, wv_out, wdn_out,
                       h_scr, acc_scr, a_scr, *, n_sub, final_norm):
    j = pl.program_id(0)
    rows = x_ref.shape[0]
    seq = rows // n_sub

    @pl.when(j == 0)
    def _():
        h_scr[...] = _rms(x_ref[...], g_ref[...]).astype(BF16)
        acc_scr[...] = jnp.zeros_like(acc_scr)

    def branch(slot, w_ref, w_out, hist_ref, carry_out, cw_ref, cb_ref):
        w_bf = w_ref[...].astype(BF16)
        w_out[...] = w_bf
        a = jnp.dot(h_scr[...], w_bf, preferred_element_type=F32)
        outs = []
        for s in range(n_sub):
            stage = a_scr.at[slot, s]
            stage[0:SUBLANES, :] = hist_ref[s]
            stage[SUBLANES:SUBLANES + seq, :] = a[s * seq:(s + 1) * seq]
            carry_out[s] = a[(s + 1) * seq - SUBLANES:(s + 1) * seq]
            outs.append(_conv_taps(stage, seq, cw_ref[...], cb_ref[...]))
        return jnp.concatenate(outs, axis=0)

    gate = branch(0, wg_ref, wg_out, hg_ref, cg_ref, cwg_ref, cbg_ref)
    val = branch(1, wv_ref, wv_out, hv_ref, cv_ref, cwv_ref, cbv_ref)
    wd_bf = wdn_ref[...].astype(BF16)
    wdn_out[...] = wd_bf
    acc_scr[...] += jnp.dot((jax.nn.silu(gate) * val).astype(BF16), wd_bf,
                            preferred_element_type=F32)

    @pl.when(j == pl.num_programs(0) - 1)
    def _():
        y = x_ref[...] + acc_scr[...]
        if final_norm:
            y = _rms(y, gf_ref[...])
        y_ref[...] = y


def _layer_spec(shape, layer):
    nd = len(shape)
    return pl.BlockSpec((None,) + tuple(shape[1:]), lambda *_: (layer,) + (0,) * (nd - 1),
                        pipeline_mode=pl.Buffered(1))


def _conv_ffn(x, gain, wg_bf, wv_bf, conv_w, conv_b, wdn_bf, layer, final_gain, rows):
    b, s, d = x.shape
    ff = wdn_bf.shape[0]
    assert s % rows == 0 and rows % SUBLANES == 0 and ff % FF_CHUNK == 0
    final_norm = final_gain is not None
    gf = (final_gain if final_norm else jnp.ones((d,), F32)).reshape(1, d)
    row = pl.BlockSpec((None, rows, d), lambda i, j: (i, j, 0))
    carry_spec = pl.BlockSpec((None, SUBLANES, 2 * ff), lambda i, j: (i, 0, 0))
    return pl.pallas_call(
        functools.partial(_ffn_kernel, final_norm=final_norm),
        grid=(b, s // rows),
        in_specs=[row, _const_spec((1, d)), _const_spec(wg_bf.shape), _const_spec(wv_bf.shape),
                  _layer_spec(conv_w.shape, layer), _layer_spec(conv_b.shape, layer),
                  _const_spec(wdn_bf.shape), _const_spec((1, d))],
        out_specs=[row, carry_spec],
        out_shape=[jax.ShapeDtypeStruct((b, s, d), F32),
                   jax.ShapeDtypeStruct((b, SUBLANES, 2 * ff), F32)],
        scratch_shapes=[pltpu.VMEM((rows, d), BF16),
                        pltpu.VMEM((4, rows + SUBLANES, FF_CHUNK), F32),
                        pltpu.VMEM((rows, ff), BF16)],
        compiler_params=_params(2),
        name="conv_ffn",
    )(x, gain.reshape(1, d), wg_bf, wv_bf, conv_w, conv_b, wdn_bf, gf)


def _conv_ffn_stream(x, gain, hist, w_up, conv_w, conv_b, w_down, layer, final_gain, n_sub):
    r, d = x.shape
    ff = w_down.shape[1]
    c = ff // FF_CHUNK
    seq = r // n_sub
    assert ff % FF_CHUNK == 0 and seq % SUBLANES == 0
    final_norm = final_gain is not None
    gf = (final_gain if final_norm else jnp.ones((d,), F32)).reshape(1, d)

    def cols(blk_rows, half):
        return pl.BlockSpec((None, blk_rows, FF_CHUNK), lambda j: (layer, 0, half * c + j))

    def hist_cols(half):
        return pl.BlockSpec((n_sub, SUBLANES, FF_CHUNK), lambda j: (0, 0, half * c + j))

    half_cols = pl.BlockSpec((n_sub, SUBLANES, FF_CHUNK), lambda j: (0, 0, j))
    up_cols = pl.BlockSpec((d, FF_CHUNK), lambda j: (0, j))
    dn_rows = pl.BlockSpec((FF_CHUNK, d), lambda j: (j, 0))
    return pl.pallas_call(
        functools.partial(_ffn_stream_kernel, n_sub=n_sub, final_norm=final_norm),
        grid=(c,),
        in_specs=[_const_spec((r, d)), _const_spec((1, d)), hist_cols(0), hist_cols(1),
                  cols(d, 0), cols(d, 1), cols(CONV_W, 0), cols(CONV_W, 1), cols(1, 0), cols(1, 1),
                  pl.BlockSpec((None, FF_CHUNK, d), lambda j: (layer, j, 0)),
                  _const_spec((1, d))],
        out_specs=[pl.BlockSpec((r, d), lambda j: (0, 0)), half_cols, half_cols,
                   up_cols, up_cols, dn_rows],
        out_shape=[jax.ShapeDtypeStruct((r, d), F32),
                   jax.ShapeDtypeStruct((n_sub, SUBLANES, ff), F32),
                   jax.ShapeDtypeStruct((n_sub, SUBLANES, ff), F32),
                   jax.ShapeDtypeStruct((d, ff), BF16),
                   jax.ShapeDtypeStruct((d, ff), BF16),
                   jax.ShapeDtypeStruct((ff, d), BF16)],
        scratch_shapes=[pltpu.VMEM((r, d), BF16), pltpu.VMEM((r, d), F32),
                        pltpu.VMEM((2, n_sub, seq + SUBLANES, FF_CHUNK), F32)],
        compiler_params=_params(1),
        name="conv_ffn_stream",
    )(x, gain.reshape(1, d), hist, hist, w_up, w_up, conv_w, conv_w, conv_b, conv_b, w_down, gf)


def _pair_bias(table, n_q, n_k, reach, band=None):
    diag = jnp.arange(n_q + n_k - 1) - (n_q - 1)
    idx = jnp.clip(reach - diag, -MAX_REL, MAX_REL) + MAX_REL
    r = table[:, idx].astype(F32) * LOG2_E
    h, m = r.shape
    flat = jnp.tile(jnp.pad(r, ((0, 0), (0, 1))), (1, n_q))[:, :n_q * m]
    bias = flat.reshape(h, n_q, m)[:, :, n_q - 1:n_q - 1 + n_k]
    if band is not None:
        first = (jnp.arange(n_q) // CHUNK * CHUNK)[:, None]
        key = jnp.arange(n_k)[None, :]
        bias = jnp.where((key >= first) & (key < first + band), bias, NEG)
    h = bias.shape[0]
    return bias.reshape(h // HEADS_PER_VREG, HEADS_PER_VREG * n_q, n_k)


def _pad_hist(hist):
    pad = [(0, 0)] * (hist.ndim - 2) + [(SUBLANES - hist.shape[-2], 0), (0, 0)]
    return jnp.pad(hist, pad)


def kernel(x_prompt, x_sample, cache_a_k, cache_a_v, state_ffn_conv, ln_mix, ln_ffn, ln_final,
           a_w_qkv, a_rel_bias, a_w_o, b_w_in, b_v_norm, b_w_s, b_bias_s, b_w_out,
           f_w_up, f_conv_w, f_conv_b, f_w_down):
    bp, sp, d = x_prompt.shape
    bs, ts, _ = x_sample.shape
    depth = ln_mix.shape[0]
    ca = cache_a_k.shape[2]
    keep = min(REACH, sp)
    two_ff = f_w_up.shape[2]
    hist_rows = CONV_W - 1
    conv_b_all = f_conv_b.reshape(depth, 1, two_ff)

    xp = x_prompt
    xs = x_sample.reshape(1, bs * ts, d)
    kp_l, vp_l, ks_l, vs_l, gv_l, cp_l, cs_l = [], [], [], [], [], [], []

    for i in range(depth):
        j = i // 2
        if i % 2 == 0:
            wqkv = a_w_qkv[j]
            wo = a_w_o[j]
            q, k, v, kt, vt = _qkv_proj(xp, ln_mix[i], wqkv, keep, ROW_TILE)
            xp = _attn_prompt(q, k, v, xp, wo,
                              _pair_bias(a_rel_bias[j], GROUP_ROWS, GROUP_KEYS, REACH, BAND))
            kp_l.append(kt.reshape(bp, keep, N_HEADS, HEAD_DIM))
            vp_l.append(vt.reshape(bp, keep, N_HEADS, HEAD_DIM))
            q, k, v, kt, vt = _qkv_proj(xs, ln_mix[i], wqkv, bs * ts, bs * ts)
            xs = _attn_sample(
                q.reshape(bs, ts, d), k.reshape(bs, ts, d), v.reshape(bs, ts, d),
                cache_a_k[j].reshape(bs, ca, d), cache_a_v[j].reshape(bs, ca, d),
                xs.reshape(bs, ts, d), wo, _pair_bias(a_rel_bias[j], ts, ca + ts, ca),
            ).reshape(1, bs * ts, d)
            ks_l.append(kt.reshape(bs, ts, N_HEADS, HEAD_DIM))
            vs_l.append(vt.reshape(bs, ts, N_HEADS, HEAD_DIM))
        else:
            win = b_w_in[j]
            wout = b_w_out[j]
            pos = jnp.arange(MLP_BLOCK)
            causal = (pos[None, :] // CHUNK) <= (pos[:, None] // CHUNK)
            ws = jnp.where(causal, b_w_s[j], 0.0).astype(BF16)
            g = ws.shape[0]

            def pair_rows(w):
                w = w.reshape(g // HEADS_PER_VREG, HEADS_PER_VREG, w.shape[1], w.shape[2])
                return jnp.transpose(w, (0, 2, 1, 3)).reshape(
                    g // HEADS_PER_VREG, w.shape[2], HEADS_PER_VREG * w.shape[3])

            def bias_rows(bias):
                return jnp.repeat(bias.T.astype(F32), d // g, axis=1)

            xp = _gmlp(xp, ln_mix[i], win, b_v_norm[j], pair_rows(ws), bias_rows(b_bias_s[j]),
                       wout, ROW_TILE, MLP_BLOCK, False)[0]
            xs, gv = _gmlp(xs, ln_mix[i], win, b_v_norm[j], pair_rows(ws[:, :ts, :ts]),
                           bias_rows(b_bias_s[j][:, :ts]), wout, bs * ts, ts, True)
            gv_l.append(gv.reshape(bs, ts, d))

        fin = ln_final if i == depth - 1 else None
        xs2, cs_g, cs_v, wg, wv, wdn = _conv_ffn_stream(
            xs[0], ln_ffn[i], _pad_hist(state_ffn_conv[i]), f_w_up, f_conv_w, conv_b_all,
            f_w_down, i, fin, bs)
        xs = xs2[None]
        xp, cp = _conv_ffn(xp, ln_ffn[i], wg, wv, f_conv_w, conv_b_all, wdn, i, fin,
                           FFN_ROW_TILE)
        cp_l.append(cp[:, SUBLANES - hist_rows:, :])
        cs_l.append(jnp.concatenate([cs_g, cs_v], axis=-1)[:, SUBLANES - hist_rows:, :])

    return (xp, xs.reshape(bs, ts, d), jnp.stack(kp_l), jnp.stack(vp_l), jnp.stack(ks_l),
            jnp.stack(vs_l), jnp.stack(gv_l), jnp.stack(cp_l), jnp.stack(cs_l))
```

```python
import functools

import jax
import jax.numpy as jnp
import numpy as np
from jax import lax
from jax.experimental import pallas as pl
from jax.experimental.pallas import tpu as pltpu

F32 = jnp.float32
BF16 = jnp.bfloat16

CHUNK = 64
PAST_CHUNKS = 8
REACH = PAST_CHUNKS * CHUNK
BAND = REACH + CHUNK
MAX_REL = 128
N_HEADS = 16
HEAD_DIM = 64
MLP_BLOCK = 128
N_GROUPS = 16
CONV_W = 3
EPS = 1e-6
NEG = -1e30
LOG2_E = float(np.log2(np.e))
Q_SCALE = HEAD_DIM ** -0.5 * LOG2_E

LANES = 128
SUBLANES = 8
MXU_DIM = 256
HEADS_PER_VREG = LANES // HEAD_DIM
VMEM_LIMIT = 56 * 1024 * 1024

ROW_TILE = 512
FFN_ROW_TILE = 512
FF_CHUNK = MXU_DIM
GROUP_CHUNKS = LANES // CHUNK
GROUP_ROWS = GROUP_CHUNKS * CHUNK
GROUP_KEYS = REACH + GROUP_ROWS
VAR_COL0 = (REACH - MAX_REL) // LANES * LANES
ATTN_GROUPS_PER_TRIP = 2


def _rms(x, g):
    return x * lax.rsqrt(jnp.mean(x * x, axis=-1, keepdims=True) + EPS) * g


def _const_spec(shape):
    nd = len(shape)
    return pl.BlockSpec(shape, lambda *_: (0,) * nd, pipeline_mode=pl.Buffered(1))


def _params(n_grid):
    return pltpu.CompilerParams(
        dimension_semantics=("arbitrary",) * n_grid, vmem_limit_bytes=VMEM_LIMIT)


def _qkv_kernel(x_ref, g_ref, w_ref, q_ref, k_ref, v_ref, kt_ref, vt_ref, *, d, tail_tile0):
    t = pl.program_id(1)
    h = _rms(x_ref[...], g_ref[...]).astype(BF16)
    q = jnp.dot(h, w_ref[:, 0:d].astype(BF16), preferred_element_type=F32)
    q_ref[...] = (q * Q_SCALE).astype(BF16)
    k = jnp.dot(h, w_ref[:, d:2 * d].astype(BF16), preferred_element_type=F32)
    k_ref[...] = k.astype(BF16)
    v = jnp.dot(h, w_ref[:, 2 * d:3 * d].astype(BF16), preferred_element_type=F32)
    v_ref[...] = v.astype(BF16)

    @pl.when(t >= tail_tile0)
    def _():
        kt_ref[...] = k
        vt_ref[...] = v


def _qkv_proj(x, gain, w_bf, keep, tm):
    b, s, d = x.shape
    nt = s // tm
    assert s % tm == 0 and keep % tm == 0
    tail_tile0 = nt - keep // tm
    row = pl.BlockSpec((None, tm, d), lambda i, j: (i, j, 0))
    tail = pl.BlockSpec((None, tm, d), lambda i, j: (i, jnp.maximum(j - tail_tile0, 0), 0))
    return pl.pallas_call(
        functools.partial(_qkv_kernel, d=d, tail_tile0=tail_tile0),
        grid=(b, nt),
        in_specs=[row, _const_spec((1, d)), _const_spec((d, 3 * d))],
        out_specs=[row, row, row, tail, tail],
        out_shape=[jax.ShapeDtypeStruct((b, s, d), BF16)] * 3
        + [jax.ShapeDtypeStruct((b, keep, d), F32)] * 2,
        compiler_params=_params(2),
        name="qkv_proj",
    )(x, gain.reshape(1, d), w_bf)


def _pair_scores(qp, kb):
    lane = lax.broadcasted_iota(jnp.int32, qp.shape, 1)
    qf = qp.astype(F32)
    zero = jnp.zeros_like(qf)
    q_bd = jnp.concatenate(
        [jnp.where(lane < HEAD_DIM, qf, zero), jnp.where(lane >= HEAD_DIM, qf, zero)],
        axis=0).astype(BF16)
    return lax.dot_general(q_bd, kb, (((1,), (1,)), ((), ())), preferred_element_type=F32)


def _pair_softmax(s, bias, lim):
    if bias is not None:
        s = s + bias
    if lim is not None:
        col = lax.broadcasted_iota(jnp.int32, s.shape, 1)
        s = jnp.where(col >= lim, s, NEG)
    m = jnp.max(s, axis=-1, keepdims=True)
    e = jnp.exp2(s - m)
    return e.astype(BF16), 1.0 / jnp.sum(e, axis=-1, keepdims=True)


def _pair_output(e, inv_l, vb):
    nq = e.shape[0] // HEADS_PER_VREG
    lane = lax.broadcasted_iota(jnp.int32, (nq, LANES), 1)
    pv = jnp.dot(e, vb, preferred_element_type=F32) * inv_l
    return jnp.where(lane < HEAD_DIM, pv[:nq], pv[nq:])


def _pipelined_pairs(n, scores, softmax, output, lead_scores=1, lead_softmax=1):
    s, e = {}, {}
    for step in range(n + lead_scores + lead_softmax):
        i_sm = step - lead_scores
        i_out = i_sm - lead_softmax
        if step < n:
            s[step] = scores(step)
        if 0 <= i_sm < n:
            e[i_sm] = softmax(i_sm, s.pop(i_sm))
        if i_out >= 0:
            output(i_out, *e.pop(i_out))


def _band_bias_init(u_ref, bias_scr, edge_scr):
    n_var = GROUP_KEYS - VAR_COL0
    ulen = u_ref.shape[1]
    row = lax.broadcasted_iota(jnp.int32, (GROUP_ROWS, n_var), 0)
    col = lax.broadcasted_iota(jnp.int32, (GROUP_ROWS, n_var), 1) + VAR_COL0
    past_band = col >= row // CHUNK * CHUNK + BAND
    for h in range(u_ref.shape[0]):
        per_diag = jnp.broadcast_to(u_ref[h:h + 1, :], (GROUP_ROWS, ulen))
        skew = pltpu.roll(per_diag, ulen - (GROUP_ROWS - 1), 1, stride=1, stride_axis=0)
        p, half = divmod(h, HEADS_PER_VREG)
        bias_scr[p, half * GROUP_ROWS:(half + 1) * GROUP_ROWS, :] = jnp.where(
            past_band, NEG, skew[:, :n_var])
    row = lax.broadcasted_iota(jnp.int32, edge_scr.shape, 0) % GROUP_ROWS
    col = lax.broadcasted_iota(jnp.int32, edge_scr.shape, 1)
    edge_scr[...] = jnp.where(col < row // CHUNK * CHUNK, NEG, 0.0)


def _attn_prompt_kernel(q_ref, k_ref, v_ref, x_ref, wo_ref, u_ref, y_ref,
                        kbuf, vbuf, obuf, bias_scr, edge_scr, *, tq):
    t = pl.program_id(1)
    n_pairs = q_ref.shape[1] // LANES

    @pl.when((pl.program_id(0) == 0) & (t == 0))
    def _():
        _band_bias_init(u_ref, bias_scr, edge_scr)

    @pl.when(t == 0)
    def _():
        kbuf[0:tq, :] = jnp.zeros((tq, kbuf.shape[1]), BF16)
        vbuf[0:tq, :] = jnp.zeros((tq, vbuf.shape[1]), BF16)

    @pl.when(t > 0)
    def _():
        kbuf[0:tq, :] = kbuf[tq:2 * tq, :]
        vbuf[0:tq, :] = vbuf[tq:2 * tq, :]

    kbuf[tq:2 * tq, :] = k_ref[...]
    vbuf[tq:2 * tq, :] = v_ref[...]

    def chunk_loop(masked):
        span = ATTN_GROUPS_PER_TRIP * GROUP_ROWS

        def trip_body(i, carry):
            base = pl.multiple_of(i * span, span)

            def item(n):
                g, p = divmod(n, n_pairs)
                return base + g * GROUP_ROWS, slice(p * LANES, (p + 1) * LANES), p

            def scores(n):
                r0, cols, _ = item(n)
                return _pair_scores(q_ref[pl.ds(r0, GROUP_ROWS), cols],
                                    kbuf[pl.ds(r0, GROUP_KEYS), cols])

            def softmax(n, s):
                r0, _, p = item(n)
                s = jnp.concatenate(
                    [s[:, :LANES] + edge_scr[...], s[:, LANES:VAR_COL0],
                     s[:, VAR_COL0:] + bias_scr[p]], axis=1)
                lim = (REACH - (t * tq + r0)) if masked else None
                return _pair_softmax(s, None, lim)

            def output(n, e, inv_l):
                r0, cols, _ = item(n)
                o = _pair_output(e, inv_l, vbuf[pl.ds(r0, GROUP_KEYS), cols])
                obuf[pl.ds(r0, GROUP_ROWS), cols] = o.astype(BF16)

            _pipelined_pairs(ATTN_GROUPS_PER_TRIP * n_pairs, scores, softmax, output, 1, 2)
            return carry

        lax.fori_loop(0, tq // span, trip_body, 0)

    @pl.when(t == 0)
    def _():
        chunk_loop(True)

    @pl.when(t > 0)
    def _():
        chunk_loop(False)

    y_ref[...] = x_ref[...] + jnp.dot(obuf[...], wo_ref[...].astype(BF16),
                                      preferred_element_type=F32)


def _attn_prompt(q, k, v, x, wo, table):
    b, s, d = x.shape
    tq = REACH
    assert s % tq == 0
    h = table.shape[0]
    n_diag = GROUP_ROWS - 1 + GROUP_KEYS - VAR_COL0
    diag = jnp.arange(n_diag) + (VAR_COL0 - GROUP_ROWS + 1)
    idx = jnp.clip(REACH - diag, -MAX_REL, MAX_REL) + MAX_REL
    u = (table[:, idx] - table[:, 2 * MAX_REL:]).astype(F32) * LOG2_E
    u = jnp.pad(u, ((0, 0), (0, pl.next_power_of_2(n_diag) - n_diag)))
    row = pl.BlockSpec((None, tq, d), lambda i, j: (i, j, 0))
    rows2 = HEADS_PER_VREG * GROUP_ROWS
    return pl.pallas_call(
        functools.partial(_attn_prompt_kernel, tq=tq),
        grid=(b, s // tq),
        in_specs=[row, row, row, row, _const_spec((d, d)), _const_spec(u.shape)],
        out_specs=row,
        out_shape=jax.ShapeDtypeStruct((b, s, d), F32),
        scratch_shapes=[pltpu.VMEM((2 * tq, d), BF16), pltpu.VMEM((2 * tq, d), BF16),
                        pltpu.VMEM((tq, d), BF16),
                        pltpu.VMEM((h // HEADS_PER_VREG, rows2, GROUP_KEYS - VAR_COL0), F32),
                        pltpu.VMEM((rows2, LANES), F32)],
        compiler_params=_params(2),
        name="attn_prompt",
    )(q, k, v, x, wo, u)


def _attn_sample_kernel(q_ref, k_ref, v_ref, ck_ref, cv_ref, x_ref, wo_ref, bias_ref, y_ref,
                        obuf):
    n_pairs = q_ref.shape[1] // LANES
    k_all = jnp.concatenate([ck_ref[...].astype(BF16), k_ref[...]], axis=0)
    v_all = jnp.concatenate([cv_ref[...].astype(BF16), v_ref[...]], axis=0)
    cols = [slice(p * LANES, (p + 1) * LANES) for p in range(n_pairs)]

    def scores(p):
        return _pair_scores(q_ref[:, cols[p]], k_all[:, cols[p]])

    def softmax(p, s):
        return _pair_softmax(s, bias_ref[p], None)

    def output(p, e, inv_l):
        obuf[:, cols[p]] = _pair_output(e, inv_l, v_all[:, cols[p]]).astype(BF16)

    _pipelined_pairs(n_pairs, scores, softmax, output)
    y_ref[...] = x_ref[...] + jnp.dot(obuf[...], wo_ref[...].astype(BF16),
                                      preferred_element_type=F32)


def _attn_sample(q, k, v, cache_k, cache_v, x, wo_bf, bias_pairs):
    b, tn, d = x.shape
    ca = cache_k.shape[1]
    new = pl.BlockSpec((None, tn, d), lambda i: (i, 0, 0))
    old = pl.BlockSpec((None, ca, d), lambda i: (i, 0, 0))
    return pl.pallas_call(
        _attn_sample_kernel,
        grid=(b,),
        in_specs=[new, new, new, old, old, new, _const_spec((d, d)),
                  _const_spec(bias_pairs.shape)],
        out_specs=new,
        out_shape=jax.ShapeDtypeStruct((b, tn, d), F32),
        scratch_shapes=[pltpu.VMEM((tn, d), BF16)],
        compiler_params=_params(1),
        name="attn_sample",
    )(q, k, v, cache_k, cache_v, x, wo_bf, bias_pairs)


def _gmlp_kernel(x_ref, g_ref, win_ref, vg_ref, ws_ref, bs_ref, wout_ref, *rest,
                 blk, emit_v):
    if emit_v:
        y_ref, vout_ref, gated = rest
    else:
        y_ref, gated = rest
    rows, d = x_ref.shape
    n_pairs = d // LANES
    x = x_ref[...]
    h = _rms(x, g_ref[...]).astype(BF16)
    z = jnp.dot(h, win_ref[...].astype(BF16), preferred_element_type=F32)
    z = 0.5 * z * (1.0 + lax.erf(z * np.float32(np.sqrt(0.5))))
    u = z[:, :d]
    v = _rms(z[:, d:], vg_ref[...])
    if emit_v:
        vout_ref[...] = v
    n_blk = rows // blk
    lane = lax.broadcasted_iota(jnp.int32, (blk, n_blk * LANES), 1) & (LANES - 1)
    zero = jnp.zeros((blk, n_blk * LANES), F32)
    for p in range(n_pairs):
        cols = slice(p * LANES, (p + 1) * LANES)
        vcat = jnp.concatenate([v[r * blk:(r + 1) * blk, cols] for r in range(n_blk)], axis=1)
        v_stack = jnp.concatenate(
            [jnp.where(lane < HEAD_DIM, vcat, zero), jnp.where(lane >= HEAD_DIM, vcat, zero)],
            axis=0).astype(BF16)
        m = jnp.dot(ws_ref[p], v_stack, preferred_element_type=F32)
        for r in range(n_blk):
            rs = slice(r * blk, (r + 1) * blk)
            mixed = m[:, r * LANES:(r + 1) * LANES] + bs_ref[:, cols]
            gated[rs, cols] = (u[rs, cols] * mixed).astype(BF16)
    y_ref[...] = x + jnp.dot(gated[...], wout_ref[...].astype(BF16), preferred_element_type=F32)


def _gmlp(x, gain, win_bf, v_gain, ws_pairs, bs_rows, wout_bf, rows, blk, emit_v):
    b, s, d = x.shape
    w = win_bf.shape[1] - d
    assert s % rows == 0 and rows % blk == 0 and w == d
    row = pl.BlockSpec((None, rows, d), lambda i, j: (i, j, 0))
    out_specs = [row]
    out_shape = [jax.ShapeDtypeStruct((b, s, d), F32)]
    if emit_v:
        out_specs.append(row)
        out_shape.append(jax.ShapeDtypeStruct((b, s, w), F32))
    return pl.pallas_call(
        functools.partial(_gmlp_kernel, blk=blk, emit_v=emit_v),
        grid=(b, s // rows),
        in_specs=[row, _const_spec((1, d)), _const_spec(win_bf.shape), _const_spec((1, w)),
                  _const_spec(ws_pairs.shape), _const_spec(bs_rows.shape),
                  _const_spec(wout_bf.shape)],
        out_specs=out_specs,
        out_shape=out_shape,
        scratch_shapes=[pltpu.VMEM((rows, w), BF16)],
        compiler_params=_params(2),
        name="gmlp",
    )(x, gain.reshape(1, d), win_bf, v_gain.reshape(1, w), ws_pairs, bs_rows, wout_bf)


def _conv_taps(stage, n, w, cb):
    c = cb + stage[SUBLANES:SUBLANES + n, :] * w[2:3]
    c = c + stage[SUBLANES - 1:SUBLANES - 1 + n, :] * w[1:2]
    return c + stage[SUBLANES - 2:SUBLANES - 2 + n, :] * w[0:1]


def _ffn_kernel(x_ref, g_ref, wg_ref, wv_ref, cw_ref, cb_ref, wdn_ref, gf_ref,
                y_ref, carry_ref, h_scr, a_scr, act_scr, *, final_norm):
    t = pl.program_id(1)
    rows = x_ref.shape[0]
    ff = wdn_ref.shape[0]
    fc = FF_CHUNK

    @pl.when(t == 0)
    def _():
        carry_ref[...] = jnp.zeros_like(carry_ref)

    x = x_ref[...]
    h_scr[...] = _rms(x, g_ref[...]).astype(BF16)

    for j in range(ff // fc):
        chunk = slice(j * fc, (j + 1) * fc)
        halves = []
        for half, w_ref in enumerate((wg_ref, wv_ref)):
            cols = slice(half * ff + j * fc, half * ff + (j + 1) * fc)
            stage = a_scr.at[2 * (j % 2) + half]
            a = jnp.dot(h_scr[...], w_ref[:, chunk], preferred_element_type=F32)
            stage[0:SUBLANES, :] = carry_ref[:, cols]
            stage[SUBLANES:SUBLANES + rows, :] = a
            carry_ref[:, cols] = a[rows - SUBLANES:rows]
            halves.append(_conv_taps(stage, rows, cw_ref[:, cols], cb_ref[:, cols]))
        act_scr[:, chunk] = (jax.nn.silu(halves[0]) * halves[1]).astype(BF16)

    y = x + jnp.dot(act_scr[...], wdn_ref[...], preferred_element_type=F32)
    if final_norm:
        y = _rms(y, gf_ref[...])
    y_ref[...] = y


def _ffn_stream_kernel(x_ref, g_ref, hg_ref, hv_ref, wg_ref, wv_ref, cwg_ref, cwv_ref, cbg_ref,
                       cbv_ref, wdn_ref, gf_ref, y_ref, cg_ref, cv_ref, wg_out, wv_out, wdn_out,
                       h_scr, acc_scr, a_scr, *, n_sub, final_norm):
    j = pl.program_id(0)
    rows = x_ref.shape[0]
    seq = rows // n_sub

    @pl.when(j == 0)
    def _():
        h_scr[...] = _rms(x_ref[...], g_ref[...]).astype(BF16)
        acc_scr[...] = jnp.zeros_like(acc_scr)

    def branch(slot, w_ref, w_out, hist_ref, carry_out, cw_ref, cb_ref):
        w_bf = w_ref[...].astype(BF16)
        w_out[...] = w_bf
        a = jnp.dot(h_scr[...], w_bf, preferred_element_type=F32)
        outs = []
        for s in range(n_sub):
            stage = a_scr.at[slot, s]
            stage[0:SUBLANES, :] = hist_ref[s]
            stage[SUBLANES:SUBLANES + seq, :] = a[s * seq:(s + 1) * seq]
            carry_out[s] = a[(s + 1) * seq - SUBLANES:(s + 1) * seq]
            outs.append(_conv_taps(stage, seq, cw_ref[...], cb_ref[...]))
        return jnp.concatenate(outs, axis=0)

    gate = branch(0, wg_ref, wg_out, hg_ref, cg_ref, cwg_ref, cbg_ref)
    val = branch(1, wv_ref, wv_out, hv_ref, cv_ref, cwv_ref, cbv_ref)
    wd_bf = wdn_ref[...].astype(BF16)
    wdn_out[...] = wd_bf
    acc_scr[...] += jnp.dot((jax.nn.silu(gate) * val).astype(BF16), wd_bf,
                            preferred_element_type=F32)

    @pl.when(j == pl.num_programs(0) - 1)
    def _():
        y = x_ref[...] + acc_scr[...]
        if final_norm:
            y = _rms(y, gf_ref[...])
        y_ref[...] = y


def _layer_spec(shape, layer):
    nd = len(shape)
    return pl.BlockSpec((None,) + tuple(shape[1:]), lambda *_: (layer,) + (0,) * (nd - 1),
                        pipeline_mode=pl.Buffered(1))


def _conv_ffn(x, gain, wg_bf, wv_bf, conv_w, conv_b, wdn_bf, layer, final_gain, rows):
    b, s, d = x.shape
    ff = wdn_bf.shape[0]
    assert s % rows == 0 and rows % SUBLANES == 0 and ff % FF_CHUNK == 0
    final_norm = final_gain is not None
    gf = (final_gain if final_norm else jnp.ones((d,), F32)).reshape(1, d)
    row = pl.BlockSpec((None, rows, d), lambda i, j: (i, j, 0))
    carry_spec = pl.BlockSpec((None, SUBLANES, 2 * ff), lambda i, j: (i, 0, 0))
    return pl.pallas_call(
        functools.partial(_ffn_kernel, final_norm=final_norm),
        grid=(b, s // rows),
        in_specs=[row, _const_spec((1, d)), _const_spec(wg_bf.shape), _const_spec(wv_bf.shape),
                  _layer_spec(conv_w.shape, layer), _layer_spec(conv_b.shape, layer),
                  _const_spec(wdn_bf.shape), _const_spec((1, d))],
        out_specs=[row, carry_spec],
        out_shape=[jax.ShapeDtypeStruct((b, s, d), F32),
                   jax.ShapeDtypeStruct((b, SUBLANES, 2 * ff), F32)],
        scratch_shapes=[pltpu.VMEM((rows, d), BF16),
                        pltpu.VMEM((4, rows + SUBLANES, FF_CHUNK), F32),
                        pltpu.VMEM((rows, ff), BF16)],
        compiler_params=_params(2),
        name="conv_ffn",
    )(x, gain.reshape(1, d), wg_bf, wv_bf, conv_w, conv_b, wdn_bf, gf)


def _conv_ffn_stream(x, gain, hist, w_up, conv_w, conv_b, w_down, layer, final_gain, n_sub):
    r, d = x.shape
    ff = w_down.shape[1]
    c = ff // FF_CHUNK
    seq = r // n_sub
    assert ff % FF_CHUNK == 0 and seq % SUBLANES == 0
    final_norm = final_gain is not None
    gf = (final_gain if final_norm else jnp.ones((d,), F32)).reshape(1, d)

    def cols(blk_rows, half):
        return pl.BlockSpec((None, blk_rows, FF_CHUNK), lambda j: (layer, 0, half * c + j))

    def hist_cols(half):
        return pl.BlockSpec((n_sub, SUBLANES, FF_CHUNK), lambda j: (0, 0, half * c + j))

    half_cols = pl.BlockSpec((n_sub, SUBLANES, FF_CHUNK), lambda j: (0, 0, j))
    up_cols = pl.BlockSpec((d, FF_CHUNK), lambda j: (0, j))
    dn_rows = pl.BlockSpec((FF_CHUNK, d), lambda j: (j, 0))
    return pl.pallas_call(
        functools.partial(_ffn_stream_kernel, n_sub=n_sub, final_norm=final_norm),
        grid=(c,),
        in_specs=[_const_spec((r, d)), _const_spec((1, d)), hist_cols(0), hist_cols(1),
                  cols(d, 0), cols(d, 1), cols(CONV_W, 0), cols(CONV_W, 1), cols(1, 0), cols(1, 1),
                  pl.BlockSpec((None, FF_CHUNK, d), lambda j: (layer, j, 0)),
                  _const_spec((1, d))],
        out_specs=[pl.BlockSpec((r, d), lambda j: (0, 0)), half_cols, half_cols,
                   up_cols, up_cols, dn_rows],
        out_shape=[jax.ShapeDtypeStruct((r, d), F32),
                   jax.ShapeDtypeStruct((n_sub, SUBLANES, ff), F32),
                   jax.ShapeDtypeStruct((n_sub, SUBLANES, ff), F32),
                   jax.ShapeDtypeStruct((d, ff), BF16),
                   jax.ShapeDtypeStruct((d, ff), BF16),
                   jax.ShapeDtypeStruct((ff, d), BF16)],
        scratch_shapes=[pltpu.VMEM((r, d), BF16), pltpu.VMEM((r, d), F32),
                        pltpu.VMEM((2, n_sub, seq + SUBLANES, FF_CHUNK), F32)],
        compiler_params=_params(1),
        name="conv_ffn_stream",
    )(x, gain.reshape(1, d), hist, hist, w_up, w_up, conv_w, conv_w, conv_b, conv_b, w_down, gf)


def _pair_bias(table, n_q, n_k, reach):
    diag = jnp.arange(n_q + n_k - 1) - (n_q - 1)
    idx = jnp.clip(reach - diag, -MAX_REL, MAX_REL) + MAX_REL
    r = table[:, idx].astype(F32) * LOG2_E
    h, m = r.shape
    flat = jnp.tile(jnp.pad(r, ((0, 0), (0, 1))), (1, n_q))[:, :n_q * m]
    bias = flat.reshape(h, n_q, m)[:, :, n_q - 1:n_q - 1 + n_k]
    return bias.reshape(h // HEADS_PER_VREG, HEADS_PER_VREG * n_q, n_k)


def _pad_hist(hist):
    pad = [(0, 0)] * (hist.ndim - 2) + [(SUBLANES - hist.shape[-2], 0), (0, 0)]
    return jnp.pad(hist, pad)


def kernel(x_prompt, x_sample, cache_a_k, cache_a_v, state_ffn_conv, ln_mix, ln_ffn, ln_final,
           a_w_qkv, a_rel_bias, a_w_o, b_w_in, b_v_norm, b_w_s, b_bias_s, b_w_out,
           f_w_up, f_conv_w, f_conv_b, f_w_down):
    bp, sp, d = x_prompt.shape
    bs, ts, _ = x_sample.shape
    depth = ln_mix.shape[0]
    ca = cache_a_k.shape[2]
    keep = min(REACH, sp)
    two_ff = f_w_up.shape[2]
    hist_rows = CONV_W - 1
    conv_b_all = f_conv_b.reshape(depth, 1, two_ff)

    xp = x_prompt
    xs = x_sample.reshape(1, bs * ts, d)
    kp_l, vp_l, ks_l, vs_l, gv_l, cp_l, cs_l = [], [], [], [], [], [], []

    for i in range(depth):
        j = i // 2
        if i % 2 == 0:
            wqkv = a_w_qkv[j]
            wo = a_w_o[j]
            q, k, v, kt, vt = _qkv_proj(xp, ln_mix[i], wqkv, keep, ROW_TILE)
            xp = _attn_prompt(q, k, v, xp, wo, a_rel_bias[j])
            kp_l.append(kt.reshape(bp, keep, N_HEADS, HEAD_DIM))
            vp_l.append(vt.reshape(bp, keep, N_HEADS, HEAD_DIM))
            q, k, v, kt, vt = _qkv_proj(xs, ln_mix[i], wqkv, bs * ts, bs * ts)
            xs = _attn_sample(
                q.reshape(bs, ts, d), k.reshape(bs, ts, d), v.reshape(bs, ts, d),
                cache_a_k[j].reshape(bs, ca, d), cache_a_v[j].reshape(bs, ca, d),
                xs.reshape(bs, ts, d), wo, _pair_bias(a_rel_bias[j], ts, ca + ts, ca),
            ).reshape(1, bs * ts, d)
            ks_l.append(kt.reshape(bs, ts, N_HEADS, HEAD_DIM))
            vs_l.append(vt.reshape(bs, ts, N_HEADS, HEAD_DIM))
        else:
            win = b_w_in[j]
            wout = b_w_out[j]
            pos = jnp.arange(MLP_BLOCK)
            causal = (pos[None, :] // CHUNK) <= (pos[:, None] // CHUNK)
            ws = jnp.where(causal, b_w_s[j], 0.0).astype(BF16)
            g = ws.shape[0]

            def pair_rows(w):
                w = w.reshape(g // HEADS_PER_VREG, HEADS_PER_VREG, w.shape[1], w.shape[2])
                return jnp.transpose(w, (0, 2, 1, 3)).reshape(
                    g // HEADS_PER_VREG, w.shape[2], HEADS_PER_VREG * w.shape[3])

            def bias_rows(bias):
                return jnp.repeat(bias.T.astype(F32), d // g, axis=1)

            xp = _gmlp(xp, ln_mix[i], win, b_v_norm[j], pair_rows(ws), bias_rows(b_bias_s[j]),
                       wout, ROW_TILE, MLP_BLOCK, False)[0]
            xs, gv = _gmlp(xs, ln_mix[i], win, b_v_norm[j], pair_rows(ws[:, :ts, :ts]),
                           bias_rows(b_bias_s[j][:, :ts]), wout, bs * ts, ts, True)
            gv_l.append(gv.reshape(bs, ts, d))

        fin = ln_final if i == depth - 1 else None
        xs2, cs_g, cs_v, wg, wv, wdn = _conv_ffn_stream(
            xs[0], ln_ffn[i], _pad_hist(state_ffn_conv[i]), f_w_up, f_conv_w, conv_b_all,
            f_w_down, i, fin, bs)
        xs = xs2[None]
        xp, cp = _conv_ffn(xp, ln_ffn[i], wg, wv, f_conv_w, conv_b_all, wdn, i, fin,
                           FFN_ROW_TILE)
        cp_l.append(cp[:, SUBLANES - hist_rows:, :])
        cs_l.append(jnp.concatenate([cs_g, cs_v], axis=-1)[:, SUBLANES - hist_rows:, :])

    return (xp, xs.reshape(bs, ts, d), jnp.stack(kp_l), jnp.stack(vp_l), jnp.stack(ks_l),
            jnp.stack(vs_l), jnp.stack(gv_l), jnp.stack(cp_l), jnp.stack(cs_l))
```

```python
import functools

import jax
import jax.numpy as jnp
import numpy as np
from jax import lax
from jax.experimental import pallas as pl
from jax.experimental.pallas import tpu as pltpu

F32 = jnp.float32
BF16 = jnp.bfloat16

CHUNK = 64
PAST_CHUNKS = 8
REACH = PAST_CHUNKS * CHUNK
BAND = REACH + CHUNK
MAX_REL = 128
N_HEADS = 16
HEAD_DIM = 64
MLP_BLOCK = 128
N_GROUPS = 16
CONV_W = 3
EPS = 1e-6
NEG = -1e30
LOG2_E = float(np.log2(np.e))
Q_SCALE = HEAD_DIM ** -0.5 * LOG2_E

LANES = 128
SUBLANES = 8
MXU_DIM = 256
HEADS_PER_VREG = LANES // HEAD_DIM
VMEM_LIMIT = 56 * 1024 * 1024

ROW_TILE = 512
FFN_ROW_TILE = 512
FF_CHUNK = MXU_DIM
GROUP_CHUNKS = LANES // CHUNK
GROUP_ROWS = GROUP_CHUNKS * CHUNK
GROUP_KEYS = REACH + GROUP_ROWS
VAR_COL0 = (REACH - MAX_REL) // LANES * LANES
ATTN_GROUPS_PER_TRIP = 4
PROJ_GROUPS = 2


def _rms(x, g):
    return x * lax.rsqrt(jnp.mean(x * x, axis=-1, keepdims=True) + EPS) * g


def _const_spec(shape):
    nd = len(shape)
    return pl.BlockSpec(shape, lambda *_: (0,) * nd, pipeline_mode=pl.Buffered(1))


def _params(n_grid):
    return pltpu.CompilerParams(
        dimension_semantics=("arbitrary",) * n_grid, vmem_limit_bytes=VMEM_LIMIT)


def _qkv_kernel(x_ref, g_ref, w_ref, q_ref, k_ref, v_ref, kt_ref, vt_ref, *, d, tail_tile0):
    t = pl.program_id(1)
    h = _rms(x_ref[...], g_ref[...]).astype(BF16)
    q = jnp.dot(h, w_ref[:, 0:d].astype(BF16), preferred_element_type=F32)
    q_ref[...] = (q * Q_SCALE).astype(BF16)
    k = jnp.dot(h, w_ref[:, d:2 * d].astype(BF16), preferred_element_type=F32)
    k_ref[...] = k.astype(BF16)
    v = jnp.dot(h, w_ref[:, 2 * d:3 * d].astype(BF16), preferred_element_type=F32)
    v_ref[...] = v.astype(BF16)

    @pl.when(t >= tail_tile0)
    def _():
        kt_ref[...] = k
        vt_ref[...] = v


def _qkv_proj(x, gain, w_bf, keep, tm):
    b, s, d = x.shape
    nt = s // tm
    assert s % tm == 0 and keep % tm == 0
    tail_tile0 = nt - keep // tm
    row = pl.BlockSpec((None, tm, d), lambda i, j: (i, j, 0))
    tail = pl.BlockSpec((None, tm, d), lambda i, j: (i, jnp.maximum(j - tail_tile0, 0), 0))
    return pl.pallas_call(
        functools.partial(_qkv_kernel, d=d, tail_tile0=tail_tile0),
        grid=(b, nt),
        in_specs=[row, _const_spec((1, d)), _const_spec((d, 3 * d))],
        out_specs=[row, row, row, tail, tail],
        out_shape=[jax.ShapeDtypeStruct((b, s, d), BF16)] * 3
        + [jax.ShapeDtypeStruct((b, keep, d), F32)] * 2,
        compiler_params=_params(2),
        name="qkv_proj",
    )(x, gain.reshape(1, d), w_bf)


def _pair_scores(qp, kb):
    lane = lax.broadcasted_iota(jnp.int32, qp.shape, 1)
    qf = qp.astype(F32)
    zero = jnp.zeros_like(qf)
    q_bd = jnp.concatenate(
        [jnp.where(lane < HEAD_DIM, qf, zero), jnp.where(lane >= HEAD_DIM, qf, zero)],
        axis=0).astype(BF16)
    return lax.dot_general(q_bd, kb, (((1,), (1,)), ((), ())), preferred_element_type=F32)


def _pair_softmax(s, bias, lim):
    if bias is not None:
        s = s + bias
    if lim is not None:
        col = lax.broadcasted_iota(jnp.int32, s.shape, 1)
        s = jnp.where(col >= lim, s, NEG)
    m = jnp.max(s, axis=-1, keepdims=True)
    e = jnp.exp2(s - m)
    return e.astype(BF16), 1.0 / jnp.sum(e, axis=-1, keepdims=True)


def _pair_output(e, inv_l, vb):
    nq = e.shape[0] // HEADS_PER_VREG
    lane = lax.broadcasted_iota(jnp.int32, (nq, LANES), 1)
    pv = jnp.dot(e, vb, preferred_element_type=F32) * inv_l
    return jnp.where(lane < HEAD_DIM, pv[:nq], pv[nq:])


def _pipelined_pairs(n, scores, softmax, output, lead_scores=1, lead_softmax=1, done=None):
    s, e = {}, {}
    for step in range(n + lead_scores + lead_softmax):
        i_sm = step - lead_scores
        i_out = i_sm - lead_softmax
        if step < n:
            s[step] = scores(step)
        if 0 <= i_sm < n:
            e[i_sm] = softmax(i_sm, s.pop(i_sm))
        if i_out >= 0:
            output(i_out, *e.pop(i_out))
            if done is not None:
                done(i_out)


def _band_bias_init(u_ref, bias_scr, edge_scr):
    n_var = GROUP_KEYS - VAR_COL0
    ulen = u_ref.shape[1]
    row = lax.broadcasted_iota(jnp.int32, (GROUP_ROWS, n_var), 0)
    col = lax.broadcasted_iota(jnp.int32, (GROUP_ROWS, n_var), 1) + VAR_COL0
    past_band = col >= row // CHUNK * CHUNK + BAND
    for h in range(u_ref.shape[0]):
        per_diag = jnp.broadcast_to(u_ref[h:h + 1, :], (GROUP_ROWS, ulen))
        skew = pltpu.roll(per_diag, ulen - (GROUP_ROWS - 1), 1, stride=1, stride_axis=0)
        p, half = divmod(h, HEADS_PER_VREG)
        bias_scr[p, half * GROUP_ROWS:(half + 1) * GROUP_ROWS, :] = jnp.where(
            past_band, NEG, skew[:, :n_var])
    row = lax.broadcasted_iota(jnp.int32, edge_scr.shape, 0) % GROUP_ROWS
    col = lax.broadcasted_iota(jnp.int32, edge_scr.shape, 1)
    edge_scr[...] = jnp.where(col < row // CHUNK * CHUNK, NEG, 0.0)


def _attn_prompt_kernel(q_ref, k_ref, v_ref, x_ref, wo_ref, u_ref, y_ref,
                        kbuf, vbuf, obuf, bias_scr, edge_scr, *, tq):
    t = pl.program_id(1)
    n_pairs = q_ref.shape[1] // LANES

    @pl.when((pl.program_id(0) == 0) & (t == 0))
    def _():
        _band_bias_init(u_ref, bias_scr, edge_scr)

    @pl.when(t == 0)
    def _():
        kbuf[0:tq, :] = jnp.zeros((tq, kbuf.shape[1]), BF16)
        vbuf[0:tq, :] = jnp.zeros((tq, vbuf.shape[1]), BF16)

    @pl.when(t > 0)
    def _():
        kbuf[0:tq, :] = kbuf[tq:2 * tq, :]
        vbuf[0:tq, :] = vbuf[tq:2 * tq, :]

    kbuf[tq:2 * tq, :] = k_ref[...]
    vbuf[tq:2 * tq, :] = v_ref[...]

    def chunk_loop(masked):
        span = ATTN_GROUPS_PER_TRIP * GROUP_ROWS

        def trip_body(i, carry):
            base = pl.multiple_of(i * span, span)

            def item(n):
                g, p = divmod(n, n_pairs)
                return base + g * GROUP_ROWS, slice(p * LANES, (p + 1) * LANES), p

            def scores(n):
                r0, cols, _ = item(n)
                return _pair_scores(q_ref[pl.ds(r0, GROUP_ROWS), cols],
                                    kbuf[pl.ds(r0, GROUP_KEYS), cols])

            def softmax(n, s):
                r0, _, p = item(n)
                s = jnp.concatenate(
                    [s[:, :LANES] + edge_scr[...], s[:, LANES:VAR_COL0],
                     s[:, VAR_COL0:] + bias_scr[p]], axis=1)
                lim = (REACH - (t * tq + r0)) if masked else None
                return _pair_softmax(s, None, lim)

            def output(n, e, inv_l):
                r0, cols, _ = item(n)
                o = _pair_output(e, inv_l, vbuf[pl.ds(r0, GROUP_KEYS), cols])
                obuf[pl.ds(r0, GROUP_ROWS), cols] = o.astype(BF16)

            def done(n):
                if (n + 1) % (PROJ_GROUPS * n_pairs) == 0:
                    rows = PROJ_GROUPS * GROUP_ROWS
                    r0 = pl.multiple_of(base + (n + 1) // n_pairs * GROUP_ROWS - rows, rows)
                    y_ref[pl.ds(r0, rows), :] = x_ref[pl.ds(r0, rows), :] + jnp.dot(
                        obuf[pl.ds(r0, rows), :], wo_ref[...].astype(BF16),
                        preferred_element_type=F32)

            _pipelined_pairs(ATTN_GROUPS_PER_TRIP * n_pairs, scores, softmax, output, 1, 2, done)
            return carry

        lax.fori_loop(0, tq // span, trip_body, 0)

    @pl.when(t == 0)
    def _():
        chunk_loop(True)

    @pl.when(t > 0)
    def _():
        chunk_loop(False)


def _attn_prompt(q, k, v, x, wo, table):
    b, s, d = x.shape
    tq = REACH
    assert s % tq == 0
    h = table.shape[0]
    n_diag = GROUP_ROWS - 1 + GROUP_KEYS - VAR_COL0
    diag = jnp.arange(n_diag) + (VAR_COL0 - GROUP_ROWS + 1)
    idx = jnp.clip(REACH - diag, -MAX_REL, MAX_REL) + MAX_REL
    u = (table[:, idx] - table[:, 2 * MAX_REL:]).astype(F32) * LOG2_E
    u = jnp.pad(u, ((0, 0), (0, pl.next_power_of_2(n_diag) - n_diag)))
    row = pl.BlockSpec((None, tq, d), lambda i, j: (i, j, 0))
    rows2 = HEADS_PER_VREG * GROUP_ROWS
    return pl.pallas_call(
        functools.partial(_attn_prompt_kernel, tq=tq),
        grid=(b, s // tq),
        in_specs=[row, row, row, row, _const_spec((d, d)), _const_spec(u.shape)],
        out_specs=row,
        out_shape=jax.ShapeDtypeStruct((b, s, d), F32),
        scratch_shapes=[pltpu.VMEM((2 * tq, d), BF16), pltpu.VMEM((2 * tq, d), BF16),
                        pltpu.VMEM((tq, d), BF16),
                        pltpu.VMEM((h // HEADS_PER_VREG, rows2, GROUP_KEYS - VAR_COL0), F32),
                        pltpu.VMEM((rows2, LANES), F32)],
        compiler_params=_params(2),
        name="attn_prompt",
    )(q, k, v, x, wo, u)


def _attn_sample_kernel(q_ref, k_ref, v_ref, ck_ref, cv_ref, x_ref, wo_ref, bias_ref, y_ref,
                        obuf):
    n_pairs = q_ref.shape[1] // LANES
    k_all = jnp.concatenate([ck_ref[...].astype(BF16), k_ref[...]], axis=0)
    v_all = jnp.concatenate([cv_ref[...].astype(BF16), v_ref[...]], axis=0)
    cols = [slice(p * LANES, (p + 1) * LANES) for p in range(n_pairs)]

    def scores(p):
        return _pair_scores(q_ref[:, cols[p]], k_all[:, cols[p]])

    def softmax(p, s):
        return _pair_softmax(s, bias_ref[p], None)

    def output(p, e, inv_l):
        obuf[:, cols[p]] = _pair_output(e, inv_l, v_all[:, cols[p]]).astype(BF16)

    _pipelined_pairs(n_pairs, scores, softmax, output)
    y_ref[...] = x_ref[...] + jnp.dot(obuf[...], wo_ref[...].astype(BF16),
                                      preferred_element_type=F32)


def _attn_sample(q, k, v, cache_k, cache_v, x, wo_bf, bias_pairs):
    b, tn, d = x.shape
    ca = cache_k.shape[1]
    new = pl.BlockSpec((None, tn, d), lambda i: (i, 0, 0))
    old = pl.BlockSpec((None, ca, d), lambda i: (i, 0, 0))
    return pl.pallas_call(
        _attn_sample_kernel,
        grid=(b,),
        in_specs=[new, new, new, old, old, new, _const_spec((d, d)),
                  _const_spec(bias_pairs.shape)],
        out_specs=new,
        out_shape=jax.ShapeDtypeStruct((b, tn, d), F32),
        scratch_shapes=[pltpu.VMEM((tn, d), BF16)],
        compiler_params=_params(1),
        name="attn_sample",
    )(q, k, v, cache_k, cache_v, x, wo_bf, bias_pairs)


def _gmlp_kernel(x_ref, g_ref, win_ref, vg_ref, ws_ref, bs_ref, wout_ref, *rest,
                 blk, emit_v):
    if emit_v:
        y_ref, vout_ref, gated = rest
    else:
        y_ref, gated = rest
    rows, d = x_ref.shape
    n_pairs = d // LANES
    x = x_ref[...]
    h = _rms(x, g_ref[...]).astype(BF16)
    def gelu(z):
        return 0.5 * z * (1.0 + lax.erf(z * np.float32(np.sqrt(0.5))))

    v = _rms(gelu(jnp.dot(h, win_ref[:, d:].astype(BF16), preferred_element_type=F32)),
             vg_ref[...])
    u = gelu(jnp.dot(h, win_ref[:, :d].astype(BF16), preferred_element_type=F32))
    if emit_v:
        vout_ref[...] = v
    n_blk = rows // blk
    lane = lax.broadcasted_iota(jnp.int32, (blk, n_blk * LANES), 1) & (LANES - 1)
    zero = jnp.zeros((blk, n_blk * LANES), F32)
    for p in range(n_pairs):
        cols = slice(p * LANES, (p + 1) * LANES)
        vcat = jnp.concatenate([v[r * blk:(r + 1) * blk, cols] for r in range(n_blk)], axis=1)
        v_stack = jnp.concatenate(
            [jnp.where(lane < HEAD_DIM, vcat, zero), jnp.where(lane >= HEAD_DIM, vcat, zero)],
            axis=0).astype(BF16)
        m = jnp.dot(ws_ref[p], v_stack, preferred_element_type=F32)
        for r in range(n_blk):
            rs = slice(r * blk, (r + 1) * blk)
            mixed = m[:, r * LANES:(r + 1) * LANES] + bs_ref[:, cols]
            gated[rs, cols] = (u[rs, cols] * mixed).astype(BF16)
    y_ref[...] = x + jnp.dot(gated[...], wout_ref[...].astype(BF16), preferred_element_type=F32)


def _gmlp(x, gain, win_bf, v_gain, ws_pairs, bs_rows, wout_bf, rows, blk, emit_v):
    b, s, d = x.shape
    w = win_bf.shape[1] - d
    assert s % rows == 0 and rows % blk == 0 and w == d
    row = pl.BlockSpec((None, rows, d), lambda i, j: (i, j, 0))
    out_specs = [row]
    out_shape = [jax.ShapeDtypeStruct((b, s, d), F32)]
    if emit_v:
        out_specs.append(row)
        out_shape.append(jax.ShapeDtypeStruct((b, s, w), F32))
    return pl.pallas_call(
        functools.partial(_gmlp_kernel, blk=blk, emit_v=emit_v),
        grid=(b, s // rows),
        in_specs=[row, _const_spec((1, d)), _const_spec(win_bf.shape), _const_spec((1, w)),
                  _const_spec(ws_pairs.shape), _const_spec(bs_rows.shape),
                  _const_spec(wout_bf.shape)],
        out_specs=out_specs,
        out_shape=out_shape,
        scratch_shapes=[pltpu.VMEM((rows, w), BF16)],
        compiler_params=_params(2),
        name="gmlp",
    )(x, gain.reshape(1, d), win_bf, v_gain.reshape(1, w), ws_pairs, bs_rows, wout_bf)


def _conv_taps(stage, n, w, cb):
    c = cb + stage[SUBLANES:SUBLANES + n, :] * w[2:3]
    c = c + stage[SUBLANES - 1:SUBLANES - 1 + n, :] * w[1:2]
    return c + stage[SUBLANES - 2:SUBLANES - 2 + n, :] * w[0:1]


def _ffn_kernel(x_ref, g_ref, wg_ref, wv_ref, cw_ref, cb_ref, wdn_ref, gf_ref,
                y_ref, carry_ref, h_scr, a_scr, act_scr, *, final_norm):
    t = pl.program_id(1)
    rows = x_ref.shape[0]
    ff = wdn_ref.shape[0]
    fc = FF_CHUNK

    @pl.when(t == 0)
    def _():
        carry_ref[...] = jnp.zeros_like(carry_ref)

    x = x_ref[...]
    h_scr[...] = _rms(x, g_ref[...]).astype(BF16)

    for j in range(ff // fc):
        chunk = slice(j * fc, (j + 1) * fc)
        halves = []
        for half, w_ref in enumerate((wg_ref, wv_ref)):
            cols = slice(half * ff + j * fc, half * ff + (j + 1) * fc)
            stage = a_scr.at[2 * (j % 2) + half]
            a = jnp.dot(h_scr[...], w_ref[:, chunk], preferred_element_type=F32)
            stage[0:SUBLANES, :] = carry_ref[:, cols]
            stage[SUBLANES:SUBLANES + rows, :] = a
            carry_ref[:, cols] = a[rows - SUBLANES:rows]
            halves.append(_conv_taps(stage, rows, cw_ref[:, cols], cb_ref[:, cols]))
        act_scr[:, chunk] = (jax.nn.silu(halves[0]) * halves[1]).astype(BF16)

    y = x + jnp.dot(act_scr[...], wdn_ref[...], preferred_element_type=F32)
    if final_norm:
        y = _rms(y, gf_ref[...])
    y_ref[...] = y


def _ffn_stream_kernel(x_ref, g_ref, hg_ref, hv_ref, wg_ref, wv_ref, cwg_ref, cwv_ref, cbg_ref,
                       cbv_ref, wdn_ref, gf_ref, y_ref, cg_ref, cv_ref, wg_out, wv_out, wdn_out,
                       h_scr, acc_scr, a_scr, *, n_sub, final_norm):
    j = pl.program_id(0)
    rows = x_ref.shape[0]
    seq = rows // n_sub

    @pl.when(j == 0)
    def _():
        h_scr[...] = _rms(x_ref[...], g_ref[...]).astype(BF16)
        acc_scr[...] = jnp.zeros_like(acc_scr)

    def branch(slot, w_ref, w_out, hist_ref, carry_out, cw_ref, cb_ref):
        w_bf = w_ref[...].astype(BF16)
        w_out[...] = w_bf
        a = jnp.dot(h_scr[...], w_bf, preferred_element_type=F32)
        outs = []
        for s in range(n_sub):
            stage = a_scr.at[slot, s]
            stage[0:SUBLANES, :] = hist_ref[s]
            stage[SUBLANES:SUBLANES + seq, :] = a[s * seq:(s + 1) * seq]
            carry_out[s] = a[(s + 1) * seq - SUBLANES:(s + 1) * seq]
            outs.append(_conv_taps(stage, seq, cw_ref[...], cb_ref[...]))
        return jnp.concatenate(outs, axis=0)

    gate = branch(0, wg_ref, wg_out, hg_ref, cg_ref, cwg_ref, cbg_ref)
    val = branch(1, wv_ref, wv_out, hv_ref, cv_ref, cwv_ref, cbv_ref)
    wd_bf = wdn_ref[...].astype(BF16)
    wdn_out[...] = wd_bf
    acc_scr[...] += jnp.dot((jax.nn.silu(gate) * val).astype(BF16), wd_bf,
                            preferred_element_type=F32)

    @pl.when(j == pl.num_programs(0) - 1)
    def _():
        y = x_ref[...] + acc_scr[...]
        if final_norm:
            y = _rms(y, gf_ref[...])
        y_ref[...] = y


def _layer_spec(shape, layer):
    nd = len(shape)
    return pl.BlockSpec((None,) + tuple(shape[1:]), lambda *_: (layer,) + (0,) * (nd - 1),
                        pipeline_mode=pl.Buffered(1))


def _conv_ffn(x, gain, wg_bf, wv_bf, conv_w, conv_b, wdn_bf, layer, final_gain, rows):
    b, s, d = x.shape
    ff = wdn_bf.shape[0]
    assert s % rows == 0 and rows % SUBLANES == 0 and ff % FF_CHUNK == 0
    final_norm = final_gain is not None
    gf = (final_gain if final_norm else jnp.ones((d,), F32)).reshape(1, d)
    row = pl.BlockSpec((None, rows, d), lambda i, j: (i, j, 0))
    carry_spec = pl.BlockSpec((None, SUBLANES, 2 * ff), lambda i, j: (i, 0, 0))
    return pl.pallas_call(
        functools.partial(_ffn_kernel, final_norm=final_norm),
        grid=(b, s // rows),
        in_specs=[row, _const_spec((1, d)), _const_spec(wg_bf.shape), _const_spec(wv_bf.shape),
                  _layer_spec(conv_w.shape, layer), _layer_spec(conv_b.shape, layer),
                  _const_spec(wdn_bf.shape), _const_spec((1, d))],
        out_specs=[row, carry_spec],
        out_shape=[jax.ShapeDtypeStruct((b, s, d), F32),
                   jax.ShapeDtypeStruct((b, SUBLANES, 2 * ff), F32)],
        scratch_shapes=[pltpu.VMEM((rows, d), BF16),
                        pltpu.VMEM((4, rows + SUBLANES, FF_CHUNK), F32),
                        pltpu.VMEM((rows, ff), BF16)],
        compiler_params=_params(2),
        name="conv_ffn",
    )(x, gain.reshape(1, d), wg_bf, wv_bf, conv_w, conv_b, wdn_bf, gf)


def _conv_ffn_stream(x, gain, hist, w_up, conv_w, conv_b, w_down, layer, final_gain, n_sub):
    r, d = x.shape
    ff = w_down.shape[1]
    step_bytes_per_col = 2 * 3 * d * (4 + 2)
    fc = max(ff // k for k in range(1, ff // LANES + 1)
             if ff % k == 0 and (ff // k) % LANES == 0
             and (ff // k) * step_bytes_per_col <= VMEM_LIMIT // 2)
    c = ff // fc
    seq = r // n_sub
    assert ff % fc == 0 and fc % LANES == 0 and seq % SUBLANES == 0
    final_norm = final_gain is not None
    gf = (final_gain if final_norm else jnp.ones((d,), F32)).reshape(1, d)

    def cols(blk_rows, half):
        return pl.BlockSpec((None, blk_rows, fc), lambda j: (layer, 0, half * c + j))

    def hist_cols(half):
        return pl.BlockSpec((n_sub, SUBLANES, fc), lambda j: (0, 0, half * c + j))

    half_cols = pl.BlockSpec((n_sub, SUBLANES, fc), lambda j: (0, 0, j))
    up_cols = pl.BlockSpec((d, fc), lambda j: (0, j))
    dn_rows = pl.BlockSpec((fc, d), lambda j: (j, 0))
    return pl.pallas_call(
        functools.partial(_ffn_stream_kernel, n_sub=n_sub, final_norm=final_norm),
        grid=(c,),
        in_specs=[_const_spec((r, d)), _const_spec((1, d)), hist_cols(0), hist_cols(1),
                  cols(d, 0), cols(d, 1), cols(CONV_W, 0), cols(CONV_W, 1), cols(1, 0), cols(1, 1),
                  pl.BlockSpec((None, fc, d), lambda j: (layer, j, 0)),
                  _const_spec((1, d))],
        out_specs=[pl.BlockSpec((r, d), lambda j: (0, 0)), half_cols, half_cols,
                   up_cols, up_cols, dn_rows],
        out_shape=[jax.ShapeDtypeStruct((r, d), F32),
                   jax.ShapeDtypeStruct((n_sub, SUBLANES, ff), F32),
                   jax.ShapeDtypeStruct((n_sub, SUBLANES, ff), F32),
                   jax.ShapeDtypeStruct((d, ff), BF16),
                   jax.ShapeDtypeStruct((d, ff), BF16),
                   jax.ShapeDtypeStruct((ff, d), BF16)],
        scratch_shapes=[pltpu.VMEM((r, d), BF16), pltpu.VMEM((r, d), F32),
                        pltpu.VMEM((2, n_sub, seq + SUBLANES, fc), F32)],
        compiler_params=_params(1),
        name="conv_ffn_stream",
    )(x, gain.reshape(1, d), hist, hist, w_up, w_up, conv_w, conv_w, conv_b, conv_b, w_down, gf)


def _pair_bias(table, n_q, n_k, reach):
    diag = jnp.arange(n_q + n_k - 1) - (n_q - 1)
    idx = jnp.clip(reach - diag, -MAX_REL, MAX_REL) + MAX_REL
    r = table[:, idx].astype(F32) * LOG2_E
    h, m = r.shape
    flat = jnp.tile(jnp.pad(r, ((0, 0), (0, 1))), (1, n_q))[:, :n_q * m]
    bias = flat.reshape(h, n_q, m)[:, :, n_q - 1:n_q - 1 + n_k]
    return bias.reshape(h // HEADS_PER_VREG, HEADS_PER_VREG * n_q, n_k)


def _pad_hist(hist):
    pad = [(0, 0)] * (hist.ndim - 2) + [(SUBLANES - hist.shape[-2], 0), (0, 0)]
    return jnp.pad(hist, pad)


def kernel(x_prompt, x_sample, cache_a_k, cache_a_v, state_ffn_conv, ln_mix, ln_ffn, ln_final,
           a_w_qkv, a_rel_bias, a_w_o, b_w_in, b_v_norm, b_w_s, b_bias_s, b_w_out,
           f_w_up, f_conv_w, f_conv_b, f_w_down):
    bp, sp, d = x_prompt.shape
    bs, ts, _ = x_sample.shape
    depth = ln_mix.shape[0]
    ca = cache_a_k.shape[2]
    keep = min(REACH, sp)
    two_ff = f_w_up.shape[2]
    hist_rows = CONV_W - 1
    conv_b_all = f_conv_b.reshape(depth, 1, two_ff)

    xp = x_prompt
    xs = x_sample.reshape(1, bs * ts, d)
    kp_l, vp_l, ks_l, vs_l, gv_l, cp_l, cs_l = [], [], [], [], [], [], []

    for i in range(depth):
        j = i // 2
        if i % 2 == 0:
            wqkv = a_w_qkv[j]
            wo = a_w_o[j]
            q, k, v, kt, vt = _qkv_proj(xp, ln_mix[i], wqkv, keep, ROW_TILE)
            xp = _attn_prompt(q, k, v, xp, wo, a_rel_bias[j])
            kp_l.append(kt.reshape(bp, keep, N_HEADS, HEAD_DIM))
            vp_l.append(vt.reshape(bp, keep, N_HEADS, HEAD_DIM))
            q, k, v, kt, vt = _qkv_proj(xs, ln_mix[i], wqkv, bs * ts, bs * ts)
            xs = _attn_sample(
                q.reshape(bs, ts, d), k.reshape(bs, ts, d), v.reshape(bs, ts, d),
                cache_a_k[j].reshape(bs, ca, d), cache_a_v[j].reshape(bs, ca, d),
                xs.reshape(bs, ts, d), wo, _pair_bias(a_rel_bias[j], ts, ca + ts, ca),
            ).reshape(1, bs * ts, d)
            ks_l.append(kt.reshape(bs, ts, N_HEADS, HEAD_DIM))
            vs_l.append(vt.reshape(bs, ts, N_HEADS, HEAD_DIM))
        else:
            win = b_w_in[j]
            wout = b_w_out[j]
            pos = jnp.arange(MLP_BLOCK)
            causal = (pos[None, :] // CHUNK) <= (pos[:, None] // CHUNK)
            ws = jnp.where(causal, b_w_s[j], 0.0).astype(BF16)
            g = ws.shape[0]

            def pair_rows(w):
                w = w.reshape(g // HEADS_PER_VREG, HEADS_PER_VREG, w.shape[1], w.shape[2])
                return jnp.transpose(w, (0, 2, 1, 3)).reshape(
                    g // HEADS_PER_VREG, w.shape[2], HEADS_PER_VREG * w.shape[3])

            def bias_rows(bias):
                return jnp.repeat(bias.T.astype(F32), d // g, axis=1)

            xp = _gmlp(xp, ln_mix[i], win, b_v_norm[j], pair_rows(ws), bias_rows(b_bias_s[j]),
                       wout, ROW_TILE, MLP_BLOCK, False)[0]
            xs, gv = _gmlp(xs, ln_mix[i], win, b_v_norm[j], pair_rows(ws[:, :ts, :ts]),
                           bias_rows(b_bias_s[j][:, :ts]), wout, bs * ts, ts, True)
            gv_l.append(gv.reshape(bs, ts, d))

        fin = ln_final if i == depth - 1 else None
        xs2, cs_g, cs_v, wg, wv, wdn = _conv_ffn_stream(
            xs[0], ln_ffn[i], _pad_hist(state_ffn_conv[i]), f_w_up, f_conv_w, conv_b_all,
            f_w_down, i, fin, bs)
        xs = xs2[None]
        xp, cp = _conv_ffn(xp, ln_ffn[i], wg, wv, f_conv_w, conv_b_all, wdn, i, fin,
                           FFN_ROW_TILE)
        cp_l.append(cp[:, SUBLANES - hist_rows:, :])
        cs_l.append(jnp.concatenate([cs_g, cs_v], axis=-1)[:, SUBLANES - hist_rows:, :])

    return (xp, xs.reshape(bs, ts, d), jnp.stack(kp_l), jnp.stack(vp_l), jnp.stack(ks_l),
            jnp.stack(vs_l), jnp.stack(gv_l), jnp.stack(cp_l), jnp.stack(cs_l))
```

```python
import functools

import jax
import jax.numpy as jnp
import numpy as np
from jax import lax
from jax.experimental import pallas as pl
from jax.experimental.pallas import tpu as pltpu

F32 = jnp.float32
BF16 = jnp.bfloat16

CHUNK = 64
PAST_CHUNKS = 8
REACH = PAST_CHUNKS * CHUNK
BAND = REACH + CHUNK
MAX_REL = 128
N_HEADS = 16
HEAD_DIM = 64
MLP_BLOCK = 128
N_GROUPS = 16
CONV_W = 3
EPS = 1e-6
NEG = -1e30
LOG2_E = float(np.log2(np.e))
Q_SCALE = HEAD_DIM ** -0.5 * LOG2_E

LANES = 128
SUBLANES = 8
MXU_DIM = 256
HEADS_PER_VREG = LANES // HEAD_DIM
VMEM_LIMIT = 56 * 1024 * 1024

ROW_TILE = 512
FFN_ROW_TILE = 512
FF_CHUNK = MXU_DIM
GROUP_CHUNKS = LANES // CHUNK
GROUP_ROWS = GROUP_CHUNKS * CHUNK
GROUP_KEYS = REACH + GROUP_ROWS
VAR_COL0 = (REACH - MAX_REL) // LANES * LANES
ATTN_GROUPS_PER_TRIP = 4
PROJ_GROUPS = 2


def _rms(x, g):
    return x * lax.rsqrt(jnp.mean(x * x, axis=-1, keepdims=True) + EPS) * g


def _const_spec(shape):
    nd = len(shape)
    return pl.BlockSpec(shape, lambda *_: (0,) * nd, pipeline_mode=pl.Buffered(1))


def _params(n_grid):
    return pltpu.CompilerParams(
        dimension_semantics=("arbitrary",) * n_grid, vmem_limit_bytes=VMEM_LIMIT)


def _qkv_kernel(x_ref, g_ref, w_ref, q_ref, k_ref, v_ref, kt_ref, vt_ref, *, d, tail_tile0):
    t = pl.program_id(1)
    h = _rms(x_ref[...], g_ref[...]).astype(BF16)
    q = jnp.dot(h, w_ref[:, 0:d].astype(BF16), preferred_element_type=F32)
    q_ref[...] = (q * Q_SCALE).astype(BF16)
    k = jnp.dot(h, w_ref[:, d:2 * d].astype(BF16), preferred_element_type=F32)
    k_ref[...] = k.astype(BF16)
    v = jnp.dot(h, w_ref[:, 2 * d:3 * d].astype(BF16), preferred_element_type=F32)
    v_ref[...] = v.astype(BF16)

    @pl.when(t >= tail_tile0)
    def _():
        kt_ref[...] = k
        vt_ref[...] = v


def _qkv_proj(x, gain, w_bf, keep, tm):
    b, s, d = x.shape
    nt = s // tm
    assert s % tm == 0 and keep % tm == 0
    tail_tile0 = nt - keep // tm
    row = pl.BlockSpec((None, tm, d), lambda i, j: (i, j, 0))
    tail = pl.BlockSpec((None, tm, d), lambda i, j: (i, jnp.maximum(j - tail_tile0, 0), 0))
    return pl.pallas_call(
        functools.partial(_qkv_kernel, d=d, tail_tile0=tail_tile0),
        grid=(b, nt),
        in_specs=[row, _const_spec((1, d)), _const_spec((d, 3 * d))],
        out_specs=[row, row, row, tail, tail],
        out_shape=[jax.ShapeDtypeStruct((b, s, d), BF16)] * 3
        + [jax.ShapeDtypeStruct((b, keep, d), F32)] * 2,
        compiler_params=_params(2),
        name="qkv_proj",
    )(x, gain.reshape(1, d), w_bf)


def _pair_scores(qp, kb):
    lane = lax.broadcasted_iota(jnp.int32, qp.shape, 1)
    qf = qp.astype(F32)
    zero = jnp.zeros_like(qf)
    q_bd = jnp.concatenate(
        [jnp.where(lane < HEAD_DIM, qf, zero), jnp.where(lane >= HEAD_DIM, qf, zero)],
        axis=0).astype(BF16)
    return lax.dot_general(q_bd, kb, (((1,), (1,)), ((), ())), preferred_element_type=F32)


def _pair_softmax(s, bias, lim):
    if bias is not None:
        s = s + bias
    if lim is not None:
        col = lax.broadcasted_iota(jnp.int32, s.shape, 1)
        s = jnp.where(col >= lim, s, NEG)
    m = jnp.max(s, axis=-1, keepdims=True)
    e = jnp.exp2(s - m)
    return e.astype(BF16), 1.0 / jnp.sum(e, axis=-1, keepdims=True)


def _pair_output(e, inv_l, vb):
    nq = e.shape[0] // HEADS_PER_VREG
    lane = lax.broadcasted_iota(jnp.int32, (nq, LANES), 1)
    pv = jnp.dot(e, vb, preferred_element_type=F32) * inv_l
    return jnp.where(lane < HEAD_DIM, pv[:nq], pv[nq:])


def _pipelined_pairs(n, scores, softmax, output, lead_scores=1, lead_softmax=1, done=None):
    s, e = {}, {}
    for step in range(n + lead_scores + lead_softmax):
        i_sm = step - lead_scores
        i_out = i_sm - lead_softmax
        if step < n:
            s[step] = scores(step)
        if 0 <= i_sm < n:
            e[i_sm] = softmax(i_sm, s.pop(i_sm))
        if i_out >= 0:
            output(i_out, *e.pop(i_out))
            if done is not None:
                done(i_out)


def _band_bias_init(u_ref, bias_scr, edge_scr):
    n_var = GROUP_KEYS - VAR_COL0
    ulen = u_ref.shape[1]
    row = lax.broadcasted_iota(jnp.int32, (GROUP_ROWS, n_var), 0)
    col = lax.broadcasted_iota(jnp.int32, (GROUP_ROWS, n_var), 1) + VAR_COL0
    past_band = col >= row // CHUNK * CHUNK + BAND
    for h in range(u_ref.shape[0]):
        per_diag = jnp.broadcast_to(u_ref[h:h + 1, :], (GROUP_ROWS, ulen))
        skew = pltpu.roll(per_diag, ulen - (GROUP_ROWS - 1), 1, stride=1, stride_axis=0)
        p, half = divmod(h, HEADS_PER_VREG)
        bias_scr[p, half * GROUP_ROWS:(half + 1) * GROUP_ROWS, :] = jnp.where(
            past_band, NEG, skew[:, :n_var])
    row = lax.broadcasted_iota(jnp.int32, edge_scr.shape, 0) % GROUP_ROWS
    col = lax.broadcasted_iota(jnp.int32, edge_scr.shape, 1)
    edge_scr[...] = jnp.where(col < row // CHUNK * CHUNK, NEG, 0.0)


def _attn_prompt_kernel(q_ref, k_ref, v_ref, x_ref, wo_ref, u_ref, y_ref,
                        kbuf, vbuf, obuf, bias_scr, edge_scr, *, tq):
    t = pl.program_id(1)
    n_pairs = q_ref.shape[1] // LANES

    @pl.when((pl.program_id(0) == 0) & (t == 0))
    def _():
        _band_bias_init(u_ref, bias_scr, edge_scr)

    @pl.when(t == 0)
    def _():
        kbuf[0:tq, :] = jnp.zeros((tq, kbuf.shape[1]), BF16)
        vbuf[0:tq, :] = jnp.zeros((tq, vbuf.shape[1]), BF16)

    @pl.when(t > 0)
    def _():
        kbuf[0:tq, :] = kbuf[tq:2 * tq, :]
        vbuf[0:tq, :] = vbuf[tq:2 * tq, :]

    kbuf[tq:2 * tq, :] = k_ref[...]
    vbuf[tq:2 * tq, :] = v_ref[...]

    def chunk_loop(masked):
        span = ATTN_GROUPS_PER_TRIP * GROUP_ROWS

        def trip_body(i, carry):
            base = pl.multiple_of(i * span, span)

            def item(n):
                g, p = divmod(n, n_pairs)
                return base + g * GROUP_ROWS, slice(p * LANES, (p + 1) * LANES), p

            def scores(n):
                r0, cols, _ = item(n)
                return _pair_scores(q_ref[pl.ds(r0, GROUP_ROWS), cols],
                                    kbuf[pl.ds(r0, GROUP_KEYS), cols])

            def softmax(n, s):
                r0, _, p = item(n)
                s = jnp.concatenate(
                    [s[:, :LANES] + edge_scr[...], s[:, LANES:VAR_COL0],
                     s[:, VAR_COL0:] + bias_scr[p]], axis=1)
                lim = (REACH - (t * tq + r0)) if masked else None
                return _pair_softmax(s, None, lim)

            def output(n, e, inv_l):
                r0, cols, _ = item(n)
                o = _pair_output(e, inv_l, vbuf[pl.ds(r0, GROUP_KEYS), cols])
                obuf[pl.ds(r0, GROUP_ROWS), cols] = o.astype(BF16)

            def done(n):
                if (n + 1) % (PROJ_GROUPS * n_pairs) == 0:
                    rows = PROJ_GROUPS * GROUP_ROWS
                    r0 = pl.multiple_of(base + (n + 1) // n_pairs * GROUP_ROWS - rows, rows)
                    y_ref[pl.ds(r0, rows), :] = x_ref[pl.ds(r0, rows), :] + jnp.dot(
                        obuf[pl.ds(r0, rows), :], wo_ref[...].astype(BF16),
                        preferred_element_type=F32)

            _pipelined_pairs(ATTN_GROUPS_PER_TRIP * n_pairs, scores, softmax, output, 1, 2, done)
            return carry

        lax.fori_loop(0, tq // span, trip_body, 0)

    @pl.when(t == 0)
    def _():
        chunk_loop(True)

    @pl.when(t > 0)
    def _():
        chunk_loop(False)


def _attn_prompt(q, k, v, x, wo, table):
    b, s, d = x.shape
    tq = REACH
    assert s % tq == 0
    h = table.shape[0]
    n_diag = GROUP_ROWS - 1 + GROUP_KEYS - VAR_COL0
    diag = jnp.arange(n_diag) + (VAR_COL0 - GROUP_ROWS + 1)
    idx = jnp.clip(REACH - diag, -MAX_REL, MAX_REL) + MAX_REL
    u = (table[:, idx] - table[:, 2 * MAX_REL:]).astype(F32) * LOG2_E
    u = jnp.pad(u, ((0, 0), (0, pl.next_power_of_2(n_diag) - n_diag)))
    row = pl.BlockSpec((None, tq, d), lambda i, j: (i, j, 0))
    rows2 = HEADS_PER_VREG * GROUP_ROWS
    return pl.pallas_call(
        functools.partial(_attn_prompt_kernel, tq=tq),
        grid=(b, s // tq),
        in_specs=[row, row, row, row, _const_spec((d, d)), _const_spec(u.shape)],
        out_specs=row,
        out_shape=jax.ShapeDtypeStruct((b, s, d), F32),
        scratch_shapes=[pltpu.VMEM((2 * tq, d), BF16), pltpu.VMEM((2 * tq, d), BF16),
                        pltpu.VMEM((tq, d), BF16),
                        pltpu.VMEM((h // HEADS_PER_VREG, rows2, GROUP_KEYS - VAR_COL0), F32),
                        pltpu.VMEM((rows2, LANES), F32)],
        compiler_params=_params(2),
        name="attn_prompt",
    )(q, k, v, x, wo, u)


def _attn_sample_kernel(q_ref, k_ref, v_ref, ck_ref, cv_ref, x_ref, wo_ref, bias_ref, y_ref,
                        obuf):
    n_pairs = q_ref.shape[1] // LANES
    k_all = jnp.concatenate([ck_ref[...], k_ref[...]], axis=0)
    v_all = jnp.concatenate([cv_ref[...], v_ref[...]], axis=0)
    cols = [slice(p * LANES, (p + 1) * LANES) for p in range(n_pairs)]

    def scores(p):
        return _pair_scores(q_ref[:, cols[p]], k_all[:, cols[p]])

    def softmax(p, s):
        return _pair_softmax(s, bias_ref[p], None)

    def output(p, e, inv_l):
        obuf[:, cols[p]] = _pair_output(e, inv_l, v_all[:, cols[p]]).astype(BF16)

    _pipelined_pairs(n_pairs, scores, softmax, output)
    y_ref[...] = x_ref[...] + jnp.dot(obuf[...], wo_ref[...].astype(BF16),
                                      preferred_element_type=F32)


def _attn_sample(q, k, v, cache_k, cache_v, x, wo_bf, bias_pairs):
    b, tn, d = x.shape
    ca = cache_k.shape[1]
    new = pl.BlockSpec((None, tn, d), lambda i: (i, 0, 0))
    old = pl.BlockSpec((None, ca, d), lambda i: (i, 0, 0))
    return pl.pallas_call(
        _attn_sample_kernel,
        grid=(b,),
        in_specs=[new, new, new, old, old, new, _const_spec((d, d)),
                  _const_spec(bias_pairs.shape)],
        out_specs=new,
        out_shape=jax.ShapeDtypeStruct((b, tn, d), F32),
        scratch_shapes=[pltpu.VMEM((tn, d), BF16)],
        compiler_params=_params(1),
        name="attn_sample",
    )(q, k, v, cache_k, cache_v, x, wo_bf, bias_pairs)


def _gmlp_kernel(x_ref, g_ref, win_ref, vg_ref, ws_ref, bs_ref, wout_ref, *rest,
                 blk, emit_v):
    if emit_v:
        y_ref, vout_ref, gated = rest
    else:
        y_ref, gated = rest
    rows, d = x_ref.shape
    n_pairs = d // LANES
    x = x_ref[...]
    h = _rms(x, g_ref[...]).astype(BF16)
    def gelu(z):
        return 0.5 * z * (1.0 + lax.erf(z * np.float32(np.sqrt(0.5))))

    v = _rms(gelu(jnp.dot(h, win_ref[:, d:].astype(BF16), preferred_element_type=F32)),
             vg_ref[...])
    u = gelu(jnp.dot(h, win_ref[:, :d].astype(BF16), preferred_element_type=F32))
    if emit_v:
        vout_ref[...] = v
    n_blk = rows // blk
    lane = lax.broadcasted_iota(jnp.int32, (blk, n_blk * LANES), 1) & (LANES - 1)
    zero = jnp.zeros((blk, n_blk * LANES), F32)
    for p in range(n_pairs):
        cols = slice(p * LANES, (p + 1) * LANES)
        vcat = jnp.concatenate([v[r * blk:(r + 1) * blk, cols] for r in range(n_blk)], axis=1)
        v_stack = jnp.concatenate(
            [jnp.where(lane < HEAD_DIM, vcat, zero), jnp.where(lane >= HEAD_DIM, vcat, zero)],
            axis=0).astype(BF16)
        m = jnp.dot(ws_ref[p], v_stack, preferred_element_type=F32)
        for r in range(n_blk):
            rs = slice(r * blk, (r + 1) * blk)
            mixed = m[:, r * LANES:(r + 1) * LANES] + bs_ref[:, cols]
            gated[rs, cols] = (u[rs, cols] * mixed).astype(BF16)
    y_ref[...] = x + jnp.dot(gated[...], wout_ref[...].astype(BF16), preferred_element_type=F32)


def _gmlp(x, gain, win_bf, v_gain, ws_pairs, bs_rows, wout_bf, rows, blk, emit_v):
    b, s, d = x.shape
    w = win_bf.shape[1] - d
    assert s % rows == 0 and rows % blk == 0 and w == d
    row = pl.BlockSpec((None, rows, d), lambda i, j: (i, j, 0))
    out_specs = [row]
    out_shape = [jax.ShapeDtypeStruct((b, s, d), F32)]
    if emit_v:
        out_specs.append(row)
        out_shape.append(jax.ShapeDtypeStruct((b, s, w), F32))
    return pl.pallas_call(
        functools.partial(_gmlp_kernel, blk=blk, emit_v=emit_v),
        grid=(b, s // rows),
        in_specs=[row, _const_spec((1, d)), _const_spec(win_bf.shape), _const_spec((1, w)),
                  _const_spec(ws_pairs.shape), _const_spec(bs_rows.shape),
                  _const_spec(wout_bf.shape)],
        out_specs=out_specs,
        out_shape=out_shape,
        scratch_shapes=[pltpu.VMEM((rows, w), BF16)],
        compiler_params=_params(2),
        name="gmlp",
    )(x, gain.reshape(1, d), win_bf, v_gain.reshape(1, w), ws_pairs, bs_rows, wout_bf)


def _conv_taps(stage, n, w, cb):
    c = cb + stage[SUBLANES:SUBLANES + n, :] * w[2:3]
    c = c + stage[SUBLANES - 1:SUBLANES - 1 + n, :] * w[1:2]
    return c + stage[SUBLANES - 2:SUBLANES - 2 + n, :] * w[0:1]


def _ffn_kernel(x_ref, g_ref, wg_ref, wv_ref, cw_ref, cb_ref, wdn_ref, gf_ref,
                y_ref, carry_ref, h_scr, a_scr, act_scr, *, final_norm):
    t = pl.program_id(1)
    rows = x_ref.shape[0]
    ff = wdn_ref.shape[0]
    fc = FF_CHUNK

    @pl.when(t == 0)
    def _():
        carry_ref[...] = jnp.zeros_like(carry_ref)

    x = x_ref[...]
    h_scr[...] = _rms(x, g_ref[...]).astype(BF16)

    for j in range(ff // fc):
        chunk = slice(j * fc, (j + 1) * fc)
        halves = []
        for half, w_ref in enumerate((wg_ref, wv_ref)):
            cols = slice(half * ff + j * fc, half * ff + (j + 1) * fc)
            stage = a_scr.at[2 * (j % 2) + half]
            a = jnp.dot(h_scr[...], w_ref[:, chunk], preferred_element_type=F32)
            stage[0:SUBLANES, :] = carry_ref[:, cols]
            stage[SUBLANES:SUBLANES + rows, :] = a
            carry_ref[:, cols] = a[rows - SUBLANES:rows]
            halves.append(_conv_taps(stage, rows, cw_ref[:, cols], cb_ref[:, cols]))
        act_scr[:, chunk] = (jax.nn.silu(halves[0]) * halves[1]).astype(BF16)

    y = x + jnp.dot(act_scr[...], wdn_ref[...], preferred_element_type=F32)
    if final_norm:
        y = _rms(y, gf_ref[...])
    y_ref[...] = y


def _ffn_stream_kernel(x_ref, g_ref, hg_ref, hv_ref, wg_ref, wv_ref, cwg_ref, cwv_ref, cbg_ref,
                       cbv_ref, wdn_ref, gf_ref, y_ref, cg_ref, cv_ref, wg_out, wv_out, wdn_out,
                       h_scr, acc_scr, a_scr, *, n_sub, final_norm):
    j = pl.program_id(0)
    rows = x_ref.shape[0]
    seq = rows // n_sub

    @pl.when(j == 0)
    def _():
        h_scr[...] = _rms(x_ref[...], g_ref[...]).astype(BF16)
        acc_scr[...] = jnp.zeros_like(acc_scr)

    def branch(slot, w_ref, w_out, hist_ref, carry_out, cw_ref, cb_ref):
        w_bf = w_ref[...].astype(BF16)
        w_out[...] = w_bf
        a = jnp.dot(h_scr[...], w_bf, preferred_element_type=F32)
        outs = []
        for s in range(n_sub):
            stage = a_scr.at[slot, s]
            stage[0:SUBLANES, :] = hist_ref[s]
            stage[SUBLANES:SUBLANES + seq, :] = a[s * seq:(s + 1) * seq]
            carry_out[s] = a[(s + 1) * seq - SUBLANES:(s + 1) * seq]
            outs.append(_conv_taps(stage, seq, cw_ref[...], cb_ref[...]))
        return jnp.concatenate(outs, axis=0)

    gate = branch(0, wg_ref, wg_out, hg_ref, cg_ref, cwg_ref, cbg_ref)
    val = branch(1, wv_ref, wv_out, hv_ref, cv_ref, cwv_ref, cbv_ref)
    wd_bf = wdn_ref[...].astype(BF16)
    wdn_out[...] = wd_bf
    acc_scr[...] += jnp.dot((jax.nn.silu(gate) * val).astype(BF16), wd_bf,
                            preferred_element_type=F32)

    @pl.when(j == pl.num_programs(0) - 1)
    def _():
        y = x_ref[...] + acc_scr[...]
        if final_norm:
            y = _rms(y, gf_ref[...])
        y_ref[...] = y


def _layer_spec(shape, layer):
    nd = len(shape)
    return pl.BlockSpec((None,) + tuple(shape[1:]), lambda *_: (layer,) + (0,) * (nd - 1),
                        pipeline_mode=pl.Buffered(1))


def _conv_ffn(x, gain, wg_bf, wv_bf, conv_w, conv_b, wdn_bf, layer, final_gain, rows):
    b, s, d = x.shape
    ff = wdn_bf.shape[0]
    assert s % rows == 0 and rows % SUBLANES == 0 and ff % FF_CHUNK == 0
    final_norm = final_gain is not None
    gf = (final_gain if final_norm else jnp.ones((d,), F32)).reshape(1, d)
    row = pl.BlockSpec((None, rows, d), lambda i, j: (i, j, 0))
    carry_spec = pl.BlockSpec((None, SUBLANES, 2 * ff), lambda i, j: (i, 0, 0))
    return pl.pallas_call(
        functools.partial(_ffn_kernel, final_norm=final_norm),
        grid=(b, s // rows),
        in_specs=[row, _const_spec((1, d)), _const_spec(wg_bf.shape), _const_spec(wv_bf.shape),
                  _layer_spec(conv_w.shape, layer), _layer_spec(conv_b.shape, layer),
                  _const_spec(wdn_bf.shape), _const_spec((1, d))],
        out_specs=[row, carry_spec],
        out_shape=[jax.ShapeDtypeStruct((b, s, d), F32),
                   jax.ShapeDtypeStruct((b, SUBLANES, 2 * ff), F32)],
        scratch_shapes=[pltpu.VMEM((rows, d), BF16),
                        pltpu.VMEM((4, rows + SUBLANES, FF_CHUNK), F32),
                        pltpu.VMEM((rows, ff), BF16)],
        compiler_params=_params(2),
        name="conv_ffn",
    )(x, gain.reshape(1, d), wg_bf, wv_bf, conv_w, conv_b, wdn_bf, gf)


def _conv_ffn_stream(x, gain, hist, w_up, conv_w, conv_b, w_down, layer, final_gain, n_sub):
    r, d = x.shape
    ff = w_down.shape[1]
    step_bytes_per_col = 2 * 3 * d * (4 + 2)
    fc = max(ff // k for k in range(1, ff // LANES + 1)
             if ff % k == 0 and (ff // k) % LANES == 0
             and (ff // k) * step_bytes_per_col <= VMEM_LIMIT // 2)
    c = ff // fc
    seq = r // n_sub
    assert ff % fc == 0 and fc % LANES == 0 and seq % SUBLANES == 0
    final_norm = final_gain is not None
    gf = (final_gain if final_norm else jnp.ones((d,), F32)).reshape(1, d)

    def cols(blk_rows, half):
        return pl.BlockSpec((None, blk_rows, fc), lambda j: (layer, 0, half * c + j))

    def hist_cols(half):
        return pl.BlockSpec((n_sub, SUBLANES, fc), lambda j: (0, 0, half * c + j))

    half_cols = pl.BlockSpec((n_sub, SUBLANES, fc), lambda j: (0, 0, j))
    up_cols = pl.BlockSpec((d, fc), lambda j: (0, j))
    dn_rows = pl.BlockSpec((fc, d), lambda j: (j, 0))
    return pl.pallas_call(
        functools.partial(_ffn_stream_kernel, n_sub=n_sub, final_norm=final_norm),
        grid=(c,),
        in_specs=[_const_spec((r, d)), _const_spec((1, d)), hist_cols(0), hist_cols(1),
                  cols(d, 0), cols(d, 1), cols(CONV_W, 0), cols(CONV_W, 1), cols(1, 0), cols(1, 1),
                  pl.BlockSpec((None, fc, d), lambda j: (layer, j, 0)),
                  _const_spec((1, d))],
        out_specs=[pl.BlockSpec((r, d), lambda j: (0, 0)), half_cols, half_cols,
                   up_cols, up_cols, dn_rows],
        out_shape=[jax.ShapeDtypeStruct((r, d), F32),
                   jax.ShapeDtypeStruct((n_sub, SUBLANES, ff), F32),
                   jax.ShapeDtypeStruct((n_sub, SUBLANES, ff), F32),
                   jax.ShapeDtypeStruct((d, ff), BF16),
                   jax.ShapeDtypeStruct((d, ff), BF16),
                   jax.ShapeDtypeStruct((ff, d), BF16)],
        scratch_shapes=[pltpu.VMEM((r, d), BF16), pltpu.VMEM((r, d), F32),
                        pltpu.VMEM((2, n_sub, seq + SUBLANES, fc), F32)],
        compiler_params=_params(1),
        name="conv_ffn_stream",
    )(x, gain.reshape(1, d), hist, hist, w_up, w_up, conv_w, conv_w, conv_b, conv_b, w_down, gf)


def _pair_bias(table, n_q, n_k, reach):
    diag = jnp.arange(n_q + n_k - 1) - (n_q - 1)
    idx = jnp.clip(reach - diag, -MAX_REL, MAX_REL) + MAX_REL
    r = table[:, idx].astype(F32) * LOG2_E
    h, m = r.shape
    flat = jnp.tile(jnp.pad(r, ((0, 0), (0, 1))), (1, n_q))[:, :n_q * m]
    bias = flat.reshape(h, n_q, m)[:, :, n_q - 1:n_q - 1 + n_k]
    return bias.reshape(h // HEADS_PER_VREG, HEADS_PER_VREG * n_q, n_k)


def _pad_hist(hist):
    pad = [(0, 0)] * (hist.ndim - 2) + [(SUBLANES - hist.shape[-2], 0), (0, 0)]
    return jnp.pad(hist, pad)


def kernel(x_prompt, x_sample, cache_a_k, cache_a_v, state_ffn_conv, ln_mix, ln_ffn, ln_final,
           a_w_qkv, a_rel_bias, a_w_o, b_w_in, b_v_norm, b_w_s, b_bias_s, b_w_out,
           f_w_up, f_conv_w, f_conv_b, f_w_down):
    bp, sp, d = x_prompt.shape
    bs, ts, _ = x_sample.shape
    depth = ln_mix.shape[0]
    ca = cache_a_k.shape[2]
    keep = min(REACH, sp)
    two_ff = f_w_up.shape[2]
    hist_rows = CONV_W - 1
    conv_b_all = f_conv_b.reshape(depth, 1, two_ff)

    xp = x_prompt
    xs = x_sample.reshape(1, bs * ts, d)
    kp_l, vp_l, ks_l, vs_l, gv_l, cp_l, cs_l = [], [], [], [], [], [], []

    for i in range(depth):
        j = i // 2
        if i % 2 == 0:
            wqkv = a_w_qkv[j]
            wo = a_w_o[j]
            q, k, v, kt, vt = _qkv_proj(xp, ln_mix[i], wqkv, keep, ROW_TILE)
            xp = _attn_prompt(q, k, v, xp, wo, a_rel_bias[j])
            kp_l.append(kt.reshape(bp, keep, N_HEADS, HEAD_DIM))
            vp_l.append(vt.reshape(bp, keep, N_HEADS, HEAD_DIM))
            q, k, v, kt, vt = _qkv_proj(xs, ln_mix[i], wqkv, bs * ts, bs * ts)
            xs = _attn_sample(
                q.reshape(bs, ts, d), k.reshape(bs, ts, d), v.reshape(bs, ts, d),
                cache_a_k[j].reshape(bs, ca, d).astype(BF16),
                cache_a_v[j].reshape(bs, ca, d).astype(BF16),
                xs.reshape(bs, ts, d), wo, _pair_bias(a_rel_bias[j], ts, ca + ts, ca),
            ).reshape(1, bs * ts, d)
            ks_l.append(kt.reshape(bs, ts, N_HEADS, HEAD_DIM))
            vs_l.append(vt.reshape(bs, ts, N_HEADS, HEAD_DIM))
        else:
            win = b_w_in[j]
            wout = b_w_out[j]
            pos = jnp.arange(MLP_BLOCK)
            causal = (pos[None, :] // CHUNK) <= (pos[:, None] // CHUNK)
            ws = jnp.where(causal, b_w_s[j], 0.0).astype(BF16)
            g = ws.shape[0]

            def pair_rows(w):
                w = w.reshape(g // HEADS_PER_VREG, HEADS_PER_VREG, w.shape[1], w.shape[2])
                return jnp.transpose(w, (0, 2, 1, 3)).reshape(
                    g // HEADS_PER_VREG, w.shape[2], HEADS_PER_VREG * w.shape[3])

            def bias_rows(bias):
                return jnp.repeat(bias.T.astype(F32), d // g, axis=1)

            xp = _gmlp(xp, ln_mix[i], win, b_v_norm[j], pair_rows(ws), bias_rows(b_bias_s[j]),
                       wout, ROW_TILE, MLP_BLOCK, False)[0]
            xs, gv = _gmlp(xs, ln_mix[i], win, b_v_norm[j], pair_rows(ws[:, :ts, :ts]),
                           bias_rows(b_bias_s[j][:, :ts]), wout, bs * ts, ts, True)
            gv_l.append(gv.reshape(bs, ts, d))

        fin = ln_final if i == depth - 1 else None
        xs2, cs_g, cs_v, wg, wv, wdn = _conv_ffn_stream(
            xs[0], ln_ffn[i], _pad_hist(state_ffn_conv[i]), f_w_up, f_conv_w, conv_b_all,
            f_w_down, i, fin, bs)
        xs = xs2[None]
        xp, cp = _conv_ffn(xp, ln_ffn[i], wg, wv, f_conv_w, conv_b_all, wdn, i, fin,
                           FFN_ROW_TILE)
        cp_l.append(cp[:, SUBLANES - hist_rows:, :])
        cs_l.append(jnp.concatenate([cs_g, cs_v], axis=-1)[:, SUBLANES - hist_rows:, :])

    return (xp, xs.reshape(bs, ts, d), jnp.stack(kp_l), jnp.stack(vp_l), jnp.stack(ks_l),
            jnp.stack(vs_l), jnp.stack(gv_l), jnp.stack(cp_l), jnp.stack(cs_l))
```

```python
import functools

import jax
import jax.numpy as jnp
import numpy as np
from jax import lax
from jax.experimental import pallas as pl
from jax.experimental.pallas import tpu as pltpu

F32 = jnp.float32
BF16 = jnp.bfloat16

CHUNK = 64
PAST_CHUNKS = 8
REACH = PAST_CHUNKS * CHUNK
BAND = REACH + CHUNK
MAX_REL = 128
N_HEADS = 16
HEAD_DIM = 64
MLP_BLOCK = 128
N_GROUPS = 16
CONV_W = 3
EPS = 1e-6
NEG = -1e30
LOG2_E = float(np.log2(np.e))
Q_SCALE = HEAD_DIM ** -0.5 * LOG2_E

LANES = 128
SUBLANES = 8
MXU_DIM = 256
HEADS_PER_VREG = LANES // HEAD_DIM
VMEM_LIMIT = 56 * 1024 * 1024

ROW_TILE = 1024
FFN_ROW_TILE = 1024
FF_CHUNK = MXU_DIM
GROUP_CHUNKS = LANES // CHUNK
GROUP_ROWS = GROUP_CHUNKS * CHUNK
GROUP_KEYS = REACH + GROUP_ROWS
VAR_COL0 = (REACH - MAX_REL) // LANES * LANES
ATTN_GROUPS_PER_TRIP = 4
PROJ_GROUPS = 2


def _rms(x, g):
    return x * lax.rsqrt(jnp.mean(x * x, axis=-1, keepdims=True) + EPS) * g


def _const_spec(shape):
    nd = len(shape)
    return pl.BlockSpec(shape, lambda *_: (0,) * nd, pipeline_mode=pl.Buffered(1))


def _params(n_grid):
    return pltpu.CompilerParams(
        dimension_semantics=("arbitrary",) * n_grid, vmem_limit_bytes=VMEM_LIMIT)


def _qkv_kernel(x_ref, g_ref, w_ref, q_ref, k_ref, v_ref, kt_ref, vt_ref, *, d, tail_tile0):
    t = pl.program_id(1)
    h = _rms(x_ref[...], g_ref[...]).astype(BF16)
    q = jnp.dot(h, w_ref[:, 0:d].astype(BF16), preferred_element_type=F32)
    q_ref[...] = (q * Q_SCALE).astype(BF16)
    k = jnp.dot(h, w_ref[:, d:2 * d].astype(BF16), preferred_element_type=F32)
    k_ref[...] = k.astype(BF16)
    v = jnp.dot(h, w_ref[:, 2 * d:3 * d].astype(BF16), preferred_element_type=F32)
    v_ref[...] = v.astype(BF16)

    @pl.when(t >= tail_tile0)
    def _():
        tail_rows = kt_ref.shape[0]
        kt_ref[...] = k[k.shape[0] - tail_rows:]
        vt_ref[...] = v[v.shape[0] - tail_rows:]


def _qkv_proj(x, gain, w_bf, keep, tm):
    b, s, d = x.shape
    nt = s // tm
    tail_rows = min(tm, keep)
    assert s % tm == 0 and keep % tail_rows == 0
    tail_tile0 = nt - keep // tail_rows
    row = pl.BlockSpec((None, tm, d), lambda i, j: (i, j, 0))
    tail = pl.BlockSpec((None, tail_rows, d),
                        lambda i, j: (i, jnp.maximum(j - tail_tile0, 0), 0))
    return pl.pallas_call(
        functools.partial(_qkv_kernel, d=d, tail_tile0=tail_tile0),
        grid=(b, nt),
        in_specs=[row, _const_spec((1, d)), _const_spec((d, 3 * d))],
        out_specs=[row, row, row, tail, tail],
        out_shape=[jax.ShapeDtypeStruct((b, s, d), BF16)] * 3
        + [jax.ShapeDtypeStruct((b, keep, d), F32)] * 2,
        compiler_params=_params(2),
        name="qkv_proj",
    )(x, gain.reshape(1, d), w_bf)


def _pair_scores(qp, kb):
    lane = lax.broadcasted_iota(jnp.int32, qp.shape, 1)
    qf = qp.astype(F32)
    zero = jnp.zeros_like(qf)
    q_bd = jnp.concatenate(
        [jnp.where(lane < HEAD_DIM, qf, zero), jnp.where(lane >= HEAD_DIM, qf, zero)],
        axis=0).astype(BF16)
    return lax.dot_general(q_bd, kb, (((1,), (1,)), ((), ())), preferred_element_type=F32)


def _pair_softmax(s, bias, lim):
    if bias is not None:
        s = s + bias
    if lim is not None:
        col = lax.broadcasted_iota(jnp.int32, s.shape, 1)
        s = jnp.where(col >= lim, s, NEG)
    m = jnp.max(s, axis=-1, keepdims=True)
    e = jnp.exp2(s - m)
    return e.astype(BF16), 1.0 / jnp.sum(e, axis=-1, keepdims=True)


def _pair_output(e, inv_l, vb):
    nq = e.shape[0] // HEADS_PER_VREG
    lane = lax.broadcasted_iota(jnp.int32, (nq, LANES), 1)
    pv = jnp.dot(e, vb, preferred_element_type=F32) * inv_l
    return jnp.where(lane < HEAD_DIM, pv[:nq], pv[nq:])


def _pipelined_pairs(n, scores, softmax, output, lead_scores=1, lead_softmax=1, done=None):
    s, e = {}, {}
    for step in range(n + lead_scores + lead_softmax):
        i_sm = step - lead_scores
        i_out = i_sm - lead_softmax
        if step < n:
            s[step] = scores(step)
        if 0 <= i_sm < n:
            e[i_sm] = softmax(i_sm, s.pop(i_sm))
        if i_out >= 0:
            output(i_out, *e.pop(i_out))
            if done is not None:
                done(i_out)


def _band_bias_init(u_ref, bias_scr, edge_scr):
    n_var = GROUP_KEYS - VAR_COL0
    ulen = u_ref.shape[1]
    row = lax.broadcasted_iota(jnp.int32, (GROUP_ROWS, n_var), 0)
    col = lax.broadcasted_iota(jnp.int32, (GROUP_ROWS, n_var), 1) + VAR_COL0
    past_band = col >= row // CHUNK * CHUNK + BAND
    for h in range(u_ref.shape[0]):
        per_diag = jnp.broadcast_to(u_ref[h:h + 1, :], (GROUP_ROWS, ulen))
        skew = pltpu.roll(per_diag, ulen - (GROUP_ROWS - 1), 1, stride=1, stride_axis=0)
        p, half = divmod(h, HEADS_PER_VREG)
        bias_scr[p, half * GROUP_ROWS:(half + 1) * GROUP_ROWS, :] = jnp.where(
            past_band, NEG, skew[:, :n_var])
    row = lax.broadcasted_iota(jnp.int32, edge_scr.shape, 0) % GROUP_ROWS
    col = lax.broadcasted_iota(jnp.int32, edge_scr.shape, 1)
    edge_scr[...] = jnp.where(col < row // CHUNK * CHUNK, NEG, 0.0)


def _attn_prompt_kernel(q_ref, k_ref, v_ref, x_ref, wo_ref, u_ref, y_ref,
                        kbuf, vbuf, obuf, bias_scr, edge_scr, *, tq):
    t = pl.program_id(1)
    n_pairs = q_ref.shape[1] // LANES

    @pl.when((pl.program_id(0) == 0) & (t == 0))
    def _():
        _band_bias_init(u_ref, bias_scr, edge_scr)

    @pl.when(t == 0)
    def _():
        kbuf[0:tq, :] = jnp.zeros((tq, kbuf.shape[1]), BF16)
        vbuf[0:tq, :] = jnp.zeros((tq, vbuf.shape[1]), BF16)

    @pl.when(t > 0)
    def _():
        kbuf[0:tq, :] = kbuf[tq:2 * tq, :]
        vbuf[0:tq, :] = vbuf[tq:2 * tq, :]

    kbuf[tq:2 * tq, :] = k_ref[...]
    vbuf[tq:2 * tq, :] = v_ref[...]

    def chunk_loop(masked):
        span = ATTN_GROUPS_PER_TRIP * GROUP_ROWS

        def trip_body(i, carry):
            base = pl.multiple_of(i * span, span)

            def item(n):
                g, p = divmod(n, n_pairs)
                return base + g * GROUP_ROWS, slice(p * LANES, (p + 1) * LANES), p

            def scores(n):
                r0, cols, _ = item(n)
                return _pair_scores(q_ref[pl.ds(r0, GROUP_ROWS), cols],
                                    kbuf[pl.ds(r0, GROUP_KEYS), cols])

            def softmax(n, s):
                r0, _, p = item(n)
                s = jnp.concatenate(
                    [s[:, :LANES] + edge_scr[...], s[:, LANES:VAR_COL0],
                     s[:, VAR_COL0:] + bias_scr[p]], axis=1)
                lim = (REACH - (t * tq + r0)) if masked else None
                return _pair_softmax(s, None, lim)

            def output(n, e, inv_l):
                r0, cols, _ = item(n)
                o = _pair_output(e, inv_l, vbuf[pl.ds(r0, GROUP_KEYS), cols])
                obuf[pl.ds(r0, GROUP_ROWS), cols] = o.astype(BF16)

            def done(n):
                if (n + 1) % (PROJ_GROUPS * n_pairs) == 0:
                    rows = PROJ_GROUPS * GROUP_ROWS
                    r0 = pl.multiple_of(base + (n + 1) // n_pairs * GROUP_ROWS - rows, rows)
                    y_ref[pl.ds(r0, rows), :] = x_ref[pl.ds(r0, rows), :] + jnp.dot(
                        obuf[pl.ds(r0, rows), :], wo_ref[...].astype(BF16),
                        preferred_element_type=F32)

            _pipelined_pairs(ATTN_GROUPS_PER_TRIP * n_pairs, scores, softmax, output, 1, 2, done)
            return carry

        lax.fori_loop(0, tq // span, trip_body, 0)

    @pl.when(t == 0)
    def _():
        chunk_loop(True)

    @pl.when(t > 0)
    def _():
        chunk_loop(False)


def _attn_prompt(q, k, v, x, wo, table):
    b, s, d = x.shape
    tq = REACH
    assert s % tq == 0
    h = table.shape[0]
    n_diag = GROUP_ROWS - 1 + GROUP_KEYS - VAR_COL0
    diag = jnp.arange(n_diag) + (VAR_COL0 - GROUP_ROWS + 1)
    idx = jnp.clip(REACH - diag, -MAX_REL, MAX_REL) + MAX_REL
    u = (table[:, idx] - table[:, 2 * MAX_REL:]).astype(F32) * LOG2_E
    u = jnp.pad(u, ((0, 0), (0, pl.next_power_of_2(n_diag) - n_diag)))
    row = pl.BlockSpec((None, tq, d), lambda i, j: (i, j, 0))
    rows2 = HEADS_PER_VREG * GROUP_ROWS
    return pl.pallas_call(
        functools.partial(_attn_prompt_kernel, tq=tq),
        grid=(b, s // tq),
        in_specs=[row, row, row, row, _const_spec((d, d)), _const_spec(u.shape)],
        out_specs=row,
        out_shape=jax.ShapeDtypeStruct((b, s, d), F32),
        scratch_shapes=[pltpu.VMEM((2 * tq, d), BF16), pltpu.VMEM((2 * tq, d), BF16),
                        pltpu.VMEM((tq, d), BF16),
                        pltpu.VMEM((h // HEADS_PER_VREG, rows2, GROUP_KEYS - VAR_COL0), F32),
                        pltpu.VMEM((rows2, LANES), F32)],
        compiler_params=_params(2),
        name="attn_prompt",
    )(q, k, v, x, wo, u)


def _attn_sample_kernel(q_ref, k_ref, v_ref, ck_ref, cv_ref, x_ref, wo_ref, bias_ref, y_ref,
                        obuf):
    n_pairs = q_ref.shape[1] // LANES
    k_all = jnp.concatenate([ck_ref[...].astype(BF16), k_ref[...]], axis=0)
    v_all = jnp.concatenate([cv_ref[...].astype(BF16), v_ref[...]], axis=0)
    cols = [slice(p * LANES, (p + 1) * LANES) for p in range(n_pairs)]

    def scores(p):
        return _pair_scores(q_ref[:, cols[p]], k_all[:, cols[p]])

    def softmax(p, s):
        return _pair_softmax(s, bias_ref[p], None)

    def output(p, e, inv_l):
        obuf[:, cols[p]] = _pair_output(e, inv_l, v_all[:, cols[p]]).astype(BF16)

    _pipelined_pairs(n_pairs, scores, softmax, output)
    y_ref[...] = x_ref[...] + jnp.dot(obuf[...], wo_ref[...].astype(BF16),
                                      preferred_element_type=F32)


def _attn_sample(q, k, v, cache_k, cache_v, x, wo_bf, bias_pairs):
    b, tn, d = x.shape
    ca = cache_k.shape[1]
    new = pl.BlockSpec((None, tn, d), lambda i: (i, 0, 0))
    old = pl.BlockSpec((None, ca, d), lambda i: (i, 0, 0))
    return pl.pallas_call(
        _attn_sample_kernel,
        grid=(b,),
        in_specs=[new, new, new, old, old, new, _const_spec((d, d)),
                  _const_spec(bias_pairs.shape)],
        out_specs=new,
        out_shape=jax.ShapeDtypeStruct((b, tn, d), F32),
        scratch_shapes=[pltpu.VMEM((tn, d), BF16)],
        compiler_params=_params(1),
        name="attn_sample",
    )(q, k, v, cache_k, cache_v, x, wo_bf, bias_pairs)


def _gmlp_kernel(x_ref, g_ref, win_ref, vg_ref, ws_ref, bs_ref, wout_ref, *rest,
                 blk, emit_v):
    if emit_v:
        y_ref, vout_ref, gated = rest
    else:
        y_ref, gated = rest
    rows, d = x_ref.shape
    n_pairs = d // LANES
    x = x_ref[...]
    h = _rms(x, g_ref[...]).astype(BF16)
    def gelu(z):
        return 0.5 * z * (1.0 + lax.erf(z * np.float32(np.sqrt(0.5))))

    v = _rms(gelu(jnp.dot(h, win_ref[:, d:].astype(BF16), preferred_element_type=F32)),
             vg_ref[...])
    u = gelu(jnp.dot(h, win_ref[:, :d].astype(BF16), preferred_element_type=F32))
    if emit_v:
        vout_ref[...] = v
    n_blk = rows // blk
    lane = lax.broadcasted_iota(jnp.int32, (blk, n_blk * LANES), 1) & (LANES - 1)
    zero = jnp.zeros((blk, n_blk * LANES), F32)
    for p in range(n_pairs):
        cols = slice(p * LANES, (p + 1) * LANES)
        vcat = jnp.concatenate([v[r * blk:(r + 1) * blk, cols] for r in range(n_blk)], axis=1)
        v_stack = jnp.concatenate(
            [jnp.where(lane < HEAD_DIM, vcat, zero), jnp.where(lane >= HEAD_DIM, vcat, zero)],
            axis=0).astype(BF16)
        m = jnp.dot(ws_ref[p], v_stack, preferred_element_type=F32)
        for r in range(n_blk):
            rs = slice(r * blk, (r + 1) * blk)
            mixed = m[:, r * LANES:(r + 1) * LANES] + bs_ref[:, cols]
            gated[rs, cols] = (u[rs, cols] * mixed).astype(BF16)
    y_ref[...] = x + jnp.dot(gated[...], wout_ref[...].astype(BF16), preferred_element_type=F32)


def _gmlp(x, gain, win_bf, v_gain, ws_pairs, bs_rows, wout_bf, rows, blk, emit_v):
    b, s, d = x.shape
    w = win_bf.shape[1] - d
    assert s % rows == 0 and rows % blk == 0 and w == d
    row = pl.BlockSpec((None, rows, d), lambda i, j: (i, j, 0))
    out_specs = [row]
    out_shape = [jax.ShapeDtypeStruct((b, s, d), F32)]
    if emit_v:
        out_specs.append(row)
        out_shape.append(jax.ShapeDtypeStruct((b, s, w), F32))
    return pl.pallas_call(
        functools.partial(_gmlp_kernel, blk=blk, emit_v=emit_v),
        grid=(b, s // rows),
        in_specs=[row, _const_spec((1, d)), _const_spec(win_bf.shape), _const_spec((1, w)),
                  _const_spec(ws_pairs.shape), _const_spec(bs_rows.shape),
                  _const_spec(wout_bf.shape)],
        out_specs=out_specs,
        out_shape=out_shape,
        scratch_shapes=[pltpu.VMEM((rows, w), BF16)],
        compiler_params=_params(2),
        name="gmlp",
    )(x, gain.reshape(1, d), win_bf, v_gain.reshape(1, w), ws_pairs, bs_rows, wout_bf)


def _conv_taps(stage, n, w, cb):
    c = cb + stage[SUBLANES:SUBLANES + n, :] * w[2:3]
    c = c + stage[SUBLANES - 1:SUBLANES - 1 + n, :] * w[1:2]
    return c + stage[SUBLANES - 2:SUBLANES - 2 + n, :] * w[0:1]


def _ffn_kernel(x_ref, g_ref, wg_ref, wv_ref, cw_ref, cb_ref, wdn_ref, gf_ref,
                y_ref, carry_ref, h_scr, a_scr, act_scr, *, final_norm):
    t = pl.program_id(1)
    rows = x_ref.shape[0]
    ff = wdn_ref.shape[0]
    fc = FF_CHUNK

    @pl.when(t == 0)
    def _():
        carry_ref[...] = jnp.zeros_like(carry_ref)

    x = x_ref[...]
    h_scr[...] = _rms(x, g_ref[...]).astype(BF16)

    for j in range(ff // fc):
        chunk = slice(j * fc, (j + 1) * fc)
        halves = []
        for half, w_ref in enumerate((wg_ref, wv_ref)):
            cols = slice(half * ff + j * fc, half * ff + (j + 1) * fc)
            stage = a_scr.at[2 * (j % 2) + half]
            a = jnp.dot(h_scr[...], w_ref[:, chunk], preferred_element_type=F32)
            stage[0:SUBLANES, :] = carry_ref[:, cols]
            stage[SUBLANES:SUBLANES + rows, :] = a
            carry_ref[:, cols] = a[rows - SUBLANES:rows]
            halves.append(_conv_taps(stage, rows, cw_ref[:, cols], cb_ref[:, cols]))
        act_scr[:, chunk] = (jax.nn.silu(halves[0]) * halves[1]).astype(BF16)

    y = x + jnp.dot(act_scr[...], wdn_ref[...], preferred_element_type=F32)
    if final_norm:
        y = _rms(y, gf_ref[...])
    y_ref[...] = y


def _ffn_stream_kernel(x_ref, g_ref, hg_ref, hv_ref, wg_ref, wv_ref, cwg_ref, cwv_ref, cbg_ref,
                       cbv_ref, wdn_ref, gf_ref, y_ref, cg_ref, cv_ref, wg_out, wv_out, wdn_out,
                       h_scr, acc_scr, a_scr, *, n_sub, final_norm):
    j = pl.program_id(0)
    rows = x_ref.shape[0]
    seq = rows // n_sub

    @pl.when(j == 0)
    def _():
        h_scr[...] = _rms(x_ref[...], g_ref[...]).astype(BF16)
        acc_scr[...] = jnp.zeros_like(acc_scr)

    def branch(slot, w_ref, w_out, hist_ref, carry_out, cw_ref, cb_ref):
        w_bf = w_ref[...].astype(BF16)
        w_out[...] = w_bf
        a = jnp.dot(h_scr[...], w_bf, preferred_element_type=F32)
        outs = []
        for s in range(n_sub):
            stage = a_scr.at[slot, s]
            stage[0:SUBLANES, :] = hist_ref[s]
            stage[SUBLANES:SUBLANES + seq, :] = a[s * seq:(s + 1) * seq]
            carry_out[s] = a[(s + 1) * seq - SUBLANES:(s + 1) * seq]
            outs.append(_conv_taps(stage, seq, cw_ref[...], cb_ref[...]))
        return jnp.concatenate(outs, axis=0)

    gate = branch(0, wg_ref, wg_out, hg_ref, cg_ref, cwg_ref, cbg_ref)
    val = branch(1, wv_ref, wv_out, hv_ref, cv_ref, cwv_ref, cbv_ref)
    wd_bf = wdn_ref[...].astype(BF16)
    wdn_out[...] = wd_bf
    acc_scr[...] += jnp.dot((jax.nn.silu(gate) * val).astype(BF16), wd_bf,
                            preferred_element_type=F32)

    @pl.when(j == pl.num_programs(0) - 1)
    def _():
        y = x_ref[...] + acc_scr[...]
        if final_norm:
            y = _rms(y, gf_ref[...])
        y_ref[...] = y


def _layer_spec(shape, layer):
    nd = len(shape)
    return pl.BlockSpec((None,) + tuple(shape[1:]), lambda *_: (layer,) + (0,) * (nd - 1),
                        pipeline_mode=pl.Buffered(1))


def _conv_ffn(x, gain, wg_bf, wv_bf, conv_w, conv_b, wdn_bf, layer, final_gain, rows):
    b, s, d = x.shape
    ff = wdn_bf.shape[0]
    assert s % rows == 0 and rows % SUBLANES == 0 and ff % FF_CHUNK == 0
    final_norm = final_gain is not None
    gf = (final_gain if final_norm else jnp.ones((d,), F32)).reshape(1, d)
    row = pl.BlockSpec((None, rows, d), lambda i, j: (i, j, 0))
    carry_spec = pl.BlockSpec((None, SUBLANES, 2 * ff), lambda i, j: (i, 0, 0))
    return pl.pallas_call(
        functools.partial(_ffn_kernel, final_norm=final_norm),
        grid=(b, s // rows),
        in_specs=[row, _const_spec((1, d)), _const_spec(wg_bf.shape), _const_spec(wv_bf.shape),
                  _layer_spec(conv_w.shape, layer), _layer_spec(conv_b.shape, layer),
                  _const_spec(wdn_bf.shape), _const_spec((1, d))],
        out_specs=[row, carry_spec],
        out_shape=[jax.ShapeDtypeStruct((b, s, d), F32),
                   jax.ShapeDtypeStruct((b, SUBLANES, 2 * ff), F32)],
        scratch_shapes=[pltpu.VMEM((rows, d), BF16),
                        pltpu.VMEM((4, rows + SUBLANES, FF_CHUNK), F32),
                        pltpu.VMEM((rows, ff), BF16)],
        compiler_params=_params(2),
        name="conv_ffn",
    )(x, gain.reshape(1, d), wg_bf, wv_bf, conv_w, conv_b, wdn_bf, gf)


def _conv_ffn_stream(x, gain, hist, w_up, conv_w, conv_b, w_down, layer, final_gain, n_sub):
    r, d = x.shape
    ff = w_down.shape[1]
    step_bytes_per_col = 2 * 3 * d * (4 + 2)
    fc = max(ff // k for k in range(1, ff // LANES + 1)
             if ff % k == 0 and (ff // k) % LANES == 0
             and (ff // k) * step_bytes_per_col <= VMEM_LIMIT // 2)
    c = ff // fc
    seq = r // n_sub
    assert ff % fc == 0 and fc % LANES == 0 and seq % SUBLANES == 0
    final_norm = final_gain is not None
    gf = (final_gain if final_norm else jnp.ones((d,), F32)).reshape(1, d)

    def cols(blk_rows, half):
        return pl.BlockSpec((None, blk_rows, fc), lambda j: (layer, 0, half * c + j))

    def hist_cols(half):
        return pl.BlockSpec((n_sub, SUBLANES, fc), lambda j: (0, 0, half * c + j))

    half_cols = pl.BlockSpec((n_sub, SUBLANES, fc), lambda j: (0, 0, j))
    up_cols = pl.BlockSpec((d, fc), lambda j: (0, j))
    dn_rows = pl.BlockSpec((fc, d), lambda j: (j, 0))
    return pl.pallas_call(
        functools.partial(_ffn_stream_kernel, n_sub=n_sub, final_norm=final_norm),
        grid=(c,),
        in_specs=[_const_spec((r, d)), _const_spec((1, d)), hist_cols(0), hist_cols(1),
                  cols(d, 0), cols(d, 1), cols(CONV_W, 0), cols(CONV_W, 1), cols(1, 0), cols(1, 1),
                  pl.BlockSpec((None, fc, d), lambda j: (layer, j, 0)),
                  _const_spec((1, d))],
        out_specs=[pl.BlockSpec((r, d), lambda j: (0, 0)), half_cols, half_cols,
                   up_cols, up_cols, dn_rows],
        out_shape=[jax.ShapeDtypeStruct((r, d), F32),
                   jax.ShapeDtypeStruct((n_sub, SUBLANES, ff), F32),
                   jax.ShapeDtypeStruct((n_sub, SUBLANES, ff), F32),
                   jax.ShapeDtypeStruct((d, ff), BF16),
                   jax.ShapeDtypeStruct((d, ff), BF16),
                   jax.ShapeDtypeStruct((ff, d), BF16)],
        scratch_shapes=[pltpu.VMEM((r, d), BF16), pltpu.VMEM((r, d), F32),
                        pltpu.VMEM((2, n_sub, seq + SUBLANES, fc), F32)],
        compiler_params=_params(1),
        name="conv_ffn_stream",
    )(x, gain.reshape(1, d), hist, hist, w_up, w_up, conv_w, conv_w, conv_b, conv_b, w_down, gf)


def _pair_bias(table, n_q, n_k, reach):
    diag = jnp.arange(n_q + n_k - 1) - (n_q - 1)
    idx = jnp.clip(reach - diag, -MAX_REL, MAX_REL) + MAX_REL
    r = table[:, idx].astype(F32) * LOG2_E
    h, m = r.shape
    flat = jnp.tile(jnp.pad(r, ((0, 0), (0, 1))), (1, n_q))[:, :n_q * m]
    bias = flat.reshape(h, n_q, m)[:, :, n_q - 1:n_q - 1 + n_k]
    return bias.reshape(h // HEADS_PER_VREG, HEADS_PER_VREG * n_q, n_k)


def _pad_hist(hist):
    pad = [(0, 0)] * (hist.ndim - 2) + [(SUBLANES - hist.shape[-2], 0), (0, 0)]
    return jnp.pad(hist, pad)


def kernel(x_prompt, x_sample, cache_a_k, cache_a_v, state_ffn_conv, ln_mix, ln_ffn, ln_final,
           a_w_qkv, a_rel_bias, a_w_o, b_w_in, b_v_norm, b_w_s, b_bias_s, b_w_out,
           f_w_up, f_conv_w, f_conv_b, f_w_down):
    bp, sp, d = x_prompt.shape
    bs, ts, _ = x_sample.shape
    depth = ln_mix.shape[0]
    ca = cache_a_k.shape[2]
    keep = min(REACH, sp)
    two_ff = f_w_up.shape[2]
    hist_rows = CONV_W - 1
    conv_b_all = f_conv_b.reshape(depth, 1, two_ff)

    xp = x_prompt
    xs = x_sample.reshape(1, bs * ts, d)
    kp_l, vp_l, ks_l, vs_l, gv_l, cp_l, cs_l = [], [], [], [], [], [], []

    for i in range(depth):
        j = i // 2
        if i % 2 == 0:
            wqkv = a_w_qkv[j]
            wo = a_w_o[j]
            q, k, v, kt, vt = _qkv_proj(xp, ln_mix[i], wqkv, keep, ROW_TILE)
            xp = _attn_prompt(q, k, v, xp, wo, a_rel_bias[j])
            kp_l.append(kt.reshape(bp, keep, N_HEADS, HEAD_DIM))
            vp_l.append(vt.reshape(bp, keep, N_HEADS, HEAD_DIM))
            q, k, v, kt, vt = _qkv_proj(xs, ln_mix[i], wqkv, bs * ts, bs * ts)
            xs = _attn_sample(
                q.reshape(bs, ts, d), k.reshape(bs, ts, d), v.reshape(bs, ts, d),
                cache_a_k[j].reshape(bs, ca, d), cache_a_v[j].reshape(bs, ca, d),
                xs.reshape(bs, ts, d), wo, _pair_bias(a_rel_bias[j], ts, ca + ts, ca),
            ).reshape(1, bs * ts, d)
            ks_l.append(kt.reshape(bs, ts, N_HEADS, HEAD_DIM))
            vs_l.append(vt.reshape(bs, ts, N_HEADS, HEAD_DIM))
        else:
            win = b_w_in[j]
            wout = b_w_out[j]
            pos = jnp.arange(MLP_BLOCK)
            causal = (pos[None, :] // CHUNK) <= (pos[:, None] // CHUNK)
            ws = jnp.where(causal, b_w_s[j], 0.0).astype(BF16)
            g = ws.shape[0]

            def pair_rows(w):
                w = w.reshape(g // HEADS_PER_VREG, HEADS_PER_VREG, w.shape[1], w.shape[2])
                return jnp.transpose(w, (0, 2, 1, 3)).reshape(
                    g // HEADS_PER_VREG, w.shape[2], HEADS_PER_VREG * w.shape[3])

            def bias_rows(bias):
                return jnp.repeat(bias.T.astype(F32), d // g, axis=1)

            xp = _gmlp(xp, ln_mix[i], win, b_v_norm[j], pair_rows(ws), bias_rows(b_bias_s[j]),
                       wout, ROW_TILE, MLP_BLOCK, False)[0]
            xs, gv = _gmlp(xs, ln_mix[i], win, b_v_norm[j], pair_rows(ws[:, :ts, :ts]),
                           bias_rows(b_bias_s[j][:, :ts]), wout, bs * ts, ts, True)
            gv_l.append(gv.reshape(bs, ts, d))

        fin = ln_final if i == depth - 1 else None
        xs2, cs_g, cs_v, wg, wv, wdn = _conv_ffn_stream(
            xs[0], ln_ffn[i], _pad_hist(state_ffn_conv[i]), f_w_up, f_conv_w, conv_b_all,
            f_w_down, i, fin, bs)
        xs = xs2[None]
        xp, cp = _conv_ffn(xp, ln_ffn[i], wg, wv, f_conv_w, conv_b_all, wdn, i, fin,
                           FFN_ROW_TILE)
        cp_l.append(cp[:, SUBLANES - hist_rows:, :])
        cs_l.append(jnp.concatenate([cs_g, cs_v], axis=-1)[:, SUBLANES - hist_rows:, :])

    return (xp, xs.reshape(bs, ts, d), jnp.stack(kp_l), jnp.stack(vp_l), jnp.stack(ks_l),
            jnp.stack(vs_l), jnp.stack(gv_l), jnp.stack(cp_l), jnp.stack(cs_l))
```

```python
import functools

import jax
import jax.numpy as jnp
import numpy as np
from jax import lax
from jax.experimental import pallas as pl
from jax.experimental.pallas import tpu as pltpu

F32 = jnp.float32
BF16 = jnp.bfloat16

CHUNK = 64
PAST_CHUNKS = 8
REACH = PAST_CHUNKS * CHUNK
BAND = REACH + CHUNK
MAX_REL = 128
N_HEADS = 16
HEAD_DIM = 64
MLP_BLOCK = 128
CONV_W = 3
EPS = 1e-6
NEG = -1e30
LOG2_E = float(np.log2(np.e))
Q_SCALE = HEAD_DIM ** -0.5 * LOG2_E

LANES = 128
SUBLANES = 8
MXU_DIM = 256
HEADS_PER_VREG = LANES // HEAD_DIM
VMEM_LIMIT = 56 * 1024 * 1024

ROW_TILE = 1024
FFN_ROW_TILE = 512
FF_CHUNK = MXU_DIM
GROUP_CHUNKS = LANES // CHUNK
GROUP_ROWS = GROUP_CHUNKS * CHUNK
GROUP_KEYS = REACH + GROUP_ROWS
VAR_COL0 = (REACH - MAX_REL) // LANES * LANES
ATTN_GROUPS_PER_TRIP = 4
PROJ_GROUPS = 2
ATTN_LEAD_SCORES = 1
ATTN_LEAD_SOFTMAX = 3


def _rms(x, g):
    return x * lax.rsqrt(jnp.mean(x * x, axis=-1, keepdims=True) + EPS) * g


def _const_spec(shape):
    nd = len(shape)
    return pl.BlockSpec(shape, lambda *_: (0,) * nd, pipeline_mode=pl.Buffered(1))


def _params(n_grid):
    return pltpu.CompilerParams(
        dimension_semantics=("arbitrary",) * n_grid, vmem_limit_bytes=VMEM_LIMIT)


def _qkv_kernel(x_ref, g_ref, w_ref, q_ref, k_ref, v_ref, kt_ref, vt_ref, *, d, tail_tile0):
    t = pl.program_id(1)
    h = _rms(x_ref[...], g_ref[...]).astype(BF16)
    q = jnp.dot(h, w_ref[:, 0:d].astype(BF16), preferred_element_type=F32)
    q_ref[...] = (q * Q_SCALE).astype(BF16)
    k = jnp.dot(h, w_ref[:, d:2 * d].astype(BF16), preferred_element_type=F32)
    k_ref[...] = k.astype(BF16)
    v = jnp.dot(h, w_ref[:, 2 * d:3 * d].astype(BF16), preferred_element_type=F32)
    v_ref[...] = v.astype(BF16)

    @pl.when(t >= tail_tile0)
    def _():
        tail_rows = kt_ref.shape[0]
        kt_ref[...] = k[k.shape[0] - tail_rows:]
        vt_ref[...] = v[v.shape[0] - tail_rows:]


def _qkv_proj(x, gain, w_qkv, keep, tm):
    b, s, d = x.shape
    nt = s // tm
    tail_rows = min(tm, keep)
    assert s % tm == 0 and keep % tail_rows == 0
    tail_tile0 = nt - keep // tail_rows
    row = pl.BlockSpec((None, tm, d), lambda i, j: (i, j, 0))
    tail = pl.BlockSpec((None, tail_rows, d),
                        lambda i, j: (i, jnp.maximum(j - tail_tile0, 0), 0))
    return pl.pallas_call(
        functools.partial(_qkv_kernel, d=d, tail_tile0=tail_tile0),
        grid=(b, nt),
        in_specs=[row, _const_spec((1, d)), _const_spec((d, 3 * d))],
        out_specs=[row, row, row, tail, tail],
        out_shape=[jax.ShapeDtypeStruct((b, s, d), BF16)] * 3
        + [jax.ShapeDtypeStruct((b, keep, d), F32)] * 2,
        compiler_params=_params(2),
        name="qkv_proj",
    )(x, gain.reshape(1, d), w_qkv)


def _pair_scores(qp, kb):
    lane = lax.broadcasted_iota(jnp.int32, qp.shape, 1)
    qf = qp.astype(F32)
    zero = jnp.zeros_like(qf)
    q_bd = jnp.concatenate(
        [jnp.where(lane < HEAD_DIM, qf, zero), jnp.where(lane >= HEAD_DIM, qf, zero)],
        axis=0).astype(BF16)
    return lax.dot_general(q_bd, kb, (((1,), (1,)), ((), ())), preferred_element_type=F32)


def _pair_softmax(s, bias, lim):
    if bias is not None:
        s = s + bias
    if lim is not None:
        col = lax.broadcasted_iota(jnp.int32, s.shape, 1)
        s = jnp.where(col >= lim, s, NEG)
    m = jnp.max(s, axis=-1, keepdims=True)
    e = jnp.exp2(s - m)
    return e.astype(BF16), 1.0 / jnp.sum(e, axis=-1, keepdims=True)


def _pair_output(e, inv_l, vb):
    nq = e.shape[0] // HEADS_PER_VREG
    lane = lax.broadcasted_iota(jnp.int32, (nq, LANES), 1)
    pv = jnp.dot(e, vb, preferred_element_type=F32) * inv_l
    return jnp.where(lane < HEAD_DIM, pv[:nq], pv[nq:])


def _pipelined_pairs(n, scores, softmax, output, lead_scores=1, lead_softmax=1, done=None):
    s, e = {}, {}
    for step in range(n + lead_scores + lead_softmax):
        i_sm = step - lead_scores
        i_out = i_sm - lead_softmax
        if step < n:
            s[step] = scores(step)
        if 0 <= i_sm < n:
            e[i_sm] = softmax(i_sm, s.pop(i_sm))
        if i_out >= 0:
            output(i_out, *e.pop(i_out))
            if done is not None:
                done(i_out)


def _band_bias_init(u_ref, bias_scr, edge_scr):
    n_var = GROUP_KEYS - VAR_COL0
    ulen = u_ref.shape[1]
    row = lax.broadcasted_iota(jnp.int32, (GROUP_ROWS, n_var), 0)
    col = lax.broadcasted_iota(jnp.int32, (GROUP_ROWS, n_var), 1) + VAR_COL0
    past_band = col >= row // CHUNK * CHUNK + BAND
    for h in range(u_ref.shape[0]):
        per_diag = jnp.broadcast_to(u_ref[h:h + 1, :], (GROUP_ROWS, ulen))
        skew = pltpu.roll(per_diag, ulen - (GROUP_ROWS - 1), 1, stride=1, stride_axis=0)
        p, half = divmod(h, HEADS_PER_VREG)
        bias_scr[p, half * GROUP_ROWS:(half + 1) * GROUP_ROWS, :] = jnp.where(
            past_band, NEG, skew[:, :n_var])
    row = lax.broadcasted_iota(jnp.int32, edge_scr.shape, 0) % GROUP_ROWS
    col = lax.broadcasted_iota(jnp.int32, edge_scr.shape, 1)
    edge_scr[...] = jnp.where(col < row // CHUNK * CHUNK, NEG, 0.0)


def _attn_prompt_kernel(q_ref, k_ref, v_ref, x_ref, wo_ref, u_ref, y_ref,
                        kbuf, vbuf, obuf, bias_scr, edge_scr, *, tq):
    t = pl.program_id(1)
    n_pairs = q_ref.shape[1] // LANES

    @pl.when((pl.program_id(0) == 0) & (t == 0))
    def _():
        _band_bias_init(u_ref, bias_scr, edge_scr)

    @pl.when(t == 0)
    def _():
        kbuf[0:tq, :] = jnp.zeros((tq, kbuf.shape[1]), BF16)
        vbuf[0:tq, :] = jnp.zeros((tq, vbuf.shape[1]), BF16)

    @pl.when(t > 0)
    def _():
        kbuf[0:tq, :] = kbuf[tq:2 * tq, :]
        vbuf[0:tq, :] = vbuf[tq:2 * tq, :]

    kbuf[tq:2 * tq, :] = k_ref[...]
    vbuf[tq:2 * tq, :] = v_ref[...]

    def chunk_loop(masked):
        span = ATTN_GROUPS_PER_TRIP * GROUP_ROWS

        def trip_body(i, carry):
            base = pl.multiple_of(i * span, span)

            def item(n):
                g, p = divmod(n, n_pairs)
                return base + g * GROUP_ROWS, slice(p * LANES, (p + 1) * LANES), p

            def scores(n):
                r0, cols, _ = item(n)
                return _pair_scores(q_ref[pl.ds(r0, GROUP_ROWS), cols],
                                    kbuf[pl.ds(r0, GROUP_KEYS), cols])

            def softmax(n, s):
                r0, _, p = item(n)
                s = jnp.concatenate(
                    [s[:, :LANES] + edge_scr[...], s[:, LANES:VAR_COL0],
                     s[:, VAR_COL0:] + bias_scr[p]], axis=1)
                lim = (REACH - (t * tq + r0)) if masked else None
                return _pair_softmax(s, None, lim)

            def output(n, e, inv_l):
                r0, cols, _ = item(n)
                o = _pair_output(e, inv_l, vbuf[pl.ds(r0, GROUP_KEYS), cols])
                obuf[pl.ds(r0, GROUP_ROWS), cols] = o.astype(BF16)

            def done(n):
                if (n + 1) % (PROJ_GROUPS * n_pairs) == 0:
                    rows = PROJ_GROUPS * GROUP_ROWS
                    r0 = pl.multiple_of(base + (n + 1) // n_pairs * GROUP_ROWS - rows, rows)
                    y_ref[pl.ds(r0, rows), :] = x_ref[pl.ds(r0, rows), :] + jnp.dot(
                        obuf[pl.ds(r0, rows), :], wo_ref[...].astype(BF16),
                        preferred_element_type=F32)

            _pipelined_pairs(ATTN_GROUPS_PER_TRIP * n_pairs, scores, softmax, output,
                             ATTN_LEAD_SCORES, ATTN_LEAD_SOFTMAX, done)
            return carry

        lax.fori_loop(0, tq // span, trip_body, 0)

    @pl.when(t == 0)
    def _():
        chunk_loop(True)

    @pl.when(t > 0)
    def _():
        chunk_loop(False)


def _attn_prompt(q, k, v, x, wo, table):
    b, s, d = x.shape
    tq = REACH
    assert s % tq == 0 and tq % (ATTN_GROUPS_PER_TRIP * GROUP_ROWS) == 0
    assert ATTN_GROUPS_PER_TRIP % PROJ_GROUPS == 0
    h = table.shape[0]
    n_diag = GROUP_ROWS - 1 + GROUP_KEYS - VAR_COL0
    diag = jnp.arange(n_diag) + (VAR_COL0 - GROUP_ROWS + 1)
    idx = jnp.clip(REACH - diag, -MAX_REL, MAX_REL) + MAX_REL
    u = (table[:, idx] - table[:, 2 * MAX_REL:]).astype(F32) * LOG2_E
    u = jnp.pad(u, ((0, 0), (0, pl.next_power_of_2(n_diag) - n_diag)))
    row = pl.BlockSpec((None, tq, d), lambda i, j: (i, j, 0))
    rows2 = HEADS_PER_VREG * GROUP_ROWS
    return pl.pallas_call(
        functools.partial(_attn_prompt_kernel, tq=tq),
        grid=(b, s // tq),
        in_specs=[row, row, row, row, _const_spec((d, d)), _const_spec(u.shape)],
        out_specs=row,
        out_shape=jax.ShapeDtypeStruct((b, s, d), F32),
        scratch_shapes=[pltpu.VMEM((2 * tq, d), BF16), pltpu.VMEM((2 * tq, d), BF16),
                        pltpu.VMEM((tq, d), BF16),
                        pltpu.VMEM((h // HEADS_PER_VREG, rows2, GROUP_KEYS - VAR_COL0), F32),
                        pltpu.VMEM((rows2, LANES), F32)],
        compiler_params=_params(2),
        name="attn_prompt",
    )(q, k, v, x, wo, u)


def _attn_sample_kernel(q_ref, k_ref, v_ref, ck_ref, cv_ref, x_ref, wo_ref, bias_ref, y_ref,
                        obuf):
    n_pairs = q_ref.shape[1] // LANES
    k_all = jnp.concatenate([ck_ref[...].astype(BF16), k_ref[...]], axis=0)
    v_all = jnp.concatenate([cv_ref[...].astype(BF16), v_ref[...]], axis=0)
    cols = [slice(p * LANES, (p + 1) * LANES) for p in range(n_pairs)]

    def scores(p):
        return _pair_scores(q_ref[:, cols[p]], k_all[:, cols[p]])

    def softmax(p, s):
        return _pair_softmax(s, bias_ref[p], None)

    def output(p, e, inv_l):
        obuf[:, cols[p]] = _pair_output(e, inv_l, v_all[:, cols[p]]).astype(BF16)

    _pipelined_pairs(n_pairs, scores, softmax, output)
    y_ref[...] = x_ref[...] + jnp.dot(obuf[...], wo_ref[...].astype(BF16),
                                      preferred_element_type=F32)


def _attn_sample(q, k, v, cache_k, cache_v, x, wo, bias_pairs):
    b, tn, d = x.shape
    ca = cache_k.shape[1]
    new = pl.BlockSpec((None, tn, d), lambda i: (i, 0, 0))
    old = pl.BlockSpec((None, ca, d), lambda i: (i, 0, 0))
    return pl.pallas_call(
        _attn_sample_kernel,
        grid=(b,),
        in_specs=[new, new, new, old, old, new, _const_spec((d, d)),
                  _const_spec(bias_pairs.shape)],
        out_specs=new,
        out_shape=jax.ShapeDtypeStruct((b, tn, d), F32),
        scratch_shapes=[pltpu.VMEM((tn, d), BF16)],
        compiler_params=_params(1),
        name="attn_sample",
    )(q, k, v, cache_k, cache_v, x, wo, bias_pairs)


def _gmlp_kernel(x_ref, g_ref, win_ref, vg_ref, ws_ref, bs_ref, wout_ref, *rest,
                 blk, emit_v):
    if emit_v:
        y_ref, vout_ref, gated = rest
    else:
        y_ref, gated = rest
    rows, d = x_ref.shape
    n_pairs = d // LANES
    x = x_ref[...]
    h = _rms(x, g_ref[...]).astype(BF16)
    def gelu(z):
        return 0.5 * z * (1.0 + lax.erf(z * np.float32(np.sqrt(0.5))))

    v = _rms(gelu(jnp.dot(h, win_ref[:, d:].astype(BF16), preferred_element_type=F32)),
             vg_ref[...])
    u = gelu(jnp.dot(h, win_ref[:, :d].astype(BF16), preferred_element_type=F32))
    if emit_v:
        vout_ref[...] = v
    n_blk = rows // blk
    lane = lax.broadcasted_iota(jnp.int32, (blk, n_blk * LANES), 1) & (LANES - 1)
    zero = jnp.zeros((blk, n_blk * LANES), F32)
    for p in range(n_pairs):
        cols = slice(p * LANES, (p + 1) * LANES)
        vcat = jnp.concatenate([v[r * blk:(r + 1) * blk, cols] for r in range(n_blk)], axis=1)
        v_stack = jnp.concatenate(
            [jnp.where(lane < HEAD_DIM, vcat, zero), jnp.where(lane >= HEAD_DIM, vcat, zero)],
            axis=0).astype(BF16)
        m = jnp.dot(ws_ref[p], v_stack, preferred_element_type=F32)
        for r in range(n_blk):
            rs = slice(r * blk, (r + 1) * blk)
            mixed = m[:, r * LANES:(r + 1) * LANES] + bs_ref[:, cols]
            gated[rs, cols] = (u[rs, cols] * mixed).astype(BF16)
    y_ref[...] = x + jnp.dot(gated[...], wout_ref[...].astype(BF16), preferred_element_type=F32)


def _gmlp(x, gain, w_in, v_gain, ws_pairs, bs_rows, w_out, rows, blk, emit_v):
    b, s, d = x.shape
    w = w_in.shape[1] - d
    assert s % rows == 0 and rows % blk == 0 and w == d
    row = pl.BlockSpec((None, rows, d), lambda i, j: (i, j, 0))
    out_specs = [row]
    out_shape = [jax.ShapeDtypeStruct((b, s, d), F32)]
    if emit_v:
        out_specs.append(row)
        out_shape.append(jax.ShapeDtypeStruct((b, s, w), F32))
    return pl.pallas_call(
        functools.partial(_gmlp_kernel, blk=blk, emit_v=emit_v),
        grid=(b, s // rows),
        in_specs=[row, _const_spec((1, d)), _const_spec(w_in.shape), _const_spec((1, w)),
                  _const_spec(ws_pairs.shape), _const_spec(bs_rows.shape),
                  _const_spec(w_out.shape)],
        out_specs=out_specs,
        out_shape=out_shape,
        scratch_shapes=[pltpu.VMEM((rows, w), BF16)],
        compiler_params=_params(2),
        name="gmlp",
    )(x, gain.reshape(1, d), w_in, v_gain.reshape(1, w), ws_pairs, bs_rows, w_out)


def _conv_taps(stage, n, w, cb):
    c = cb + stage[SUBLANES:SUBLANES + n, :] * w[2:3]
    c = c + stage[SUBLANES - 1:SUBLANES - 1 + n, :] * w[1:2]
    return c + stage[SUBLANES - 2:SUBLANES - 2 + n, :] * w[0:1]


def _ffn_kernel(x_ref, g_ref, wg_ref, wv_ref, cw_ref, cb_ref, wdn_ref, gf_ref,
                y_ref, carry_ref, h_scr, a_scr, act_scr, *, final_norm):
    t = pl.program_id(1)
    rows = x_ref.shape[0]
    ff = wdn_ref.shape[0]
    fc = FF_CHUNK

    @pl.when(t == 0)
    def _():
        carry_ref[...] = jnp.zeros_like(carry_ref)

    x = x_ref[...]
    h_scr[...] = _rms(x, g_ref[...]).astype(BF16)

    for j in range(ff // fc):
        chunk = slice(j * fc, (j + 1) * fc)
        halves = []
        for half, w_ref in enumerate((wg_ref, wv_ref)):
            cols = slice(half * ff + j * fc, half * ff + (j + 1) * fc)
            stage = a_scr.at[2 * (j % 2) + half]
            a = jnp.dot(h_scr[...], w_ref[:, chunk], preferred_element_type=F32)
            stage[0:SUBLANES, :] = carry_ref[:, cols]
            stage[SUBLANES:SUBLANES + rows, :] = a
            carry_ref[:, cols] = a[rows - SUBLANES:rows]
            halves.append(_conv_taps(stage, rows, cw_ref[:, cols], cb_ref[:, cols]))
        act_scr[:, chunk] = (jax.nn.silu(halves[0]) * halves[1]).astype(BF16)

    y = x + jnp.dot(act_scr[...], wdn_ref[...], preferred_element_type=F32)
    if final_norm:
        y = _rms(y, gf_ref[...])
    y_ref[...] = y


def _ffn_stream_kernel(x_ref, g_ref, hg_ref, hv_ref, wg_ref, wv_ref, cwg_ref, cwv_ref, cbg_ref,
                       cbv_ref, wdn_ref, gf_ref, y_ref, cg_ref, cv_ref, wg_out, wv_out, wdn_out,
                       h_scr, acc_scr, a_scr, *, n_sub, final_norm):
    j = pl.program_id(0)
    rows = x_ref.shape[0]
    seq = rows // n_sub

    @pl.when(j == 0)
    def _():
        h_scr[...] = _rms(x_ref[...], g_ref[...]).astype(BF16)
        acc_scr[...] = jnp.zeros_like(acc_scr)

    def branch(slot, w_ref, w_out, hist_ref, carry_out, cw_ref, cb_ref):
        w_bf = w_ref[...].astype(BF16)
        w_out[...] = w_bf
        a = jnp.dot(h_scr[...], w_bf, preferred_element_type=F32)
        outs = []
        for s in range(n_sub):
            stage = a_scr.at[slot, s]
            stage[0:SUBLANES, :] = hist_ref[s]
            stage[SUBLANES:SUBLANES + seq, :] = a[s * seq:(s + 1) * seq]
            carry_out[s] = a[(s + 1) * seq - SUBLANES:(s + 1) * seq]
            outs.append(_conv_taps(stage, seq, cw_ref[...], cb_ref[...]))
        return jnp.concatenate(outs, axis=0)

    gate = branch(0, wg_ref, wg_out, hg_ref, cg_ref, cwg_ref, cbg_ref)
    val = branch(1, wv_ref, wv_out, hv_ref, cv_ref, cwv_ref, cbv_ref)
    wd_bf = wdn_ref[...].astype(BF16)
    wdn_out[...] = wd_bf
    acc_scr[...] += jnp.dot((jax.nn.silu(gate) * val).astype(BF16), wd_bf,
                            preferred_element_type=F32)

    @pl.when(j == pl.num_programs(0) - 1)
    def _():
        y = x_ref[...] + acc_scr[...]
        if final_norm:
            y = _rms(y, gf_ref[...])
        y_ref[...] = y


def _layer_spec(shape, layer):
    nd = len(shape)
    return pl.BlockSpec((None,) + tuple(shape[1:]), lambda *_: (layer,) + (0,) * (nd - 1),
                        pipeline_mode=pl.Buffered(1))


def _conv_ffn(x, gain, wg_bf, wv_bf, conv_w, conv_b, wdn_bf, layer, final_gain, rows):
    b, s, d = x.shape
    ff = wdn_bf.shape[0]
    assert s % rows == 0 and rows % SUBLANES == 0 and ff % FF_CHUNK == 0
    final_norm = final_gain is not None
    gf = (final_gain if final_norm else jnp.ones((d,), F32)).reshape(1, d)
    row = pl.BlockSpec((None, rows, d), lambda i, j: (i, j, 0))
    carry_spec = pl.BlockSpec((None, SUBLANES, 2 * ff), lambda i, j: (i, 0, 0))
    return pl.pallas_call(
        functools.partial(_ffn_kernel, final_norm=final_norm),
        grid=(b, s // rows),
        in_specs=[row, _const_spec((1, d)), _const_spec(wg_bf.shape), _const_spec(wv_bf.shape),
                  _layer_spec(conv_w.shape, layer), _layer_spec(conv_b.shape, layer),
                  _const_spec(wdn_bf.shape), _const_spec((1, d))],
        out_specs=[row, carry_spec],
        out_shape=[jax.ShapeDtypeStruct((b, s, d), F32),
                   jax.ShapeDtypeStruct((b, SUBLANES, 2 * ff), F32)],
        scratch_shapes=[pltpu.VMEM((rows, d), BF16),
                        pltpu.VMEM((4, rows + SUBLANES, FF_CHUNK), F32),
                        pltpu.VMEM((rows, ff), BF16)],
        compiler_params=_params(2),
        name="conv_ffn",
    )(x, gain.reshape(1, d), wg_bf, wv_bf, conv_w, conv_b, wdn_bf, gf)


def _conv_ffn_stream(x, gain, hist, w_up, conv_w, conv_b, w_down, layer, final_gain, n_sub):
    r, d = x.shape
    ff = w_down.shape[1]
    step_bytes_per_col = 2 * 3 * d * (4 + 2)
    fc = max(ff // k for k in range(1, ff // LANES + 1)
             if ff % k == 0 and (ff // k) % LANES == 0
             and (ff // k) * step_bytes_per_col <= VMEM_LIMIT // 2)
    c = ff // fc
    seq = r // n_sub
    assert ff % fc == 0 and fc % LANES == 0 and seq % SUBLANES == 0
    final_norm = final_gain is not None
    gf = (final_gain if final_norm else jnp.ones((d,), F32)).reshape(1, d)

    def cols(blk_rows, half):
        return pl.BlockSpec((None, blk_rows, fc), lambda j: (layer, 0, half * c + j))

    def hist_cols(half):
        return pl.BlockSpec((n_sub, SUBLANES, fc), lambda j: (0, 0, half * c + j))

    half_cols = pl.BlockSpec((n_sub, SUBLANES, fc), lambda j: (0, 0, j))
    up_cols = pl.BlockSpec((d, fc), lambda j: (0, j))
    dn_rows = pl.BlockSpec((fc, d), lambda j: (j, 0))
    return pl.pallas_call(
        functools.partial(_ffn_stream_kernel, n_sub=n_sub, final_norm=final_norm),
        grid=(c,),
        in_specs=[_const_spec((r, d)), _const_spec((1, d)), hist_cols(0), hist_cols(1),
                  cols(d, 0), cols(d, 1), cols(CONV_W, 0), cols(CONV_W, 1), cols(1, 0), cols(1, 1),
                  pl.BlockSpec((None, fc, d), lambda j: (layer, j, 0)),
                  _const_spec((1, d))],
        out_specs=[pl.BlockSpec((r, d), lambda j: (0, 0)), half_cols, half_cols,
                   up_cols, up_cols, dn_rows],
        out_shape=[jax.ShapeDtypeStruct((r, d), F32),
                   jax.ShapeDtypeStruct((n_sub, SUBLANES, ff), F32),
                   jax.ShapeDtypeStruct((n_sub, SUBLANES, ff), F32),
                   jax.ShapeDtypeStruct((d, ff), BF16),
                   jax.ShapeDtypeStruct((d, ff), BF16),
                   jax.ShapeDtypeStruct((ff, d), BF16)],
        scratch_shapes=[pltpu.VMEM((r, d), BF16), pltpu.VMEM((r, d), F32),
                        pltpu.VMEM((2, n_sub, seq + SUBLANES, fc), F32)],
        compiler_params=_params(1),
        name="conv_ffn_stream",
    )(x, gain.reshape(1, d), hist, hist, w_up, w_up, conv_w, conv_w, conv_b, conv_b, w_down, gf)


def _pair_bias(table, n_q, n_k, reach):
    diag = jnp.arange(n_q + n_k - 1) - (n_q - 1)
    idx = jnp.clip(reach - diag, -MAX_REL, MAX_REL) + MAX_REL
    r = table[:, idx].astype(F32) * LOG2_E
    h, m = r.shape
    flat = jnp.tile(jnp.pad(r, ((0, 0), (0, 1))), (1, n_q))[:, :n_q * m]
    bias = flat.reshape(h, n_q, m)[:, :, n_q - 1:n_q - 1 + n_k]
    return bias.reshape(h // HEADS_PER_VREG, HEADS_PER_VREG * n_q, n_k)


def _pad_hist(hist):
    pad = [(0, 0)] * (hist.ndim - 2) + [(SUBLANES - hist.shape[-2], 0), (0, 0)]
    return jnp.pad(hist, pad)


def kernel(x_prompt, x_sample, cache_a_k, cache_a_v, state_ffn_conv, ln_mix, ln_ffn, ln_final,
           a_w_qkv, a_rel_bias, a_w_o, b_w_in, b_v_norm, b_w_s, b_bias_s, b_w_out,
           f_w_up, f_conv_w, f_conv_b, f_w_down):
    bp, sp, d = x_prompt.shape
    bs, ts, _ = x_sample.shape
    depth = ln_mix.shape[0]
    ca = cache_a_k.shape[2]
    keep = min(REACH, sp)
    two_ff = f_w_up.shape[2]
    hist_rows = CONV_W - 1
    conv_b_all = f_conv_b.reshape(depth, 1, two_ff)

    xp = x_prompt
    xs = x_sample.reshape(1, bs * ts, d)
    kp_l, vp_l, ks_l, vs_l, gv_l, cp_l, cs_l = [], [], [], [], [], [], []

    for i in range(depth):
        j = i // 2
        if i % 2 == 0:
            wqkv = a_w_qkv[j]
            wo = a_w_o[j]
            q, k, v, kt, vt = _qkv_proj(xp, ln_mix[i], wqkv, keep, ROW_TILE)
            xp = _attn_prompt(q, k, v, xp, wo, a_rel_bias[j])
            kp_l.append(kt.reshape(bp, keep, N_HEADS, HEAD_DIM))
            vp_l.append(vt.reshape(bp, keep, N_HEADS, HEAD_DIM))
            q, k, v, kt, vt = _qkv_proj(xs, ln_mix[i], wqkv, bs * ts, bs * ts)
            xs = _attn_sample(
                q.reshape(bs, ts, d), k.reshape(bs, ts, d), v.reshape(bs, ts, d),
                cache_a_k[j].reshape(bs, ca, d), cache_a_v[j].reshape(bs, ca, d),
                xs.reshape(bs, ts, d), wo, _pair_bias(a_rel_bias[j], ts, ca + ts, ca),
            ).reshape(1, bs * ts, d)
            ks_l.append(kt.reshape(bs, ts, N_HEADS, HEAD_DIM))
            vs_l.append(vt.reshape(bs, ts, N_HEADS, HEAD_DIM))
        else:
            win = b_w_in[j]
            wout = b_w_out[j]
            pos = jnp.arange(MLP_BLOCK)
            causal = (pos[None, :] // CHUNK) <= (pos[:, None] // CHUNK)
            ws = jnp.where(causal, b_w_s[j], 0.0).astype(BF16)
            g = ws.shape[0]

            def pair_rows(w):
                w = w.reshape(g // HEADS_PER_VREG, HEADS_PER_VREG, w.shape[1], w.shape[2])
                return jnp.transpose(w, (0, 2, 1, 3)).reshape(
                    g // HEADS_PER_VREG, w.shape[2], HEADS_PER_VREG * w.shape[3])

            def bias_rows(bias):
                return jnp.repeat(bias.T.astype(F32), d // g, axis=1)

            xp = _gmlp(xp, ln_mix[i], win, b_v_norm[j], pair_rows(ws), bias_rows(b_bias_s[j]),
                       wout, ROW_TILE, MLP_BLOCK, False)[0]
            xs, gv = _gmlp(xs, ln_mix[i], win, b_v_norm[j], pair_rows(ws[:, :ts, :ts]),
                           bias_rows(b_bias_s[j][:, :ts]), wout, bs * ts, ts, True)
            gv_l.append(gv.reshape(bs, ts, d))

        fin = ln_final if i == depth - 1 else None
        xs2, cs_g, cs_v, wg, wv, wdn = _conv_ffn_stream(
            xs[0], ln_ffn[i], _pad_hist(state_ffn_conv[i]), f_w_up, f_conv_w, conv_b_all,
            f_w_down, i, fin, bs)
        xs = xs2[None]
        xp, cp = _conv_ffn(xp, ln_ffn[i], wg, wv, f_conv_w, conv_b_all, wdn, i, fin,
                           FFN_ROW_TILE)
        cp_l.append(cp[:, SUBLANES - hist_rows:, :])
        cs_l.append(jnp.concatenate([cs_g, cs_v], axis=-1)[:, SUBLANES - hist_rows:, :])

    return (xp, xs.reshape(bs, ts, d), jnp.stack(kp_l), jnp.stack(vp_l), jnp.stack(ks_l),
            jnp.stack(vs_l), jnp.stack(gv_l), jnp.stack(cp_l), jnp.stack(cs_l))
```

```python
import functools

import jax
import jax.numpy as jnp
import numpy as np
from jax import lax
from jax.experimental import pallas as pl
from jax.experimental.pallas import tpu as pltpu

F32 = jnp.float32
BF16 = jnp.bfloat16

CHUNK = 64
PAST_CHUNKS = 8
REACH = PAST_CHUNKS * CHUNK
BAND = REACH + CHUNK
MAX_REL = 128
N_HEADS = 16
HEAD_DIM = 64
MLP_BLOCK = 128
CONV_W = 3
EPS = 1e-6
NEG = -1e30
LOG2_E = float(np.log2(np.e))
Q_SCALE = HEAD_DIM ** -0.5 * LOG2_E

LANES = 128
SUBLANES = 8
MXU_DIM = 256
HEADS_PER_VREG = LANES // HEAD_DIM
VMEM_LIMIT = 56 * 1024 * 1024

ROW_TILE = 1024
FFN_ROW_TILE = 512
FF_CHUNK = MXU_DIM
GROUP_CHUNKS = LANES // CHUNK
GROUP_ROWS = GROUP_CHUNKS * CHUNK
GROUP_KEYS = REACH + GROUP_ROWS
VAR_COL0 = (REACH - MAX_REL) // LANES * LANES
ATTN_GROUPS_PER_TRIP = 4
PROJ_GROUPS = 2
ATTN_LEAD_SCORES = 1
ATTN_LEAD_SOFTMAX = 3


def _rms(x, g):
    return x * lax.rsqrt(jnp.mean(x * x, axis=-1, keepdims=True) + EPS) * g


def _const_spec(shape):
    nd = len(shape)
    return pl.BlockSpec(shape, lambda *_: (0,) * nd, pipeline_mode=pl.Buffered(1))


def _params(n_grid):
    return pltpu.CompilerParams(
        dimension_semantics=("arbitrary",) * n_grid, vmem_limit_bytes=VMEM_LIMIT)


def _qkv_kernel(x_ref, g_ref, w_ref, q_ref, k_ref, v_ref, kt_ref, vt_ref, *, d, tail_tile0):
    t = pl.program_id(1)
    h = _rms(x_ref[...], g_ref[...]).astype(BF16)
    q = jnp.dot(h, w_ref[:, 0:d].astype(BF16), preferred_element_type=F32)
    q_ref[...] = (q * Q_SCALE).astype(BF16)
    k = jnp.dot(h, w_ref[:, d:2 * d].astype(BF16), preferred_element_type=F32)
    k_ref[...] = k.astype(BF16)
    v = jnp.dot(h, w_ref[:, 2 * d:3 * d].astype(BF16), preferred_element_type=F32)
    v_ref[...] = v.astype(BF16)

    @pl.when(t >= tail_tile0)
    def _():
        tail_rows = kt_ref.shape[0]
        kt_ref[...] = k[k.shape[0] - tail_rows:]
        vt_ref[...] = v[v.shape[0] - tail_rows:]


def _qkv_proj(x, gain, w_qkv, keep, tm):
    b, s, d = x.shape
    nt = s // tm
    tail_rows = min(tm, keep)
    assert s % tm == 0 and keep % tail_rows == 0
    tail_tile0 = nt - keep // tail_rows
    row = pl.BlockSpec((None, tm, d), lambda i, j: (i, j, 0))
    tail = pl.BlockSpec((None, tail_rows, d),
                        lambda i, j: (i, jnp.maximum(j - tail_tile0, 0), 0))
    return pl.pallas_call(
        functools.partial(_qkv_kernel, d=d, tail_tile0=tail_tile0),
        grid=(b, nt),
        in_specs=[row, _const_spec((1, d)), _const_spec((d, 3 * d))],
        out_specs=[row, row, row, tail, tail],
        out_shape=[jax.ShapeDtypeStruct((b, s, d), BF16)] * 3
        + [jax.ShapeDtypeStruct((b, keep, d), F32)] * 2,
        compiler_params=_params(2),
        name="qkv_proj",
    )(x, gain.reshape(1, d), w_qkv)


def _pair_scores(qp, kb):
    lane = lax.broadcasted_iota(jnp.int32, qp.shape, 1)
    qf = qp.astype(F32)
    zero = jnp.zeros_like(qf)
    q_bd = jnp.concatenate(
        [jnp.where(lane < HEAD_DIM, qf, zero), jnp.where(lane >= HEAD_DIM, qf, zero)],
        axis=0).astype(BF16)
    return lax.dot_general(q_bd, kb, (((1,), (1,)), ((), ())), preferred_element_type=F32)


def _pair_softmax(s, bias, lim):
    if bias is not None:
        s = s + bias
    if lim is not None:
        col = lax.broadcasted_iota(jnp.int32, s.shape, 1)
        s = jnp.where(col >= lim, s, NEG)
    m = jnp.max(s, axis=-1, keepdims=True)
    e = jnp.exp2(s - m)
    return e.astype(BF16), 1.0 / jnp.sum(e, axis=-1, keepdims=True)


def _pair_output(e, inv_l, vb):
    nq = e.shape[0] // HEADS_PER_VREG
    lane = lax.broadcasted_iota(jnp.int32, (nq, LANES), 1)
    pv = jnp.dot(e, vb, preferred_element_type=F32) * inv_l
    return jnp.where(lane < HEAD_DIM, pv[:nq], pv[nq:])


def _pipelined_pairs(n, scores, softmax, output, lead_scores=1, lead_softmax=1, done=None):
    s, e = {}, {}
    for step in range(n + lead_scores + lead_softmax):
        i_sm = step - lead_scores
        i_out = i_sm - lead_softmax
        if step < n:
            s[step] = scores(step)
        if 0 <= i_sm < n:
            e[i_sm] = softmax(i_sm, s.pop(i_sm))
        if i_out >= 0:
            output(i_out, *e.pop(i_out))
            if done is not None:
                done(i_out)


def _band_bias_init(u_ref, bias_scr, edge_scr):
    n_var = GROUP_KEYS - VAR_COL0
    ulen = u_ref.shape[1]
    row = lax.broadcasted_iota(jnp.int32, (GROUP_ROWS, n_var), 0)
    col = lax.broadcasted_iota(jnp.int32, (GROUP_ROWS, n_var), 1) + VAR_COL0
    past_band = col >= row // CHUNK * CHUNK + BAND
    for h in range(u_ref.shape[0]):
        per_diag = jnp.broadcast_to(u_ref[h:h + 1, :], (GROUP_ROWS, ulen))
        skew = pltpu.roll(per_diag, ulen - (GROUP_ROWS - 1), 1, stride=1, stride_axis=0)
        p, half = divmod(h, HEADS_PER_VREG)
        bias_scr[p, half * GROUP_ROWS:(half + 1) * GROUP_ROWS, :] = jnp.where(
            past_band, NEG, skew[:, :n_var])
    row = lax.broadcasted_iota(jnp.int32, edge_scr.shape, 0) % GROUP_ROWS
    col = lax.broadcasted_iota(jnp.int32, edge_scr.shape, 1)
    edge_scr[...] = jnp.where(col < row // CHUNK * CHUNK, NEG, 0.0)


def _attn_prompt_kernel(q_ref, k_ref, v_ref, x_ref, wo_ref, u_ref, y_ref,
                        kbuf, vbuf, obuf, bias_scr, edge_scr, *, tq):
    t = pl.program_id(1)
    n_pairs = q_ref.shape[1] // LANES

    @pl.when((pl.program_id(0) == 0) & (t == 0))
    def _():
        _band_bias_init(u_ref, bias_scr, edge_scr)

    @pl.when(t == 0)
    def _():
        kbuf[0:tq, :] = jnp.zeros((tq, kbuf.shape[1]), BF16)
        vbuf[0:tq, :] = jnp.zeros((tq, vbuf.shape[1]), BF16)

    @pl.when(t > 0)
    def _():
        kbuf[0:tq, :] = kbuf[tq:2 * tq, :]
        vbuf[0:tq, :] = vbuf[tq:2 * tq, :]

    kbuf[tq:2 * tq, :] = k_ref[...]
    vbuf[tq:2 * tq, :] = v_ref[...]

    def chunk_loop(masked):
        span = ATTN_GROUPS_PER_TRIP * GROUP_ROWS

        def trip_body(i, carry):
            base = pl.multiple_of(i * span, span)

            def item(n):
                g, p = divmod(n, n_pairs)
                return base + g * GROUP_ROWS, slice(p * LANES, (p + 1) * LANES), p

            def scores(n):
                r0, cols, _ = item(n)
                return _pair_scores(q_ref[pl.ds(r0, GROUP_ROWS), cols],
                                    kbuf[pl.ds(r0, GROUP_KEYS), cols])

            def softmax(n, s):
                r0, _, p = item(n)
                s = jnp.concatenate(
                    [s[:, :LANES] + edge_scr[...], s[:, LANES:VAR_COL0],
                     s[:, VAR_COL0:] + bias_scr[p]], axis=1)
                lim = (REACH - (t * tq + r0)) if masked else None
                return _pair_softmax(s, None, lim)

            def output(n, e, inv_l):
                r0, cols, _ = item(n)
                o = _pair_output(e, inv_l, vbuf[pl.ds(r0, GROUP_KEYS), cols])
                obuf[pl.ds(r0, GROUP_ROWS), cols] = o.astype(BF16)

            def done(n):
                if (n + 1) % (PROJ_GROUPS * n_pairs) == 0:
                    rows = PROJ_GROUPS * GROUP_ROWS
                    r0 = pl.multiple_of(base + (n + 1) // n_pairs * GROUP_ROWS - rows, rows)
                    y_ref[pl.ds(r0, rows), :] = x_ref[pl.ds(r0, rows), :] + jnp.dot(
                        obuf[pl.ds(r0, rows), :], wo_ref[...].astype(BF16),
                        preferred_element_type=F32)

            _pipelined_pairs(ATTN_GROUPS_PER_TRIP * n_pairs, scores, softmax, output,
                             ATTN_LEAD_SCORES, ATTN_LEAD_SOFTMAX, done)
            return carry

        lax.fori_loop(0, tq // span, trip_body, 0)

    @pl.when(t == 0)
    def _():
        chunk_loop(True)

    @pl.when(t > 0)
    def _():
        chunk_loop(False)


def _attn_prompt(q, k, v, x, wo, table):
    b, s, d = x.shape
    tq = REACH
    assert s % tq == 0 and tq % (ATTN_GROUPS_PER_TRIP * GROUP_ROWS) == 0
    assert ATTN_GROUPS_PER_TRIP % PROJ_GROUPS == 0
    h = table.shape[0]
    n_diag = GROUP_ROWS - 1 + GROUP_KEYS - VAR_COL0
    diag = jnp.arange(n_diag) + (VAR_COL0 - GROUP_ROWS + 1)
    idx = jnp.clip(REACH - diag, -MAX_REL, MAX_REL) + MAX_REL
    u = (table[:, idx] - table[:, 2 * MAX_REL:]).astype(F32) * LOG2_E
    u = jnp.pad(u, ((0, 0), (0, pl.next_power_of_2(n_diag) - n_diag)))
    row = pl.BlockSpec((None, tq, d), lambda i, j: (i, j, 0))
    rows2 = HEADS_PER_VREG * GROUP_ROWS
    return pl.pallas_call(
        functools.partial(_attn_prompt_kernel, tq=tq),
        grid=(b, s // tq),
        in_specs=[row, row, row, row, _const_spec((d, d)), _const_spec(u.shape)],
        out_specs=row,
        out_shape=jax.ShapeDtypeStruct((b, s, d), F32),
        scratch_shapes=[pltpu.VMEM((2 * tq, d), BF16), pltpu.VMEM((2 * tq, d), BF16),
                        pltpu.VMEM((tq, d), BF16),
                        pltpu.VMEM((h // HEADS_PER_VREG, rows2, GROUP_KEYS - VAR_COL0), F32),
                        pltpu.VMEM((rows2, LANES), F32)],
        compiler_params=_params(2),
        name="attn_prompt",
    )(q, k, v, x, wo, u)


def _attn_sample_kernel(q_ref, k_ref, v_ref, ck_ref, cv_ref, x_ref, wo_ref, bias_ref, y_ref,
                        obuf):
    n_pairs = q_ref.shape[1] // LANES
    ca, d = ck_ref.shape[0], q_ref.shape[1]
    k_all = jnp.concatenate([ck_ref[...].reshape(ca, d).astype(BF16), k_ref[...]], axis=0)
    v_all = jnp.concatenate([cv_ref[...].reshape(ca, d).astype(BF16), v_ref[...]], axis=0)
    cols = [slice(p * LANES, (p + 1) * LANES) for p in range(n_pairs)]

    def scores(p):
        return _pair_scores(q_ref[:, cols[p]], k_all[:, cols[p]])

    def softmax(p, s):
        return _pair_softmax(s, bias_ref[p], None)

    def output(p, e, inv_l):
        obuf[:, cols[p]] = _pair_output(e, inv_l, v_all[:, cols[p]]).astype(BF16)

    _pipelined_pairs(n_pairs, scores, softmax, output)
    y_ref[...] = x_ref[...] + jnp.dot(obuf[...], wo_ref[...].astype(BF16),
                                      preferred_element_type=F32)


def _attn_sample(q, k, v, cache_k, cache_v, x, wo, bias_pairs):
    b, tn, d = x.shape
    _, ca, n_heads, head_dim = cache_k.shape
    new = pl.BlockSpec((None, tn, d), lambda i: (i, 0, 0))
    old = pl.BlockSpec((None, ca, n_heads, head_dim), lambda i: (i, 0, 0, 0))
    return pl.pallas_call(
        _attn_sample_kernel,
        grid=(b,),
        in_specs=[new, new, new, old, old, new, _const_spec((d, d)),
                  _const_spec(bias_pairs.shape)],
        out_specs=new,
        out_shape=jax.ShapeDtypeStruct((b, tn, d), F32),
        scratch_shapes=[pltpu.VMEM((tn, d), BF16)],
        compiler_params=_params(1),
        name="attn_sample",
    )(q, k, v, cache_k, cache_v, x, wo, bias_pairs)


def _gmlp_kernel(x_ref, g_ref, win_ref, vg_ref, ws_ref, bs_ref, wout_ref, *rest,
                 blk, emit_v):
    if emit_v:
        y_ref, vout_ref, gated = rest
    else:
        y_ref, gated = rest
    rows, d = x_ref.shape
    n_pairs = d // LANES
    x = x_ref[...]
    h = _rms(x, g_ref[...]).astype(BF16)
    def gelu(z):
        return 0.5 * z * (1.0 + lax.erf(z * np.float32(np.sqrt(0.5))))

    v = _rms(gelu(jnp.dot(h, win_ref[:, d:].astype(BF16), preferred_element_type=F32)),
             vg_ref[...])
    u = gelu(jnp.dot(h, win_ref[:, :d].astype(BF16), preferred_element_type=F32))
    if emit_v:
        vout_ref[...] = v
    n_blk = rows // blk
    lane = lax.broadcasted_iota(jnp.int32, (blk, n_blk * LANES), 1) & (LANES - 1)
    zero = jnp.zeros((blk, n_blk * LANES), F32)
    for p in range(n_pairs):
        cols = slice(p * LANES, (p + 1) * LANES)
        vcat = jnp.concatenate([v[r * blk:(r + 1) * blk, cols] for r in range(n_blk)], axis=1)
        v_stack = jnp.concatenate(
            [jnp.where(lane < HEAD_DIM, vcat, zero), jnp.where(lane >= HEAD_DIM, vcat, zero)],
            axis=0).astype(BF16)
        m = jnp.dot(ws_ref[p], v_stack, preferred_element_type=F32)
        for r in range(n_blk):
            rs = slice(r * blk, (r + 1) * blk)
            mixed = m[:, r * LANES:(r + 1) * LANES] + bs_ref[:, cols]
            gated[rs, cols] = (u[rs, cols] * mixed).astype(BF16)
    y_ref[...] = x + jnp.dot(gated[...], wout_ref[...].astype(BF16), preferred_element_type=F32)


def _gmlp(x, gain, w_in, v_gain, ws_pairs, bs_rows, w_out, rows, blk, emit_v):
    b, s, d = x.shape
    w = w_in.shape[1] - d
    assert s % rows == 0 and rows % blk == 0 and w == d
    row = pl.BlockSpec((None, rows, d), lambda i, j: (i, j, 0))
    out_specs = [row]
    out_shape = [jax.ShapeDtypeStruct((b, s, d), F32)]
    if emit_v:
        out_specs.append(row)
        out_shape.append(jax.ShapeDtypeStruct((b, s, w), F32))
    return pl.pallas_call(
        functools.partial(_gmlp_kernel, blk=blk, emit_v=emit_v),
        grid=(b, s // rows),
        in_specs=[row, _const_spec((1, d)), _const_spec(w_in.shape), _const_spec((1, w)),
                  _const_spec(ws_pairs.shape), _const_spec(bs_rows.shape),
                  _const_spec(w_out.shape)],
        out_specs=out_specs,
        out_shape=out_shape,
        scratch_shapes=[pltpu.VMEM((rows, w), BF16)],
        compiler_params=_params(2),
        name="gmlp",
    )(x, gain.reshape(1, d), w_in, v_gain.reshape(1, w), ws_pairs, bs_rows, w_out)


def _conv_taps(stage, n, w, cb):
    c = cb + stage[SUBLANES:SUBLANES + n, :] * w[2:3]
    c = c + stage[SUBLANES - 1:SUBLANES - 1 + n, :] * w[1:2]
    return c + stage[SUBLANES - 2:SUBLANES - 2 + n, :] * w[0:1]


def _ffn_kernel(x_ref, g_ref, wg_ref, wv_ref, cw_ref, cb_ref, wdn_ref, gf_ref,
                y_ref, carry_ref, h_scr, a_scr, act_scr, *, final_norm):
    t = pl.program_id(1)
    rows = x_ref.shape[0]
    ff = wdn_ref.shape[0]
    fc = FF_CHUNK

    @pl.when(t == 0)
    def _():
        carry_ref[...] = jnp.zeros_like(carry_ref)

    x = x_ref[...]
    h_scr[...] = _rms(x, g_ref[...]).astype(BF16)

    for j in range(ff // fc):
        chunk = slice(j * fc, (j + 1) * fc)
        halves = []
        for half, w_ref in enumerate((wg_ref, wv_ref)):
            cols = slice(half * ff + j * fc, half * ff + (j + 1) * fc)
            stage = a_scr.at[2 * (j % 2) + half]
            a = jnp.dot(h_scr[...], w_ref[:, chunk], preferred_element_type=F32)
            stage[0:SUBLANES, :] = carry_ref[:, cols]
            stage[SUBLANES:SUBLANES + rows, :] = a
            carry_ref[:, cols] = a[rows - SUBLANES:rows]
            halves.append(_conv_taps(stage, rows, cw_ref[:, cols], cb_ref[:, cols]))
        act_scr[:, chunk] = (jax.nn.silu(halves[0]) * halves[1]).astype(BF16)

    y = x + jnp.dot(act_scr[...], wdn_ref[...], preferred_element_type=F32)
    if final_norm:
        y = _rms(y, gf_ref[...])
    y_ref[...] = y


def _ffn_stream_kernel(x_ref, g_ref, hg_ref, hv_ref, wg_ref, wv_ref, cwg_ref, cwv_ref, cbg_ref,
                       cbv_ref, wdn_ref, gf_ref, y_ref, cg_ref, cv_ref, wg_out, wv_out, wdn_out,
                       h_scr, acc_scr, a_scr, *, n_sub, final_norm):
    j = pl.program_id(0)
    rows = x_ref.shape[0]
    seq = rows // n_sub

    @pl.when(j == 0)
    def _():
        h_scr[...] = _rms(x_ref[...], g_ref[...]).astype(BF16)
        acc_scr[...] = jnp.zeros_like(acc_scr)

    def branch(slot, w_ref, w_out, hist_ref, carry_out, cw_ref, cb_ref):
        w_bf = w_ref[...].astype(BF16)
        w_out[...] = w_bf
        a = jnp.dot(h_scr[...], w_bf, preferred_element_type=F32)
        outs = []
        for s in range(n_sub):
            stage = a_scr.at[slot, s]
            stage[0:SUBLANES, :] = hist_ref[s]
            stage[SUBLANES:SUBLANES + seq, :] = a[s * seq:(s + 1) * seq]
            carry_out[s] = a[(s + 1) * seq - SUBLANES:(s + 1) * seq]
            outs.append(_conv_taps(stage, seq, cw_ref[...], cb_ref[...]))
        return jnp.concatenate(outs, axis=0)

    gate = branch(0, wg_ref, wg_out, hg_ref, cg_ref, cwg_ref, cbg_ref)
    val = branch(1, wv_ref, wv_out, hv_ref, cv_ref, cwv_ref, cbv_ref)
    wd_bf = wdn_ref[...].astype(BF16)
    wdn_out[...] = wd_bf
    acc_scr[...] += jnp.dot((jax.nn.silu(gate) * val).astype(BF16), wd_bf,
                            preferred_element_type=F32)

    @pl.when(j == pl.num_programs(0) - 1)
    def _():
        y = x_ref[...] + acc_scr[...]
        if final_norm:
            y = _rms(y, gf_ref[...])
        y_ref[...] = y


def _layer_spec(shape, layer):
    nd = len(shape)
    return pl.BlockSpec((None,) + tuple(shape[1:]), lambda *_: (layer,) + (0,) * (nd - 1),
                        pipeline_mode=pl.Buffered(1))


def _conv_ffn(x, gain, wg_bf, wv_bf, conv_w, conv_b, wdn_bf, layer, final_gain, rows):
    b, s, d = x.shape
    ff = wdn_bf.shape[0]
    assert s % rows == 0 and rows % SUBLANES == 0 and ff % FF_CHUNK == 0
    final_norm = final_gain is not None
    gf = (final_gain if final_norm else jnp.ones((d,), F32)).reshape(1, d)
    row = pl.BlockSpec((None, rows, d), lambda i, j: (i, j, 0))
    carry_spec = pl.BlockSpec((None, SUBLANES, 2 * ff), lambda i, j: (i, 0, 0))
    return pl.pallas_call(
        functools.partial(_ffn_kernel, final_norm=final_norm),
        grid=(b, s // rows),
        in_specs=[row, _const_spec((1, d)), _const_spec(wg_bf.shape), _const_spec(wv_bf.shape),
                  _layer_spec(conv_w.shape, layer), _layer_spec(conv_b.shape, layer),
                  _const_spec(wdn_bf.shape), _const_spec((1, d))],
        out_specs=[row, carry_spec],
        out_shape=[jax.ShapeDtypeStruct((b, s, d), F32),
                   jax.ShapeDtypeStruct((b, SUBLANES, 2 * ff), F32)],
        scratch_shapes=[pltpu.VMEM((rows, d), BF16),
                        pltpu.VMEM((4, rows + SUBLANES, FF_CHUNK), F32),
                        pltpu.VMEM((rows, ff), BF16)],
        compiler_params=_params(2),
        name="conv_ffn",
    )(x, gain.reshape(1, d), wg_bf, wv_bf, conv_w, conv_b, wdn_bf, gf)


def _conv_ffn_stream(x, gain, hist, w_up, conv_w, conv_b, w_down, layer, final_gain, n_sub):
    r, d = x.shape
    ff = w_down.shape[1]
    step_bytes_per_col = 2 * 3 * d * (4 + 2)
    fc = max(ff // k for k in range(1, ff // LANES + 1)
             if ff % k == 0 and (ff // k) % LANES == 0
             and (ff // k) * step_bytes_per_col <= VMEM_LIMIT // 2)
    c = ff // fc
    seq = r // n_sub
    assert ff % fc == 0 and fc % LANES == 0 and seq % SUBLANES == 0
    final_norm = final_gain is not None
    gf = (final_gain if final_norm else jnp.ones((d,), F32)).reshape(1, d)

    def cols(blk_rows, half):
        return pl.BlockSpec((None, blk_rows, fc), lambda j: (layer, 0, half * c + j))

    def hist_cols(half):
        return pl.BlockSpec((n_sub, SUBLANES, fc), lambda j: (0, 0, half * c + j))

    half_cols = pl.BlockSpec((n_sub, SUBLANES, fc), lambda j: (0, 0, j))
    up_cols = pl.BlockSpec((d, fc), lambda j: (0, j))
    dn_rows = pl.BlockSpec((fc, d), lambda j: (j, 0))
    return pl.pallas_call(
        functools.partial(_ffn_stream_kernel, n_sub=n_sub, final_norm=final_norm),
        grid=(c,),
        in_specs=[_const_spec((r, d)), _const_spec((1, d)), hist_cols(0), hist_cols(1),
                  cols(d, 0), cols(d, 1), cols(CONV_W, 0), cols(CONV_W, 1), cols(1, 0), cols(1, 1),
                  pl.BlockSpec((None, fc, d), lambda j: (layer, j, 0)),
                  _const_spec((1, d))],
        out_specs=[pl.BlockSpec((r, d), lambda j: (0, 0)), half_cols, half_cols,
                   up_cols, up_cols, dn_rows],
        out_shape=[jax.ShapeDtypeStruct((r, d), F32),
                   jax.ShapeDtypeStruct((n_sub, SUBLANES, ff), F32),
                   jax.ShapeDtypeStruct((n_sub, SUBLANES, ff), F32),
                   jax.ShapeDtypeStruct((d, ff), BF16),
                   jax.ShapeDtypeStruct((d, ff), BF16),
                   jax.ShapeDtypeStruct((ff, d), BF16)],
        scratch_shapes=[pltpu.VMEM((r, d), BF16), pltpu.VMEM((r, d), F32),
                        pltpu.VMEM((2, n_sub, seq + SUBLANES, fc), F32)],
        compiler_params=_params(1),
        name="conv_ffn_stream",
    )(x, gain.reshape(1, d), hist, hist, w_up, w_up, conv_w, conv_w, conv_b, conv_b, w_down, gf)


def _pair_bias(table, n_q, n_k, reach):
    diag = jnp.arange(n_q + n_k - 1) - (n_q - 1)
    idx = jnp.clip(reach - diag, -MAX_REL, MAX_REL) + MAX_REL
    r = table[:, idx].astype(F32) * LOG2_E
    h, m = r.shape
    flat = jnp.tile(jnp.pad(r, ((0, 0), (0, 1))), (1, n_q))[:, :n_q * m]
    bias = flat.reshape(h, n_q, m)[:, :, n_q - 1:n_q - 1 + n_k]
    return bias.reshape(h // HEADS_PER_VREG, HEADS_PER_VREG * n_q, n_k)


def _pad_hist(hist):
    pad = [(0, 0)] * (hist.ndim - 2) + [(SUBLANES - hist.shape[-2], 0), (0, 0)]
    return jnp.pad(hist, pad)


def kernel(x_prompt, x_sample, cache_a_k, cache_a_v, state_ffn_conv, ln_mix, ln_ffn, ln_final,
           a_w_qkv, a_rel_bias, a_w_o, b_w_in, b_v_norm, b_w_s, b_bias_s, b_w_out,
           f_w_up, f_conv_w, f_conv_b, f_w_down):
    bp, sp, d = x_prompt.shape
    bs, ts, _ = x_sample.shape
    depth = ln_mix.shape[0]
    ca = cache_a_k.shape[2]
    keep = min(REACH, sp)
    two_ff = f_w_up.shape[2]
    hist_rows = CONV_W - 1
    conv_b_all = f_conv_b.reshape(depth, 1, two_ff)

    xp = x_prompt
    xs = x_sample.reshape(1, bs * ts, d)
    kp_l, vp_l, ks_l, vs_l, gv_l, cp_l, cs_l = [], [], [], [], [], [], []

    for i in range(depth):
        j = i // 2
        if i % 2 == 0:
            wqkv = a_w_qkv[j]
            wo = a_w_o[j]
            q, k, v, kt, vt = _qkv_proj(xp, ln_mix[i], wqkv, keep, ROW_TILE)
            xp = _attn_prompt(q, k, v, xp, wo, a_rel_bias[j])
            kp_l.append(kt.reshape(bp, keep, N_HEADS, HEAD_DIM))
            vp_l.append(vt.reshape(bp, keep, N_HEADS, HEAD_DIM))
            q, k, v, kt, vt = _qkv_proj(xs, ln_mix[i], wqkv, bs * ts, bs * ts)
            xs = _attn_sample(
                q.reshape(bs, ts, d), k.reshape(bs, ts, d), v.reshape(bs, ts, d),
                cache_a_k[j], cache_a_v[j],
                xs.reshape(bs, ts, d), wo, _pair_bias(a_rel_bias[j], ts, ca + ts, ca),
            ).reshape(1, bs * ts, d)
            ks_l.append(kt.reshape(bs, ts, N_HEADS, HEAD_DIM))
            vs_l.append(vt.reshape(bs, ts, N_HEADS, HEAD_DIM))
        else:
            win = b_w_in[j]
            wout = b_w_out[j]
            pos = jnp.arange(MLP_BLOCK)
            causal = (pos[None, :] // CHUNK) <= (pos[:, None] // CHUNK)
            ws = jnp.where(causal, b_w_s[j], 0.0).astype(BF16)
            g = ws.shape[0]

            def pair_rows(w):
                w = w.reshape(g // HEADS_PER_VREG, HEADS_PER_VREG, w.shape[1], w.shape[2])
                return jnp.transpose(w, (0, 2, 1, 3)).reshape(
                    g // HEADS_PER_VREG, w.shape[2], HEADS_PER_VREG * w.shape[3])

            def bias_rows(bias):
                return jnp.repeat(bias.T.astype(F32), d // g, axis=1)

            xp = _gmlp(xp, ln_mix[i], win, b_v_norm[j], pair_rows(ws), bias_rows(b_bias_s[j]),
                       wout, ROW_TILE, MLP_BLOCK, False)[0]
            xs, gv = _gmlp(xs, ln_mix[i], win, b_v_norm[j], pair_rows(ws[:, :ts, :ts]),
                           bias_rows(b_bias_s[j][:, :ts]), wout, bs * ts, ts, True)
            gv_l.append(gv.reshape(bs, ts, d))

        fin = ln_final if i == depth - 1 else None
        xs2, cs_g, cs_v, wg, wv, wdn = _conv_ffn_stream(
            xs[0], ln_ffn[i], _pad_hist(state_ffn_conv[i]), f_w_up, f_conv_w, conv_b_all,
            f_w_down, i, fin, bs)
        xs = xs2[None]
        xp, cp = _conv_ffn(xp, ln_ffn[i], wg, wv, f_conv_w, conv_b_all, wdn, i, fin,
                           FFN_ROW_TILE)
        cp_l.append(cp[:, SUBLANES - hist_rows:, :])
        cs_l.append(jnp.concatenate([cs_g, cs_v], axis=-1)[:, SUBLANES - hist_rows:, :])

    return (xp, xs.reshape(bs, ts, d), jnp.stack(kp_l), jnp.stack(vp_l), jnp.stack(ks_l),
            jnp.stack(vs_l), jnp.stack(gv_l), jnp.stack(cp_l), jnp.stack(cs_l))
```

```python
import functools

import jax
import jax.numpy as jnp
import numpy as np
from jax import lax
from jax.experimental import pallas as pl
from jax.experimental.pallas import tpu as pltpu

F32 = jnp.float32
BF16 = jnp.bfloat16

CHUNK = 64
PAST_CHUNKS = 8
REACH = PAST_CHUNKS * CHUNK
BAND = REACH + CHUNK
MAX_REL = 128
N_HEADS = 16
HEAD_DIM = 64
MLP_BLOCK = 128
CONV_W = 3
EPS = 1e-6
NEG = -1e30
LOG2_E = float(np.log2(np.e))
Q_SCALE = HEAD_DIM ** -0.5 * LOG2_E

LANES = 128
SUBLANES = 8
MXU_DIM = 256
HEADS_PER_VREG = LANES // HEAD_DIM
VMEM_LIMIT = 56 * 1024 * 1024

ROW_TILE = 1024
FFN_ROW_TILE = 512
FF_CHUNK = MXU_DIM
GROUP_CHUNKS = LANES // CHUNK
GROUP_ROWS = GROUP_CHUNKS * CHUNK
GROUP_KEYS = REACH + GROUP_ROWS
VAR_COL0 = (REACH - MAX_REL) // LANES * LANES
ATTN_GROUPS_PER_TRIP = 4
PROJ_GROUPS = 2
ATTN_LEAD_SCORES = 1
ATTN_LEAD_SOFTMAX = 3


def _rms(x, g):
    return x * lax.rsqrt(jnp.mean(x * x, axis=-1, keepdims=True) + EPS) * g


def _const_spec(shape):
    nd = len(shape)
    return pl.BlockSpec(shape, lambda *_: (0,) * nd, pipeline_mode=pl.Buffered(1))


def _params(n_grid):
    return pltpu.CompilerParams(
        dimension_semantics=("arbitrary",) * n_grid, vmem_limit_bytes=VMEM_LIMIT)


def _qkv_kernel(x_ref, g_ref, w_ref, q_ref, k_ref, v_ref, kt_ref, vt_ref, *, d, tail_tile0):
    t = pl.program_id(1)
    h = _rms(x_ref[...], g_ref[...]).astype(BF16)
    q = jnp.dot(h, w_ref[:, 0:d].astype(BF16), preferred_element_type=F32)
    q_ref[...] = (q * Q_SCALE).astype(BF16)
    k = jnp.dot(h, w_ref[:, d:2 * d].astype(BF16), preferred_element_type=F32)
    k_ref[...] = k.astype(BF16)
    v = jnp.dot(h, w_ref[:, 2 * d:3 * d].astype(BF16), preferred_element_type=F32)
    v_ref[...] = v.astype(BF16)

    @pl.when(t >= tail_tile0)
    def _():
        tail_rows = kt_ref.shape[0]
        kt_ref[...] = k[k.shape[0] - tail_rows:]
        vt_ref[...] = v[v.shape[0] - tail_rows:]


def _qkv_proj(x, gain, w_qkv, keep, tm):
    b, s, d = x.shape
    nt = s // tm
    tail_rows = min(tm, keep)
    assert s % tm == 0 and keep % tail_rows == 0
    tail_tile0 = nt - keep // tail_rows
    row = pl.BlockSpec((None, tm, d), lambda i, j: (i, j, 0))
    tail = pl.BlockSpec((None, tail_rows, d),
                        lambda i, j: (i, jnp.maximum(j - tail_tile0, 0), 0))
    return pl.pallas_call(
        functools.partial(_qkv_kernel, d=d, tail_tile0=tail_tile0),
        grid=(b, nt),
        in_specs=[row, _const_spec((1, d)), _const_spec((d, 3 * d))],
        out_specs=[row, row, row, tail, tail],
        out_shape=[jax.ShapeDtypeStruct((b, s, d), BF16)] * 3
        + [jax.ShapeDtypeStruct((b, keep, d), F32)] * 2,
        compiler_params=_params(2),
        name="qkv_proj",
    )(x, gain.reshape(1, d), w_qkv)


_CONTRACT_LANES = (((1,), (1,)), ((), ()))


def _block_diag_queries(qp):
    lane = lax.broadcasted_iota(jnp.int32, qp.shape, 1)
    qf = qp.astype(F32)
    zero = jnp.zeros_like(qf)
    return jnp.concatenate(
        [jnp.where(lane < HEAD_DIM, qf, zero), jnp.where(lane >= HEAD_DIM, qf, zero)],
        axis=0).astype(BF16)


def _pair_scores(qp, kb):
    return lax.dot_general(_block_diag_queries(qp), kb, _CONTRACT_LANES,
                           preferred_element_type=F32)


def _select_heads(pv):
    nq = pv.shape[0] // HEADS_PER_VREG
    lane = lax.broadcasted_iota(jnp.int32, (nq, LANES), 1)
    return jnp.where(lane < HEAD_DIM, pv[:nq], pv[nq:])


def _pair_softmax(s, bias, lim):
    if bias is not None:
        s = s + bias
    if lim is not None:
        col = lax.broadcasted_iota(jnp.int32, s.shape, 1)
        s = jnp.where(col >= lim, s, NEG)
    m = jnp.max(s, axis=-1, keepdims=True)
    e = jnp.exp2(s - m)
    return e.astype(BF16), 1.0 / jnp.sum(e, axis=-1, keepdims=True)


def _pair_output(e, inv_l, vb):
    return _select_heads(jnp.dot(e, vb, preferred_element_type=F32) * inv_l)


def _pipelined_pairs(n, scores, softmax, output, lead_scores=1, lead_softmax=1, done=None):
    s, e = {}, {}
    for step in range(n + lead_scores + lead_softmax):
        i_sm = step - lead_scores
        i_out = i_sm - lead_softmax
        if step < n:
            s[step] = scores(step)
        if 0 <= i_sm < n:
            e[i_sm] = softmax(i_sm, s.pop(i_sm))
        if i_out >= 0:
            output(i_out, *e.pop(i_out))
            if done is not None:
                done(i_out)


def _band_bias_init(u_ref, bias_scr, edge_scr):
    n_var = GROUP_KEYS - VAR_COL0
    ulen = u_ref.shape[1]
    row = lax.broadcasted_iota(jnp.int32, (GROUP_ROWS, n_var), 0)
    col = lax.broadcasted_iota(jnp.int32, (GROUP_ROWS, n_var), 1) + VAR_COL0
    past_band = col >= row // CHUNK * CHUNK + BAND
    for h in range(u_ref.shape[0]):
        per_diag = jnp.broadcast_to(u_ref[h:h + 1, :], (GROUP_ROWS, ulen))
        skew = pltpu.roll(per_diag, ulen - (GROUP_ROWS - 1), 1, stride=1, stride_axis=0)
        p, half = divmod(h, HEADS_PER_VREG)
        bias_scr[p, half * GROUP_ROWS:(half + 1) * GROUP_ROWS, :] = jnp.where(
            past_band, NEG, skew[:, :n_var])
    row = lax.broadcasted_iota(jnp.int32, edge_scr.shape, 0) % GROUP_ROWS
    col = lax.broadcasted_iota(jnp.int32, edge_scr.shape, 1)
    edge_scr[...] = jnp.where(col < row // CHUNK * CHUNK, NEG, 0.0)


def _attn_prompt_kernel(q_ref, k_ref, v_ref, x_ref, wo_ref, u_ref, y_ref,
                        kbuf, vbuf, obuf, bias_scr, edge_scr, *, tq):
    t = pl.program_id(1)
    n_pairs = q_ref.shape[1] // LANES

    @pl.when((pl.program_id(0) == 0) & (t == 0))
    def _():
        _band_bias_init(u_ref, bias_scr, edge_scr)

    @pl.when(t == 0)
    def _():
        kbuf[0:tq, :] = jnp.zeros((tq, kbuf.shape[1]), BF16)
        vbuf[0:tq, :] = jnp.zeros((tq, vbuf.shape[1]), BF16)

    @pl.when(t > 0)
    def _():
        kbuf[0:tq, :] = kbuf[tq:2 * tq, :]
        vbuf[0:tq, :] = vbuf[tq:2 * tq, :]

    kbuf[tq:2 * tq, :] = k_ref[...]
    vbuf[tq:2 * tq, :] = v_ref[...]

    def chunk_loop(masked):
        span = ATTN_GROUPS_PER_TRIP * GROUP_ROWS

        def trip_body(i, carry):
            base = pl.multiple_of(i * span, span)

            def item(n):
                g, p = divmod(n, n_pairs)
                return base + g * GROUP_ROWS, slice(p * LANES, (p + 1) * LANES), p

            def scores(n):
                r0, cols, _ = item(n)
                return _pair_scores(q_ref[pl.ds(r0, GROUP_ROWS), cols],
                                    kbuf[pl.ds(r0, GROUP_KEYS), cols])

            def softmax(n, s):
                r0, _, p = item(n)
                s = jnp.concatenate(
                    [s[:, :LANES] + edge_scr[...], s[:, LANES:VAR_COL0],
                     s[:, VAR_COL0:] + bias_scr[p]], axis=1)
                lim = (REACH - (t * tq + r0)) if masked else None
                return _pair_softmax(s, None, lim)

            def output(n, e, inv_l):
                r0, cols, _ = item(n)
                o = _pair_output(e, inv_l, vbuf[pl.ds(r0, GROUP_KEYS), cols])
                obuf[pl.ds(r0, GROUP_ROWS), cols] = o.astype(BF16)

            def done(n):
                if (n + 1) % (PROJ_GROUPS * n_pairs) == 0:
                    rows = PROJ_GROUPS * GROUP_ROWS
                    r0 = pl.multiple_of(base + (n + 1) // n_pairs * GROUP_ROWS - rows, rows)
                    y_ref[pl.ds(r0, rows), :] = x_ref[pl.ds(r0, rows), :] + jnp.dot(
                        obuf[pl.ds(r0, rows), :], wo_ref[...].astype(BF16),
                        preferred_element_type=F32)

            _pipelined_pairs(ATTN_GROUPS_PER_TRIP * n_pairs, scores, softmax, output,
                             ATTN_LEAD_SCORES, ATTN_LEAD_SOFTMAX, done)
            return carry

        lax.fori_loop(0, tq // span, trip_body, 0)

    @pl.when(t == 0)
    def _():
        chunk_loop(True)

    @pl.when(t > 0)
    def _():
        chunk_loop(False)


def _attn_prompt(q, k, v, x, wo, table):
    b, s, d = x.shape
    tq = REACH
    assert s % tq == 0 and tq % (ATTN_GROUPS_PER_TRIP * GROUP_ROWS) == 0
    assert ATTN_GROUPS_PER_TRIP % PROJ_GROUPS == 0
    h = table.shape[0]
    n_diag = GROUP_ROWS - 1 + GROUP_KEYS - VAR_COL0
    diag = jnp.arange(n_diag) + (VAR_COL0 - GROUP_ROWS + 1)
    idx = jnp.clip(REACH - diag, -MAX_REL, MAX_REL) + MAX_REL
    u = (table[:, idx] - table[:, 2 * MAX_REL:]).astype(F32) * LOG2_E
    u = jnp.pad(u, ((0, 0), (0, pl.next_power_of_2(n_diag) - n_diag)))
    row = pl.BlockSpec((None, tq, d), lambda i, j: (i, j, 0))
    rows2 = HEADS_PER_VREG * GROUP_ROWS
    return pl.pallas_call(
        functools.partial(_attn_prompt_kernel, tq=tq),
        grid=(b, s // tq),
        in_specs=[row, row, row, row, _const_spec((d, d)), _const_spec(u.shape)],
        out_specs=row,
        out_shape=jax.ShapeDtypeStruct((b, s, d), F32),
        scratch_shapes=[pltpu.VMEM((2 * tq, d), BF16), pltpu.VMEM((2 * tq, d), BF16),
                        pltpu.VMEM((tq, d), BF16),
                        pltpu.VMEM((h // HEADS_PER_VREG, rows2, GROUP_KEYS - VAR_COL0), F32),
                        pltpu.VMEM((rows2, LANES), F32)],
        compiler_params=_params(2),
        name="attn_prompt",
    )(q, k, v, x, wo, u)


def _attn_sample_kernel(q_ref, k_ref, v_ref, ckt_ref, cvt_ref, x_ref, wo_ref, bias_ref, y_ref,
                        obuf):
    n_pairs = q_ref.shape[1] // LANES
    ca = ckt_ref.shape[2]
    cols = [slice(p * LANES, (p + 1) * LANES) for p in range(n_pairs)]

    def pair_rows(ref, p):
        return jnp.concatenate([ref[HEADS_PER_VREG * p + i] for i in range(HEADS_PER_VREG)],
                               axis=0).astype(BF16)

    def scores(p):
        q_bd = _block_diag_queries(q_ref[:, cols[p]])
        s_cache = jnp.dot(q_bd, pair_rows(ckt_ref, p), preferred_element_type=F32)
        s_new = lax.dot_general(q_bd, k_ref[:, cols[p]], _CONTRACT_LANES,
                                preferred_element_type=F32)
        return jnp.concatenate([s_cache, s_new], axis=1)

    def softmax(p, s):
        return _pair_softmax(s, bias_ref[p], None)

    def output(p, e, inv_l):
        pv = lax.dot_general(e[:, :ca], pair_rows(cvt_ref, p), _CONTRACT_LANES,
                             preferred_element_type=F32)
        pv = pv + jnp.dot(e[:, ca:], v_ref[:, cols[p]], preferred_element_type=F32)
        obuf[:, cols[p]] = _select_heads(pv * inv_l).astype(BF16)

    _pipelined_pairs(n_pairs, scores, softmax, output)
    y_ref[...] = x_ref[...] + jnp.dot(obuf[...], wo_ref[...].astype(BF16),
                                      preferred_element_type=F32)


def _attn_sample(q, k, v, cache_kt, cache_vt, x, wo, bias_pairs):
    b, tn, d = x.shape
    new = pl.BlockSpec((None, tn, d), lambda i: (i, 0, 0))
    old = pl.BlockSpec((None,) + cache_kt.shape[1:], lambda i: (i, 0, 0, 0))
    return pl.pallas_call(
        _attn_sample_kernel,
        grid=(b,),
        in_specs=[new, new, new, old, old, new, _const_spec((d, d)),
                  _const_spec(bias_pairs.shape)],
        out_specs=new,
        out_shape=jax.ShapeDtypeStruct((b, tn, d), F32),
        scratch_shapes=[pltpu.VMEM((tn, d), BF16)],
        compiler_params=_params(1),
        name="attn_sample",
    )(q, k, v, cache_kt, cache_vt, x, wo, bias_pairs)


def _gmlp_kernel(x_ref, g_ref, win_ref, vg_ref, ws_ref, bs_ref, wout_ref, *rest,
                 blk, emit_v):
    if emit_v:
        y_ref, vout_ref, gated = rest
    else:
        y_ref, gated = rest
    rows, d = x_ref.shape
    n_pairs = d // LANES
    x = x_ref[...]
    h = _rms(x, g_ref[...]).astype(BF16)
    def gelu(z):
        return 0.5 * z * (1.0 + lax.erf(z * np.float32(np.sqrt(0.5))))

    v = _rms(gelu(jnp.dot(h, win_ref[:, d:].astype(BF16), preferred_element_type=F32)),
             vg_ref[...])
    u = gelu(jnp.dot(h, win_ref[:, :d].astype(BF16), preferred_element_type=F32))
    if emit_v:
        vout_ref[...] = v
    n_blk = rows // blk
    lane = lax.broadcasted_iota(jnp.int32, (blk, n_blk * LANES), 1) & (LANES - 1)
    zero = jnp.zeros((blk, n_blk * LANES), F32)
    for p in range(n_pairs):
        cols = slice(p * LANES, (p + 1) * LANES)
        vcat = jnp.concatenate([v[r * blk:(r + 1) * blk, cols] for r in range(n_blk)], axis=1)
        v_stack = jnp.concatenate(
            [jnp.where(lane < HEAD_DIM, vcat, zero), jnp.where(lane >= HEAD_DIM, vcat, zero)],
            axis=0).astype(BF16)
        m = jnp.dot(ws_ref[p], v_stack, preferred_element_type=F32)
        for r in range(n_blk):
            rs = slice(r * blk, (r + 1) * blk)
            mixed = m[:, r * LANES:(r + 1) * LANES] + bs_ref[:, cols]
            gated[rs, cols] = (u[rs, cols] * mixed).astype(BF16)
    y_ref[...] = x + jnp.dot(gated[...], wout_ref[...].astype(BF16), preferred_element_type=F32)


def _gmlp(x, gain, w_in, v_gain, ws_pairs, bs_rows, w_out, rows, blk, emit_v):
    b, s, d = x.shape
    w = w_in.shape[1] - d
    assert s % rows == 0 and rows % blk == 0 and w == d
    row = pl.BlockSpec((None, rows, d), lambda i, j: (i, j, 0))
    out_specs = [row]
    out_shape = [jax.ShapeDtypeStruct((b, s, d), F32)]
    if emit_v:
        out_specs.append(row)
        out_shape.append(jax.ShapeDtypeStruct((b, s, w), F32))
    return pl.pallas_call(
        functools.partial(_gmlp_kernel, blk=blk, emit_v=emit_v),
        grid=(b, s // rows),
        in_specs=[row, _const_spec((1, d)), _const_spec(w_in.shape), _const_spec((1, w)),
                  _const_spec(ws_pairs.shape), _const_spec(bs_rows.shape),
                  _const_spec(w_out.shape)],
        out_specs=out_specs,
        out_shape=out_shape,
        scratch_shapes=[pltpu.VMEM((rows, w), BF16)],
        compiler_params=_params(2),
        name="gmlp",
    )(x, gain.reshape(1, d), w_in, v_gain.reshape(1, w), ws_pairs, bs_rows, w_out)


def _conv_taps(stage, n, w, cb):
    c = cb + stage[SUBLANES:SUBLANES + n, :] * w[2:3]
    c = c + stage[SUBLANES - 1:SUBLANES - 1 + n, :] * w[1:2]
    return c + stage[SUBLANES - 2:SUBLANES - 2 + n, :] * w[0:1]


def _ffn_kernel(x_ref, g_ref, wg_ref, wv_ref, cw_ref, cb_ref, wdn_ref, gf_ref,
                y_ref, carry_ref, h_scr, a_scr, act_scr, *, final_norm):
    t = pl.program_id(1)
    rows = x_ref.shape[0]
    ff = wdn_ref.shape[0]
    fc = FF_CHUNK

    @pl.when(t == 0)
    def _():
        carry_ref[...] = jnp.zeros_like(carry_ref)

    x = x_ref[...]
    h_scr[...] = _rms(x, g_ref[...]).astype(BF16)

    for j in range(ff // fc):
        chunk = slice(j * fc, (j + 1) * fc)
        halves = []
        for half, w_ref in enumerate((wg_ref, wv_ref)):
            cols = slice(half * ff + j * fc, half * ff + (j + 1) * fc)
            stage = a_scr.at[2 * (j % 2) + half]
            a = jnp.dot(h_scr[...], w_ref[:, chunk], preferred_element_type=F32)
            stage[0:SUBLANES, :] = carry_ref[:, cols]
            stage[SUBLANES:SUBLANES + rows, :] = a
            carry_ref[:, cols] = a[rows - SUBLANES:rows]
            halves.append(_conv_taps(stage, rows, cw_ref[:, cols], cb_ref[:, cols]))
        act_scr[:, chunk] = (jax.nn.silu(halves[0]) * halves[1]).astype(BF16)

    y = x + jnp.dot(act_scr[...], wdn_ref[...], preferred_element_type=F32)
    if final_norm:
        y = _rms(y, gf_ref[...])
    y_ref[...] = y


def _ffn_stream_kernel(x_ref, g_ref, hg_ref, hv_ref, wg_ref, wv_ref, cwg_ref, cwv_ref, cbg_ref,
                       cbv_ref, wdn_ref, gf_ref, y_ref, cg_ref, cv_ref, wg_out, wv_out, wdn_out,
                       h_scr, acc_scr, a_scr, *, n_sub, final_norm):
    j = pl.program_id(0)
    rows = x_ref.shape[0]
    seq = rows // n_sub

    @pl.when(j == 0)
    def _():
        h_scr[...] = _rms(x_ref[...], g_ref[...]).astype(BF16)
        acc_scr[...] = jnp.zeros_like(acc_scr)

    def branch(slot, w_ref, w_out, hist_ref, carry_out, cw_ref, cb_ref):
        w_bf = w_ref[...].astype(BF16)
        w_out[...] = w_bf
        a = jnp.dot(h_scr[...], w_bf, preferred_element_type=F32)
        outs = []
        for s in range(n_sub):
            stage = a_scr.at[slot, s]
            stage[0:SUBLANES, :] = hist_ref[s]
            stage[SUBLANES:SUBLANES + seq, :] = a[s * seq:(s + 1) * seq]
            carry_out[s] = a[(s + 1) * seq - SUBLANES:(s + 1) * seq]
            outs.append(_conv_taps(stage, seq, cw_ref[...], cb_ref[...]))
        return jnp.concatenate(outs, axis=0)

    gate = branch(0, wg_ref, wg_out, hg_ref, cg_ref, cwg_ref, cbg_ref)
    val = branch(1, wv_ref, wv_out, hv_ref, cv_ref, cwv_ref, cbv_ref)
    wd_bf = wdn_ref[...].astype(BF16)
    wdn_out[...] = wd_bf
    acc_scr[...] += jnp.dot((jax.nn.silu(gate) * val).astype(BF16), wd_bf,
                            preferred_element_type=F32)

    @pl.when(j == pl.num_programs(0) - 1)
    def _():
        y = x_ref[...] + acc_scr[...]
        if final_norm:
            y = _rms(y, gf_ref[...])
        y_ref[...] = y


def _layer_spec(shape, layer):
    nd = len(shape)
    return pl.BlockSpec((None,) + tuple(shape[1:]), lambda *_: (layer,) + (0,) * (nd - 1),
                        pipeline_mode=pl.Buffered(1))


def _conv_ffn(x, gain, wg_bf, wv_bf, conv_w, conv_b, wdn_bf, layer, final_gain, rows):
    b, s, d = x.shape
    ff = wdn_bf.shape[0]
    assert s % rows == 0 and rows % SUBLANES == 0 and ff % FF_CHUNK == 0
    final_norm = final_gain is not None
    gf = (final_gain if final_norm else jnp.ones((d,), F32)).reshape(1, d)
    row = pl.BlockSpec((None, rows, d), lambda i, j: (i, j, 0))
    carry_spec = pl.BlockSpec((None, SUBLANES, 2 * ff), lambda i, j: (i, 0, 0))
    return pl.pallas_call(
        functools.partial(_ffn_kernel, final_norm=final_norm),
        grid=(b, s // rows),
        in_specs=[row, _const_spec((1, d)), _const_spec(wg_bf.shape), _const_spec(wv_bf.shape),
                  _layer_spec(conv_w.shape, layer), _layer_spec(conv_b.shape, layer),
                  _const_spec(wdn_bf.shape), _const_spec((1, d))],
        out_specs=[row, carry_spec],
        out_shape=[jax.ShapeDtypeStruct((b, s, d), F32),
                   jax.ShapeDtypeStruct((b, SUBLANES, 2 * ff), F32)],
        scratch_shapes=[pltpu.VMEM((rows, d), BF16),
                        pltpu.VMEM((4, rows + SUBLANES, FF_CHUNK), F32),
                        pltpu.VMEM((rows, ff), BF16)],
        compiler_params=_params(2),
        name="conv_ffn",
    )(x, gain.reshape(1, d), wg_bf, wv_bf, conv_w, conv_b, wdn_bf, gf)


def _conv_ffn_stream(x, gain, hist, w_up, conv_w, conv_b, w_down, layer, final_gain, n_sub):
    r, d = x.shape
    ff = w_down.shape[1]
    step_bytes_per_col = 2 * 3 * d * (4 + 2)
    fc = max(ff // k for k in range(1, ff // LANES + 1)
             if ff % k == 0 and (ff // k) % LANES == 0
             and (ff // k) * step_bytes_per_col <= VMEM_LIMIT // 2)
    c = ff // fc
    seq = r // n_sub
    assert ff % fc == 0 and fc % LANES == 0 and seq % SUBLANES == 0
    final_norm = final_gain is not None
    gf = (final_gain if final_norm else jnp.ones((d,), F32)).reshape(1, d)

    def cols(blk_rows, half):
        return pl.BlockSpec((None, blk_rows, fc), lambda j: (layer, 0, half * c + j))

    def hist_cols(half):
        return pl.BlockSpec((n_sub, SUBLANES, fc), lambda j: (0, 0, half * c + j))

    half_cols = pl.BlockSpec((n_sub, SUBLANES, fc), lambda j: (0, 0, j))
    up_cols = pl.BlockSpec((d, fc), lambda j: (0, j))
    dn_rows = pl.BlockSpec((fc, d), lambda j: (j, 0))
    return pl.pallas_call(
        functools.partial(_ffn_stream_kernel, n_sub=n_sub, final_norm=final_norm),
        grid=(c,),
        in_specs=[_const_spec((r, d)), _const_spec((1, d)), hist_cols(0), hist_cols(1),
                  cols(d, 0), cols(d, 1), cols(CONV_W, 0), cols(CONV_W, 1), cols(1, 0), cols(1, 1),
                  pl.BlockSpec((None, fc, d), lambda j: (layer, j, 0)),
                  _const_spec((1, d))],
        out_specs=[pl.BlockSpec((r, d), lambda j: (0, 0)), half_cols, half_cols,
                   up_cols, up_cols, dn_rows],
        out_shape=[jax.ShapeDtypeStruct((r, d), F32),
                   jax.ShapeDtypeStruct((n_sub, SUBLANES, ff), F32),
                   jax.ShapeDtypeStruct((n_sub, SUBLANES, ff), F32),
                   jax.ShapeDtypeStruct((d, ff), BF16),
                   jax.ShapeDtypeStruct((d, ff), BF16),
                   jax.ShapeDtypeStruct((ff, d), BF16)],
        scratch_shapes=[pltpu.VMEM((r, d), BF16), pltpu.VMEM((r, d), F32),
                        pltpu.VMEM((2, n_sub, seq + SUBLANES, fc), F32)],
        compiler_params=_params(1),
        name="conv_ffn_stream",
    )(x, gain.reshape(1, d), hist, hist, w_up, w_up, conv_w, conv_w, conv_b, conv_b, w_down, gf)


def _pair_bias(table, n_q, n_k, reach):
    diag = jnp.arange(n_q + n_k - 1) - (n_q - 1)
    idx = jnp.clip(reach - diag, -MAX_REL, MAX_REL) + MAX_REL
    r = table[:, idx].astype(F32) * LOG2_E
    h, m = r.shape
    flat = jnp.tile(jnp.pad(r, ((0, 0), (0, 1))), (1, n_q))[:, :n_q * m]
    bias = flat.reshape(h, n_q, m)[:, :, n_q - 1:n_q - 1 + n_k]
    return bias.reshape(h // HEADS_PER_VREG, HEADS_PER_VREG * n_q, n_k)


def _pad_hist(hist):
    pad = [(0, 0)] * (hist.ndim - 2) + [(SUBLANES - hist.shape[-2], 0), (0, 0)]
    return jnp.pad(hist, pad)


def kernel(x_prompt, x_sample, cache_a_k, cache_a_v, state_ffn_conv, ln_mix, ln_ffn, ln_final,
           a_w_qkv, a_rel_bias, a_w_o, b_w_in, b_v_norm, b_w_s, b_bias_s, b_w_out,
           f_w_up, f_conv_w, f_conv_b, f_w_down):
    bp, sp, d = x_prompt.shape
    bs, ts, _ = x_sample.shape
    depth = ln_mix.shape[0]
    ca = cache_a_k.shape[2]
    keep = min(REACH, sp)
    two_ff = f_w_up.shape[2]
    hist_rows = CONV_W - 1
    conv_b_all = f_conv_b.reshape(depth, 1, two_ff)

    xp = x_prompt
    xs = x_sample.reshape(1, bs * ts, d)
    kp_l, vp_l, ks_l, vs_l, gv_l, cp_l, cs_l = [], [], [], [], [], [], []

    for i in range(depth):
        j = i // 2
        if i % 2 == 0:
            wqkv = a_w_qkv[j]
            wo = a_w_o[j]
            q, k, v, kt, vt = _qkv_proj(xp, ln_mix[i], wqkv, keep, ROW_TILE)
            xp = _attn_prompt(q, k, v, xp, wo, a_rel_bias[j])
            kp_l.append(kt.reshape(bp, keep, N_HEADS, HEAD_DIM))
            vp_l.append(vt.reshape(bp, keep, N_HEADS, HEAD_DIM))
            q, k, v, kt, vt = _qkv_proj(xs, ln_mix[i], wqkv, bs * ts, bs * ts)
            xs = _attn_sample(
                q.reshape(bs, ts, d), k.reshape(bs, ts, d), v.reshape(bs, ts, d),
                jnp.transpose(cache_a_k[j], (0, 2, 3, 1)),
                jnp.transpose(cache_a_v[j], (0, 2, 3, 1)),
                xs.reshape(bs, ts, d), wo, _pair_bias(a_rel_bias[j], ts, ca + ts, ca),
            ).reshape(1, bs * ts, d)
            ks_l.append(kt.reshape(bs, ts, N_HEADS, HEAD_DIM))
            vs_l.append(vt.reshape(bs, ts, N_HEADS, HEAD_DIM))
        else:
            win = b_w_in[j]
            wout = b_w_out[j]
            pos = jnp.arange(MLP_BLOCK)
            causal = (pos[None, :] // CHUNK) <= (pos[:, None] // CHUNK)
            ws = jnp.where(causal, b_w_s[j], 0.0).astype(BF16)
            g = ws.shape[0]

            def pair_rows(w):
                w = w.reshape(g // HEADS_PER_VREG, HEADS_PER_VREG, w.shape[1], w.shape[2])
                return jnp.transpose(w, (0, 2, 1, 3)).reshape(
                    g // HEADS_PER_VREG, w.shape[2], HEADS_PER_VREG * w.shape[3])

            def bias_rows(bias):
                return jnp.repeat(bias.T.astype(F32), d // g, axis=1)

            xp = _gmlp(xp, ln_mix[i], win, b_v_norm[j], pair_rows(ws), bias_rows(b_bias_s[j]),
                       wout, ROW_TILE, MLP_BLOCK, False)[0]
            xs, gv = _gmlp(xs, ln_mix[i], win, b_v_norm[j], pair_rows(ws[:, :ts, :ts]),
                           bias_rows(b_bias_s[j][:, :ts]), wout, bs * ts, ts, True)
            gv_l.append(gv.reshape(bs, ts, d))

        fin = ln_final if i == depth - 1 else None
        xs2, cs_g, cs_v, wg, wv, wdn = _conv_ffn_stream(
            xs[0], ln_ffn[i], _pad_hist(state_ffn_conv[i]), f_w_up, f_conv_w, conv_b_all,
            f_w_down, i, fin, bs)
        xs = xs2[None]
        xp, cp = _conv_ffn(xp, ln_ffn[i], wg, wv, f_conv_w, conv_b_all, wdn, i, fin,
                           FFN_ROW_TILE)
        cp_l.append(cp[:, SUBLANES - hist_rows:, :])
        cs_l.append(jnp.concatenate([cs_g, cs_v], axis=-1)[:, SUBLANES - hist_rows:, :])

    return (xp, xs.reshape(bs, ts, d), jnp.stack(kp_l), jnp.stack(vp_l), jnp.stack(ks_l),
            jnp.stack(vs_l), jnp.stack(gv_l), jnp.stack(cp_l), jnp.stack(cs_l))
```

```python
import functools

import jax
import jax.numpy as jnp
import numpy as np
from jax import lax
from jax.experimental import pallas as pl
from jax.experimental.pallas import tpu as pltpu

F32 = jnp.float32
BF16 = jnp.bfloat16

CHUNK = 64
PAST_CHUNKS = 8
REACH = PAST_CHUNKS * CHUNK
BAND = REACH + CHUNK
MAX_REL = 128
N_HEADS = 16
HEAD_DIM = 64
MLP_BLOCK = 128
CONV_W = 3
EPS = 1e-6
NEG = -1e30
LOG2_E = float(np.log2(np.e))
Q_SCALE = HEAD_DIM ** -0.5 * LOG2_E

LANES = 128
SUBLANES = 8
MXU_DIM = 256
HEADS_PER_VREG = LANES // HEAD_DIM
VMEM_LIMIT = 56 * 1024 * 1024

ROW_TILE = 1024
FFN_ROW_TILE = 512
FF_CHUNK = MXU_DIM
GROUP_CHUNKS = LANES // CHUNK
GROUP_ROWS = GROUP_CHUNKS * CHUNK
GROUP_KEYS = REACH + GROUP_ROWS
VAR_COL0 = (REACH - MAX_REL) // LANES * LANES
ATTN_GROUPS_PER_TRIP = 4
PROJ_GROUPS = 2
ATTN_LEAD_SCORES = 1
ATTN_LEAD_SOFTMAX = 3


def _rms(x, g):
    return x * lax.rsqrt(jnp.mean(x * x, axis=-1, keepdims=True) + EPS) * g


def _const_spec(shape):
    nd = len(shape)
    return pl.BlockSpec(shape, lambda *_: (0,) * nd, pipeline_mode=pl.Buffered(1))


def _params(n_grid):
    return pltpu.CompilerParams(
        dimension_semantics=("arbitrary",) * n_grid, vmem_limit_bytes=VMEM_LIMIT)


def _qkv_kernel(x_ref, g_ref, w_ref, q_ref, k_ref, v_ref, kt_ref, vt_ref,
                *, d, tail_tile0, tail_rows, time_minor):
    t = pl.program_id(1)
    h = _rms(x_ref[...], g_ref[...]).astype(BF16)
    q = jnp.dot(h, w_ref[:, 0:d].astype(BF16), preferred_element_type=F32)
    q_ref[...] = (q * Q_SCALE).astype(BF16)
    k = jnp.dot(h, w_ref[:, d:2 * d].astype(BF16), preferred_element_type=F32)
    k_ref[...] = k.astype(BF16)
    v = jnp.dot(h, w_ref[:, 2 * d:3 * d].astype(BF16), preferred_element_type=F32)
    v_ref[...] = v.astype(BF16)

    @pl.when(t >= tail_tile0)
    def _():
        kt = k[k.shape[0] - tail_rows:]
        vt = v[v.shape[0] - tail_rows:]
        kt_ref[...] = kt.T if time_minor else kt
        vt_ref[...] = vt.T if time_minor else vt


def _qkv_proj(x, gain, w_qkv, keep, tm, time_minor):
    b, s, d = x.shape
    nt = s // tm
    tail_rows = min(tm, keep)
    assert s % tm == 0 and keep % tail_rows == 0
    tail_tile0 = nt - keep // tail_rows
    row = pl.BlockSpec((None, tm, d), lambda i, j: (i, j, 0))
    if time_minor:
        tail = pl.BlockSpec((None, d, tail_rows),
                            lambda i, j: (i, 0, jnp.maximum(j - tail_tile0, 0)))
        tail_shape = (b, d, keep)
    else:
        tail = pl.BlockSpec((None, tail_rows, d),
                            lambda i, j: (i, jnp.maximum(j - tail_tile0, 0), 0))
        tail_shape = (b, keep, d)
    return pl.pallas_call(
        functools.partial(_qkv_kernel, d=d, tail_tile0=tail_tile0, tail_rows=tail_rows,
                          time_minor=time_minor),
        grid=(b, nt),
        in_specs=[row, _const_spec((1, d)), _const_spec((d, 3 * d))],
        out_specs=[row, row, row, tail, tail],
        out_shape=[jax.ShapeDtypeStruct((b, s, d), BF16)] * 3
        + [jax.ShapeDtypeStruct(tail_shape, F32)] * 2,
        compiler_params=_params(2),
        name="qkv_proj",
    )(x, gain.reshape(1, d), w_qkv)


_CONTRACT_LANES = (((1,), (1,)), ((), ()))


def _block_diag_queries(qp):
    lane = lax.broadcasted_iota(jnp.int32, qp.shape, 1)
    qf = qp.astype(F32)
    zero = jnp.zeros_like(qf)
    return jnp.concatenate(
        [jnp.where(lane < HEAD_DIM, qf, zero), jnp.where(lane >= HEAD_DIM, qf, zero)],
        axis=0).astype(BF16)


def _pair_scores(qp, kb):
    return lax.dot_general(_block_diag_queries(qp), kb, _CONTRACT_LANES,
                           preferred_element_type=F32)


def _select_heads(pv):
    nq = pv.shape[0] // HEADS_PER_VREG
    lane = lax.broadcasted_iota(jnp.int32, (nq, LANES), 1)
    return jnp.where(lane < HEAD_DIM, pv[:nq], pv[nq:])


def _pair_softmax(s, bias, lim):
    if bias is not None:
        s = s + bias
    if lim is not None:
        col = lax.broadcasted_iota(jnp.int32, s.shape, 1)
        s = jnp.where(col >= lim, s, NEG)
    m = jnp.max(s, axis=-1, keepdims=True)
    e = jnp.exp2(s - m)
    return e.astype(BF16), 1.0 / jnp.sum(e, axis=-1, keepdims=True)


def _pair_output(e, inv_l, vb):
    return _select_heads(jnp.dot(e, vb, preferred_element_type=F32) * inv_l)


def _pipelined_pairs(n, scores, softmax, output, lead_scores=1, lead_softmax=1, done=None):
    s, e = {}, {}
    for step in range(n + lead_scores + lead_softmax):
        i_sm = step - lead_scores
        i_out = i_sm - lead_softmax
        if step < n:
            s[step] = scores(step)
        if 0 <= i_sm < n:
            e[i_sm] = softmax(i_sm, s.pop(i_sm))
        if i_out >= 0:
            output(i_out, *e.pop(i_out))
            if done is not None:
                done(i_out)


def _band_bias_init(u_ref, bias_scr, edge_scr):
    n_var = GROUP_KEYS - VAR_COL0
    ulen = u_ref.shape[1]
    row = lax.broadcasted_iota(jnp.int32, (GROUP_ROWS, n_var), 0)
    col = lax.broadcasted_iota(jnp.int32, (GROUP_ROWS, n_var), 1) + VAR_COL0
    past_band = col >= row // CHUNK * CHUNK + BAND
    for h in range(u_ref.shape[0]):
        per_diag = jnp.broadcast_to(u_ref[h:h + 1, :], (GROUP_ROWS, ulen))
        skew = pltpu.roll(per_diag, ulen - (GROUP_ROWS - 1), 1, stride=1, stride_axis=0)
        p, half = divmod(h, HEADS_PER_VREG)
        bias_scr[p, half * GROUP_ROWS:(half + 1) * GROUP_ROWS, :] = jnp.where(
            past_band, NEG, skew[:, :n_var])
    row = lax.broadcasted_iota(jnp.int32, edge_scr.shape, 0) % GROUP_ROWS
    col = lax.broadcasted_iota(jnp.int32, edge_scr.shape, 1)
    edge_scr[...] = jnp.where(col < row // CHUNK * CHUNK, NEG, 0.0)


def _attn_prompt_kernel(q_ref, k_ref, v_ref, x_ref, wo_ref, u_ref, y_ref,
                        kbuf, vbuf, obuf, bias_scr, edge_scr, *, tq):
    t = pl.program_id(1)
    n_pairs = q_ref.shape[1] // LANES

    @pl.when((pl.program_id(0) == 0) & (t == 0))
    def _():
        _band_bias_init(u_ref, bias_scr, edge_scr)

    @pl.when(t == 0)
    def _():
        kbuf[0:tq, :] = jnp.zeros((tq, kbuf.shape[1]), BF16)
        vbuf[0:tq, :] = jnp.zeros((tq, vbuf.shape[1]), BF16)

    @pl.when(t > 0)
    def _():
        kbuf[0:tq, :] = kbuf[tq:2 * tq, :]
        vbuf[0:tq, :] = vbuf[tq:2 * tq, :]

    kbuf[tq:2 * tq, :] = k_ref[...]
    vbuf[tq:2 * tq, :] = v_ref[...]

    def chunk_loop(masked):
        span = ATTN_GROUPS_PER_TRIP * GROUP_ROWS

        def trip_body(i, carry):
            base = pl.multiple_of(i * span, span)

            def item(n):
                g, p = divmod(n, n_pairs)
                return base + g * GROUP_ROWS, slice(p * LANES, (p + 1) * LANES), p

            def scores(n):
                r0, cols, _ = item(n)
                return _pair_scores(q_ref[pl.ds(r0, GROUP_ROWS), cols],
                                    kbuf[pl.ds(r0, GROUP_KEYS), cols])

            def softmax(n, s):
                r0, _, p = item(n)
                s = jnp.concatenate(
                    [s[:, :LANES] + edge_scr[...], s[:, LANES:VAR_COL0],
                     s[:, VAR_COL0:] + bias_scr[p]], axis=1)
                lim = (REACH - (t * tq + r0)) if masked else None
                return _pair_softmax(s, None, lim)

            def output(n, e, inv_l):
                r0, cols, _ = item(n)
                o = _pair_output(e, inv_l, vbuf[pl.ds(r0, GROUP_KEYS), cols])
                obuf[pl.ds(r0, GROUP_ROWS), cols] = o.astype(BF16)

            def done(n):
                if (n + 1) % (PROJ_GROUPS * n_pairs) == 0:
                    rows = PROJ_GROUPS * GROUP_ROWS
                    r0 = pl.multiple_of(base + (n + 1) // n_pairs * GROUP_ROWS - rows, rows)
                    y_ref[pl.ds(r0, rows), :] = x_ref[pl.ds(r0, rows), :] + jnp.dot(
                        obuf[pl.ds(r0, rows), :], wo_ref[...].astype(BF16),
                        preferred_element_type=F32)

            _pipelined_pairs(ATTN_GROUPS_PER_TRIP * n_pairs, scores, softmax, output,
                             ATTN_LEAD_SCORES, ATTN_LEAD_SOFTMAX, done)
            return carry

        lax.fori_loop(0, tq // span, trip_body, 0)

    @pl.when(t == 0)
    def _():
        chunk_loop(True)

    @pl.when(t > 0)
    def _():
        chunk_loop(False)


def _attn_prompt(q, k, v, x, wo, table):
    b, s, d = x.shape
    tq = REACH
    assert s % tq == 0 and tq % (ATTN_GROUPS_PER_TRIP * GROUP_ROWS) == 0
    assert ATTN_GROUPS_PER_TRIP % PROJ_GROUPS == 0
    h = table.shape[0]
    n_diag = GROUP_ROWS - 1 + GROUP_KEYS - VAR_COL0
    diag = jnp.arange(n_diag) + (VAR_COL0 - GROUP_ROWS + 1)
    idx = jnp.clip(REACH - diag, -MAX_REL, MAX_REL) + MAX_REL
    u = (table[:, idx] - table[:, 2 * MAX_REL:]).astype(F32) * LOG2_E
    u = jnp.pad(u, ((0, 0), (0, pl.next_power_of_2(n_diag) - n_diag)))
    row = pl.BlockSpec((None, tq, d), lambda i, j: (i, j, 0))
    rows2 = HEADS_PER_VREG * GROUP_ROWS
    return pl.pallas_call(
        functools.partial(_attn_prompt_kernel, tq=tq),
        grid=(b, s // tq),
        in_specs=[row, row, row, row, _const_spec((d, d)), _const_spec(u.shape)],
        out_specs=row,
        out_shape=jax.ShapeDtypeStruct((b, s, d), F32),
        scratch_shapes=[pltpu.VMEM((2 * tq, d), BF16), pltpu.VMEM((2 * tq, d), BF16),
                        pltpu.VMEM((tq, d), BF16),
                        pltpu.VMEM((h // HEADS_PER_VREG, rows2, GROUP_KEYS - VAR_COL0), F32),
                        pltpu.VMEM((rows2, LANES), F32)],
        compiler_params=_params(2),
        name="attn_prompt",
    )(q, k, v, x, wo, u)


def _attn_sample_kernel(q_ref, k_ref, v_ref, ckt_ref, cvt_ref, x_ref, wo_ref, bias_ref, y_ref,
                        obuf):
    n_pairs = q_ref.shape[1] // LANES
    ca = ckt_ref.shape[2]
    cols = [slice(p * LANES, (p + 1) * LANES) for p in range(n_pairs)]

    def pair_rows(ref, p):
        return jnp.concatenate([ref[HEADS_PER_VREG * p + i] for i in range(HEADS_PER_VREG)],
                               axis=0).astype(BF16)

    def scores(p):
        q_bd = _block_diag_queries(q_ref[:, cols[p]])
        s_cache = jnp.dot(q_bd, pair_rows(ckt_ref, p), preferred_element_type=F32)
        s_new = lax.dot_general(q_bd, k_ref[:, cols[p]], _CONTRACT_LANES,
                                preferred_element_type=F32)
        return jnp.concatenate([s_cache, s_new], axis=1)

    def softmax(p, s):
        return _pair_softmax(s, bias_ref[p], None)

    def output(p, e, inv_l):
        pv = lax.dot_general(e[:, :ca], pair_rows(cvt_ref, p), _CONTRACT_LANES,
                             preferred_element_type=F32)
        pv = pv + jnp.dot(e[:, ca:], v_ref[:, cols[p]], preferred_element_type=F32)
        obuf[:, cols[p]] = _select_heads(pv * inv_l).astype(BF16)

    _pipelined_pairs(n_pairs, scores, softmax, output)
    y_ref[...] = x_ref[...] + jnp.dot(obuf[...], wo_ref[...].astype(BF16),
                                      preferred_element_type=F32)


def _attn_sample(q, k, v, cache_kt, cache_vt, x, wo, bias_pairs):
    b, tn, d = x.shape
    new = pl.BlockSpec((None, tn, d), lambda i: (i, 0, 0))
    old = pl.BlockSpec((None,) + cache_kt.shape[1:], lambda i: (i, 0, 0, 0))
    return pl.pallas_call(
        _attn_sample_kernel,
        grid=(b,),
        in_specs=[new, new, new, old, old, new, _const_spec((d, d)),
                  _const_spec(bias_pairs.shape)],
        out_specs=new,
        out_shape=jax.ShapeDtypeStruct((b, tn, d), F32),
        scratch_shapes=[pltpu.VMEM((tn, d), BF16)],
        compiler_params=_params(1),
        name="attn_sample",
    )(q, k, v, cache_kt, cache_vt, x, wo, bias_pairs)


def _gmlp_kernel(x_ref, g_ref, win_ref, vg_ref, ws_ref, bs_ref, wout_ref, *rest,
                 blk, emit_v):
    if emit_v:
        y_ref, vout_ref, gated = rest
    else:
        y_ref, gated = rest
    rows, d = x_ref.shape
    n_pairs = d // LANES
    x = x_ref[...]
    h = _rms(x, g_ref[...]).astype(BF16)
    def gelu(z):
        return 0.5 * z * (1.0 + lax.erf(z * np.float32(np.sqrt(0.5))))

    v = _rms(gelu(jnp.dot(h, win_ref[:, d:].astype(BF16), preferred_element_type=F32)),
             vg_ref[...])
    u = gelu(jnp.dot(h, win_ref[:, :d].astype(BF16), preferred_element_type=F32))
    if emit_v:
        vout_ref[...] = v
    n_blk = rows // blk
    lane = lax.broadcasted_iota(jnp.int32, (blk, n_blk * LANES), 1) & (LANES - 1)
    zero = jnp.zeros((blk, n_blk * LANES), F32)
    for p in range(n_pairs):
        cols = slice(p * LANES, (p + 1) * LANES)
        vcat = jnp.concatenate([v[r * blk:(r + 1) * blk, cols] for r in range(n_blk)], axis=1)
        v_stack = jnp.concatenate(
            [jnp.where(lane < HEAD_DIM, vcat, zero), jnp.where(lane >= HEAD_DIM, vcat, zero)],
            axis=0).astype(BF16)
        m = jnp.dot(ws_ref[p], v_stack, preferred_element_type=F32)
        for r in range(n_blk):
            rs = slice(r * blk, (r + 1) * blk)
            mixed = m[:, r * LANES:(r + 1) * LANES] + bs_ref[:, cols]
            gated[rs, cols] = (u[rs, cols] * mixed).astype(BF16)
    y_ref[...] = x + jnp.dot(gated[...], wout_ref[...].astype(BF16), preferred_element_type=F32)


def _gmlp(x, gain, w_in, v_gain, ws_pairs, bs_rows, w_out, rows, blk, emit_v):
    b, s, d = x.shape
    w = w_in.shape[1] - d
    assert s % rows == 0 and rows % blk == 0 and w == d
    row = pl.BlockSpec((None, rows, d), lambda i, j: (i, j, 0))
    out_specs = [row]
    out_shape = [jax.ShapeDtypeStruct((b, s, d), F32)]
    if emit_v:
        out_specs.append(row)
        out_shape.append(jax.ShapeDtypeStruct((b, s, w), F32))
    return pl.pallas_call(
        functools.partial(_gmlp_kernel, blk=blk, emit_v=emit_v),
        grid=(b, s // rows),
        in_specs=[row, _const_spec((1, d)), _const_spec(w_in.shape), _const_spec((1, w)),
                  _const_spec(ws_pairs.shape), _const_spec(bs_rows.shape),
                  _const_spec(w_out.shape)],
        out_specs=out_specs,
        out_shape=out_shape,
        scratch_shapes=[pltpu.VMEM((rows, w), BF16)],
        compiler_params=_params(2),
        name="gmlp",
    )(x, gain.reshape(1, d), w_in, v_gain.reshape(1, w), ws_pairs, bs_rows, w_out)


def _conv_taps(stage, n, w, cb):
    c = cb + stage[SUBLANES:SUBLANES + n, :] * w[2:3]
    c = c + stage[SUBLANES - 1:SUBLANES - 1 + n, :] * w[1:2]
    return c + stage[SUBLANES - 2:SUBLANES - 2 + n, :] * w[0:1]


def _ffn_kernel(x_ref, g_ref, wg_ref, wv_ref, cw_ref, cb_ref, wdn_ref, gf_ref,
                y_ref, carry_ref, h_scr, a_scr, act_scr, *, final_norm):
    t = pl.program_id(1)
    rows = x_ref.shape[0]
    ff = wdn_ref.shape[0]
    fc = FF_CHUNK

    @pl.when(t == 0)
    def _():
        carry_ref[...] = jnp.zeros_like(carry_ref)

    x = x_ref[...]
    h_scr[...] = _rms(x, g_ref[...]).astype(BF16)

    for j in range(ff // fc):
        chunk = slice(j * fc, (j + 1) * fc)
        halves = []
        for half, w_ref in enumerate((wg_ref, wv_ref)):
            cols = slice(half * ff + j * fc, half * ff + (j + 1) * fc)
            stage = a_scr.at[2 * (j % 2) + half]
            a = jnp.dot(h_scr[...], w_ref[:, chunk], preferred_element_type=F32)
            stage[0:SUBLANES, :] = carry_ref[:, cols]
            stage[SUBLANES:SUBLANES + rows, :] = a
            carry_ref[:, cols] = a[rows - SUBLANES:rows]
            halves.append(_conv_taps(stage, rows, cw_ref[:, cols], cb_ref[:, cols]))
        act_scr[:, chunk] = (jax.nn.silu(halves[0]) * halves[1]).astype(BF16)

    y = x + jnp.dot(act_scr[...], wdn_ref[...], preferred_element_type=F32)
    if final_norm:
        y = _rms(y, gf_ref[...])
    y_ref[...] = y


def _ffn_stream_kernel(x_ref, g_ref, hg_ref, hv_ref, wg_ref, wv_ref, cwg_ref, cwv_ref, cbg_ref,
                       cbv_ref, wdn_ref, gf_ref, y_ref, cg_ref, cv_ref, wg_out, wv_out, wdn_out,
                       h_scr, acc_scr, a_scr, *, n_sub, final_norm):
    j = pl.program_id(0)
    rows = x_ref.shape[0]
    seq = rows // n_sub

    @pl.when(j == 0)
    def _():
        h_scr[...] = _rms(x_ref[...], g_ref[...]).astype(BF16)
        acc_scr[...] = jnp.zeros_like(acc_scr)

    def branch(slot, w_ref, w_out, hist_ref, carry_out, cw_ref, cb_ref):
        w_bf = w_ref[...].astype(BF16)
        w_out[...] = w_bf
        a = jnp.dot(h_scr[...], w_bf, preferred_element_type=F32)
        outs = []
        for s in range(n_sub):
            stage = a_scr.at[slot, s]
            stage[0:SUBLANES, :] = hist_ref[s]
            stage[SUBLANES:SUBLANES + seq, :] = a[s * seq:(s + 1) * seq]
            carry_out[s] = a[(s + 1) * seq - SUBLANES:(s + 1) * seq]
            outs.append(_conv_taps(stage, seq, cw_ref[...], cb_ref[...]))
        return jnp.concatenate(outs, axis=0)

    gate = branch(0, wg_ref, wg_out, hg_ref, cg_ref, cwg_ref, cbg_ref)
    val = branch(1, wv_ref, wv_out, hv_ref, cv_ref, cwv_ref, cbv_ref)
    wd_bf = wdn_ref[...].astype(BF16)
    wdn_out[...] = wd_bf
    acc_scr[...] += jnp.dot((jax.nn.silu(gate) * val).astype(BF16), wd_bf,
                            preferred_element_type=F32)

    @pl.when(j == pl.num_programs(0) - 1)
    def _():
        y = x_ref[...] + acc_scr[...]
        if final_norm:
            y = _rms(y, gf_ref[...])
        y_ref[...] = y


def _layer_spec(shape, layer):
    nd = len(shape)
    return pl.BlockSpec((None,) + tuple(shape[1:]), lambda *_: (layer,) + (0,) * (nd - 1),
                        pipeline_mode=pl.Buffered(1))


def _conv_ffn(x, gain, wg_bf, wv_bf, conv_w, conv_b, wdn_bf, layer, final_gain, rows):
    b, s, d = x.shape
    ff = wdn_bf.shape[0]
    assert s % rows == 0 and rows % SUBLANES == 0 and ff % FF_CHUNK == 0
    final_norm = final_gain is not None
    gf = (final_gain if final_norm else jnp.ones((d,), F32)).reshape(1, d)
    row = pl.BlockSpec((None, rows, d), lambda i, j: (i, j, 0))
    carry_spec = pl.BlockSpec((None, SUBLANES, 2 * ff), lambda i, j: (i, 0, 0))
    return pl.pallas_call(
        functools.partial(_ffn_kernel, final_norm=final_norm),
        grid=(b, s // rows),
        in_specs=[row, _const_spec((1, d)), _const_spec(wg_bf.shape), _const_spec(wv_bf.shape),
                  _layer_spec(conv_w.shape, layer), _layer_spec(conv_b.shape, layer),
                  _const_spec(wdn_bf.shape), _const_spec((1, d))],
        out_specs=[row, carry_spec],
        out_shape=[jax.ShapeDtypeStruct((b, s, d), F32),
                   jax.ShapeDtypeStruct((b, SUBLANES, 2 * ff), F32)],
        scratch_shapes=[pltpu.VMEM((rows, d), BF16),
                        pltpu.VMEM((4, rows + SUBLANES, FF_CHUNK), F32),
                        pltpu.VMEM((rows, ff), BF16)],
        compiler_params=_params(2),
        name="conv_ffn",
    )(x, gain.reshape(1, d), wg_bf, wv_bf, conv_w, conv_b, wdn_bf, gf)


def _conv_ffn_stream(x, gain, hist, w_up, conv_w, conv_b, w_down, layer, final_gain, n_sub):
    r, d = x.shape
    ff = w_down.shape[1]
    step_bytes_per_col = 2 * 3 * d * (4 + 2)
    fc = max(ff // k for k in range(1, ff // LANES + 1)
             if ff % k == 0 and (ff // k) % LANES == 0
             and (ff // k) * step_bytes_per_col <= VMEM_LIMIT // 2)
    c = ff // fc
    seq = r // n_sub
    assert ff % fc == 0 and fc % LANES == 0 and seq % SUBLANES == 0
    final_norm = final_gain is not None
    gf = (final_gain if final_norm else jnp.ones((d,), F32)).reshape(1, d)

    def cols(blk_rows, half):
        return pl.BlockSpec((None, blk_rows, fc), lambda j: (layer, 0, half * c + j))

    def hist_cols(half):
        return pl.BlockSpec((n_sub, SUBLANES, fc), lambda j: (0, 0, half * c + j))

    half_cols = pl.BlockSpec((n_sub, SUBLANES, fc), lambda j: (0, 0, j))
    up_cols = pl.BlockSpec((d, fc), lambda j: (0, j))
    dn_rows = pl.BlockSpec((fc, d), lambda j: (j, 0))
    return pl.pallas_call(
        functools.partial(_ffn_stream_kernel, n_sub=n_sub, final_norm=final_norm),
        grid=(c,),
        in_specs=[_const_spec((r, d)), _const_spec((1, d)), hist_cols(0), hist_cols(1),
                  cols(d, 0), cols(d, 1), cols(CONV_W, 0), cols(CONV_W, 1), cols(1, 0), cols(1, 1),
                  pl.BlockSpec((None, fc, d), lambda j: (layer, j, 0)),
                  _const_spec((1, d))],
        out_specs=[pl.BlockSpec((r, d), lambda j: (0, 0)), half_cols, half_cols,
                   up_cols, up_cols, dn_rows],
        out_shape=[jax.ShapeDtypeStruct((r, d), F32),
                   jax.ShapeDtypeStruct((n_sub, SUBLANES, ff), F32),
                   jax.ShapeDtypeStruct((n_sub, SUBLANES, ff), F32),
                   jax.ShapeDtypeStruct((d, ff), BF16),
                   jax.ShapeDtypeStruct((d, ff), BF16),
                   jax.ShapeDtypeStruct((ff, d), BF16)],
        scratch_shapes=[pltpu.VMEM((r, d), BF16), pltpu.VMEM((r, d), F32),
                        pltpu.VMEM((2, n_sub, seq + SUBLANES, fc), F32)],
        compiler_params=_params(1),
        name="conv_ffn_stream",
    )(x, gain.reshape(1, d), hist, hist, w_up, w_up, conv_w, conv_w, conv_b, conv_b, w_down, gf)


def _pair_bias(table, n_q, n_k, reach):
    diag = jnp.arange(n_q + n_k - 1) - (n_q - 1)
    idx = jnp.clip(reach - diag, -MAX_REL, MAX_REL) + MAX_REL
    r = table[:, idx].astype(F32) * LOG2_E
    h, m = r.shape
    flat = jnp.tile(jnp.pad(r, ((0, 0), (0, 1))), (1, n_q))[:, :n_q * m]
    bias = flat.reshape(h, n_q, m)[:, :, n_q - 1:n_q - 1 + n_k]
    return bias.reshape(h // HEADS_PER_VREG, HEADS_PER_VREG * n_q, n_k)


def _pad_hist(hist):
    pad = [(0, 0)] * (hist.ndim - 2) + [(SUBLANES - hist.shape[-2], 0), (0, 0)]
    return jnp.pad(hist, pad)


def kernel(x_prompt, x_sample, cache_a_k, cache_a_v, state_ffn_conv, ln_mix, ln_ffn, ln_final,
           a_w_qkv, a_rel_bias, a_w_o, b_w_in, b_v_norm, b_w_s, b_bias_s, b_w_out,
           f_w_up, f_conv_w, f_conv_b, f_w_down):
    bp, sp, d = x_prompt.shape
    bs, ts, _ = x_sample.shape
    depth = ln_mix.shape[0]
    ca = cache_a_k.shape[2]
    keep = min(REACH, sp)
    two_ff = f_w_up.shape[2]
    hist_rows = CONV_W - 1
    conv_b_all = f_conv_b.reshape(depth, 1, two_ff)

    xp = x_prompt
    xs = x_sample.reshape(1, bs * ts, d)
    kp_l, vp_l, ks_l, vs_l, gv_l, cp_l, cs_l = [], [], [], [], [], [], []

    for i in range(depth):
        j = i // 2
        if i % 2 == 0:
            wqkv = a_w_qkv[j]
            wo = a_w_o[j]
            q, k, v, kt, vt = _qkv_proj(xp, ln_mix[i], wqkv, keep, ROW_TILE, True)
            xp = _attn_prompt(q, k, v, xp, wo, a_rel_bias[j])
            kp_l.append(jnp.transpose(kt.reshape(bp, N_HEADS, HEAD_DIM, keep), (0, 3, 1, 2)))
            vp_l.append(jnp.transpose(vt.reshape(bp, N_HEADS, HEAD_DIM, keep), (0, 3, 1, 2)))
            q, k, v, kt, vt = _qkv_proj(xs, ln_mix[i], wqkv, bs * ts, bs * ts, False)
            xs = _attn_sample(
                q.reshape(bs, ts, d), k.reshape(bs, ts, d), v.reshape(bs, ts, d),
                jnp.transpose(cache_a_k[j], (0, 2, 3, 1)),
                jnp.transpose(cache_a_v[j], (0, 2, 3, 1)),
                xs.reshape(bs, ts, d), wo, _pair_bias(a_rel_bias[j], ts, ca + ts, ca),
            ).reshape(1, bs * ts, d)
            ks_l.append(kt.reshape(bs, ts, N_HEADS, HEAD_DIM))
            vs_l.append(vt.reshape(bs, ts, N_HEADS, HEAD_DIM))
        else:
            win = b_w_in[j]
            wout = b_w_out[j]
            pos = jnp.arange(MLP_BLOCK)
            causal = (pos[None, :] // CHUNK) <= (pos[:, None] // CHUNK)
            ws = jnp.where(causal, b_w_s[j], 0.0).astype(BF16)
            g = ws.shape[0]

            def pair_rows(w):
                w = w.reshape(g // HEADS_PER_VREG, HEADS_PER_VREG, w.shape[1], w.shape[2])
                return jnp.transpose(w, (0, 2, 1, 3)).reshape(
                    g // HEADS_PER_VREG, w.shape[2], HEADS_PER_VREG * w.shape[3])

            def bias_rows(bias):
                return jnp.repeat(bias.T.astype(F32), d // g, axis=1)

            xp = _gmlp(xp, ln_mix[i], win, b_v_norm[j], pair_rows(ws), bias_rows(b_bias_s[j]),
                       wout, ROW_TILE, MLP_BLOCK, False)[0]
            xs, gv = _gmlp(xs, ln_mix[i], win, b_v_norm[j], pair_rows(ws[:, :ts, :ts]),
                           bias_rows(b_bias_s[j][:, :ts]), wout, bs * ts, ts, True)
            gv_l.append(gv.reshape(bs, ts, d))

        fin = ln_final if i == depth - 1 else None
        xs2, cs_g, cs_v, wg, wv, wdn = _conv_ffn_stream(
            xs[0], ln_ffn[i], _pad_hist(state_ffn_conv[i]), f_w_up, f_conv_w, conv_b_all,
            f_w_down, i, fin, bs)
        xs = xs2[None]
        xp, cp = _conv_ffn(xp, ln_ffn[i], wg, wv, f_conv_w, conv_b_all, wdn, i, fin,
                           FFN_ROW_TILE)
        cp_l.append(cp[:, SUBLANES - hist_rows:, :])
        cs_l.append(jnp.concatenate([cs_g, cs_v], axis=-1)[:, SUBLANES - hist_rows:, :])

    return (xp, xs.reshape(bs, ts, d), jnp.stack(kp_l), jnp.stack(vp_l), jnp.stack(ks_l),
            jnp.stack(vs_l), jnp.stack(gv_l), jnp.stack(cp_l), jnp.stack(cs_l))
```

```python
import functools

import jax
import jax.numpy as jnp
import numpy as np
from jax import lax
from jax.experimental import pallas as pl
from jax.experimental.pallas import tpu as pltpu

F32 = jnp.float32
BF16 = jnp.bfloat16

CHUNK = 64
PAST_CHUNKS = 8
REACH = PAST_CHUNKS * CHUNK
BAND = REACH + CHUNK
MAX_REL = 128
N_HEADS = 16
HEAD_DIM = 64
MLP_BLOCK = 128
CONV_W = 3
EPS = 1e-6
NEG = -1e30
LOG2_E = float(np.log2(np.e))
Q_SCALE = HEAD_DIM ** -0.5 * LOG2_E

LANES = 128
SUBLANES = 8
MXU_DIM = 256
HEADS_PER_VREG = LANES // HEAD_DIM
VMEM_LIMIT = 56 * 1024 * 1024

ROW_TILE = 1024
FFN_ROW_TILE = 512
FF_CHUNK = MXU_DIM
GROUP_CHUNKS = LANES // CHUNK
GROUP_ROWS = GROUP_CHUNKS * CHUNK
GROUP_KEYS = REACH + GROUP_ROWS
VAR_COL0 = (REACH - MAX_REL) // LANES * LANES
ATTN_GROUPS_PER_TRIP = 4
PROJ_GROUPS = 2
ATTN_LEAD_SCORES = 1
ATTN_LEAD_SOFTMAX = 3


def _rms(x, g):
    return x * lax.rsqrt(jnp.mean(x * x, axis=-1, keepdims=True) + EPS) * g


def _const_spec(shape):
    nd = len(shape)
    return pl.BlockSpec(shape, lambda *_: (0,) * nd, pipeline_mode=pl.Buffered(1))


def _params(n_grid):
    return pltpu.CompilerParams(
        dimension_semantics=("arbitrary",) * n_grid, vmem_limit_bytes=VMEM_LIMIT)


def _qkv_kernel(x_ref, g_ref, w_ref, q_ref, k_ref, v_ref, kt_ref, vt_ref,
                *, d, tail_tile0, tail_rows, time_minor):
    t = pl.program_id(1)
    h = _rms(x_ref[...], g_ref[...]).astype(BF16)
    q = jnp.dot(h, w_ref[:, 0:d].astype(BF16), preferred_element_type=F32)
    q_ref[...] = (q * Q_SCALE).astype(BF16)
    k = jnp.dot(h, w_ref[:, d:2 * d].astype(BF16), preferred_element_type=F32)
    k_ref[...] = k.astype(BF16)
    v = jnp.dot(h, w_ref[:, 2 * d:3 * d].astype(BF16), preferred_element_type=F32)
    v_ref[...] = v.astype(BF16)

    @pl.when(t >= tail_tile0)
    def _():
        kt = k[k.shape[0] - tail_rows:]
        vt = v[v.shape[0] - tail_rows:]
        kt_ref[...] = kt.T if time_minor else kt
        vt_ref[...] = vt.T if time_minor else vt


def _qkv_proj(x, gain, w_qkv, keep, tm, time_minor):
    b, s, d = x.shape
    nt = s // tm
    tail_rows = min(tm, keep)
    assert s % tm == 0 and keep % tail_rows == 0
    tail_tile0 = nt - keep // tail_rows
    row = pl.BlockSpec((None, tm, d), lambda i, j: (i, j, 0))
    if time_minor:
        tail = pl.BlockSpec((None, d, tail_rows),
                            lambda i, j: (i, 0, jnp.maximum(j - tail_tile0, 0)))
        tail_shape = (b, d, keep)
    else:
        tail = pl.BlockSpec((None, tail_rows, d),
                            lambda i, j: (i, jnp.maximum(j - tail_tile0, 0), 0))
        tail_shape = (b, keep, d)
    return pl.pallas_call(
        functools.partial(_qkv_kernel, d=d, tail_tile0=tail_tile0, tail_rows=tail_rows,
                          time_minor=time_minor),
        grid=(b, nt),
        in_specs=[row, _const_spec((1, d)), _const_spec((d, 3 * d))],
        out_specs=[row, row, row, tail, tail],
        out_shape=[jax.ShapeDtypeStruct((b, s, d), BF16)] * 3
        + [jax.ShapeDtypeStruct(tail_shape, F32)] * 2,
        compiler_params=_params(2),
        name="qkv_proj",
    )(x, gain.reshape(1, d), w_qkv)


_CONTRACT_LANES = (((1,), (1,)), ((), ()))


def _block_diag_queries(qp):
    lane = lax.broadcasted_iota(jnp.int32, qp.shape, 1)
    qf = qp.astype(F32)
    zero = jnp.zeros_like(qf)
    return jnp.concatenate(
        [jnp.where(lane < HEAD_DIM, qf, zero), jnp.where(lane >= HEAD_DIM, qf, zero)],
        axis=0).astype(BF16)


def _pair_scores(qp, kb):
    return lax.dot_general(_block_diag_queries(qp), kb, _CONTRACT_LANES,
                           preferred_element_type=F32)


def _select_heads(pv):
    nq = pv.shape[0] // HEADS_PER_VREG
    lane = lax.broadcasted_iota(jnp.int32, (nq, LANES), 1)
    return jnp.where(lane < HEAD_DIM, pv[:nq], pv[nq:])


def _pair_softmax(s, bias, lim):
    if bias is not None:
        s = s + bias
    if lim is not None:
        col = lax.broadcasted_iota(jnp.int32, s.shape, 1)
        s = jnp.where(col >= lim, s, NEG)
    m = jnp.max(s, axis=-1, keepdims=True)
    e = jnp.exp2(s - m)
    return e.astype(BF16), 1.0 / jnp.sum(e, axis=-1, keepdims=True)


def _pair_output(e, inv_l, vb):
    return _select_heads(jnp.dot(e, vb, preferred_element_type=F32) * inv_l)


def _pipelined_pairs(n, scores, softmax, output, lead_scores=1, lead_softmax=1, done=None):
    s, e = {}, {}
    for step in range(n + lead_scores + lead_softmax):
        i_sm = step - lead_scores
        i_out = i_sm - lead_softmax
        if step < n:
            s[step] = scores(step)
        if 0 <= i_sm < n:
            e[i_sm] = softmax(i_sm, s.pop(i_sm))
        if i_out >= 0:
            output(i_out, *e.pop(i_out))
            if done is not None:
                done(i_out)


def _band_bias_init(u_ref, bias_scr, edge_scr):
    n_var = GROUP_KEYS - VAR_COL0
    ulen = u_ref.shape[1]
    row = lax.broadcasted_iota(jnp.int32, (GROUP_ROWS, n_var), 0)
    col = lax.broadcasted_iota(jnp.int32, (GROUP_ROWS, n_var), 1) + VAR_COL0
    past_band = col >= row // CHUNK * CHUNK + BAND
    for h in range(u_ref.shape[0]):
        per_diag = jnp.broadcast_to(u_ref[h:h + 1, :], (GROUP_ROWS, ulen))
        skew = pltpu.roll(per_diag, ulen - (GROUP_ROWS - 1), 1, stride=1, stride_axis=0)
        p, half = divmod(h, HEADS_PER_VREG)
        bias_scr[p, half * GROUP_ROWS:(half + 1) * GROUP_ROWS, :] = jnp.where(
            past_band, NEG, skew[:, :n_var])
    row = lax.broadcasted_iota(jnp.int32, edge_scr.shape, 0) % GROUP_ROWS
    col = lax.broadcasted_iota(jnp.int32, edge_scr.shape, 1)
    edge_scr[...] = jnp.where(col < row // CHUNK * CHUNK, NEG, 0.0)


def _attn_prompt_kernel(q_ref, k_ref, v_ref, x_ref, wo_ref, u_ref, y_ref,
                        kbuf, vbuf, obuf, bias_scr, edge_scr, *, tq):
    t = pl.program_id(1)
    n_pairs = q_ref.shape[1] // LANES

    @pl.when((pl.program_id(0) == 0) & (t == 0))
    def _():
        _band_bias_init(u_ref, bias_scr, edge_scr)

    @pl.when(t == 0)
    def _():
        kbuf[0:tq, :] = jnp.zeros((tq, kbuf.shape[1]), BF16)
        vbuf[0:tq, :] = jnp.zeros((tq, vbuf.shape[1]), BF16)

    @pl.when(t > 0)
    def _():
        kbuf[0:tq, :] = kbuf[tq:2 * tq, :]
        vbuf[0:tq, :] = vbuf[tq:2 * tq, :]

    kbuf[tq:2 * tq, :] = k_ref[...]
    vbuf[tq:2 * tq, :] = v_ref[...]

    def chunk_loop(masked):
        span = ATTN_GROUPS_PER_TRIP * GROUP_ROWS

        def trip_body(i, carry):
            base = pl.multiple_of(i * span, span)

            def item(n):
                g, p = divmod(n, n_pairs)
                return base + g * GROUP_ROWS, slice(p * LANES, (p + 1) * LANES), p

            def scores(n):
                r0, cols, _ = item(n)
                return _pair_scores(q_ref[pl.ds(r0, GROUP_ROWS), cols],
                                    kbuf[pl.ds(r0, GROUP_KEYS), cols])

            def softmax(n, s):
                r0, _, p = item(n)
                s = jnp.concatenate(
                    [s[:, :LANES] + edge_scr[...], s[:, LANES:VAR_COL0],
                     s[:, VAR_COL0:] + bias_scr[p]], axis=1)
                lim = (REACH - (t * tq + r0)) if masked else None
                return _pair_softmax(s, None, lim)

            def output(n, e, inv_l):
                r0, cols, _ = item(n)
                o = _pair_output(e, inv_l, vbuf[pl.ds(r0, GROUP_KEYS), cols])
                obuf[pl.ds(r0, GROUP_ROWS), cols] = o.astype(BF16)

            def done(n):
                if (n + 1) % (PROJ_GROUPS * n_pairs) == 0:
                    rows = PROJ_GROUPS * GROUP_ROWS
                    r0 = pl.multiple_of(base + (n + 1) // n_pairs * GROUP_ROWS - rows, rows)
                    y_ref[pl.ds(r0, rows), :] = x_ref[pl.ds(r0, rows), :] + jnp.dot(
                        obuf[pl.ds(r0, rows), :], wo_ref[...].astype(BF16),
                        preferred_element_type=F32)

            _pipelined_pairs(ATTN_GROUPS_PER_TRIP * n_pairs, scores, softmax, output,
                             ATTN_LEAD_SCORES, ATTN_LEAD_SOFTMAX, done)
            return carry

        lax.fori_loop(0, tq // span, trip_body, 0)

    @pl.when(t == 0)
    def _():
        chunk_loop(True)

    @pl.when(t > 0)
    def _():
        chunk_loop(False)


def _attn_prompt(q, k, v, x, wo, table):
    b, s, d = x.shape
    tq = REACH
    assert s % tq == 0 and tq % (ATTN_GROUPS_PER_TRIP * GROUP_ROWS) == 0
    assert ATTN_GROUPS_PER_TRIP % PROJ_GROUPS == 0
    h = table.shape[0]
    n_diag = GROUP_ROWS - 1 + GROUP_KEYS - VAR_COL0
    diag = jnp.arange(n_diag) + (VAR_COL0 - GROUP_ROWS + 1)
    idx = jnp.clip(REACH - diag, -MAX_REL, MAX_REL) + MAX_REL
    u = (table[:, idx] - table[:, 2 * MAX_REL:]).astype(F32) * LOG2_E
    u = jnp.pad(u, ((0, 0), (0, pl.next_power_of_2(n_diag) - n_diag)))
    row = pl.BlockSpec((None, tq, d), lambda i, j: (i, j, 0))
    rows2 = HEADS_PER_VREG * GROUP_ROWS
    return pl.pallas_call(
        functools.partial(_attn_prompt_kernel, tq=tq),
        grid=(b, s // tq),
        in_specs=[row, row, row, row, _const_spec((d, d)), _const_spec(u.shape)],
        out_specs=row,
        out_shape=jax.ShapeDtypeStruct((b, s, d), F32),
        scratch_shapes=[pltpu.VMEM((2 * tq, d), BF16), pltpu.VMEM((2 * tq, d), BF16),
                        pltpu.VMEM((tq, d), BF16),
                        pltpu.VMEM((h // HEADS_PER_VREG, rows2, GROUP_KEYS - VAR_COL0), F32),
                        pltpu.VMEM((rows2, LANES), F32)],
        compiler_params=_params(2),
        name="attn_prompt",
    )(q, k, v, x, wo, u)


def _attn_sample_kernel(q_ref, k_ref, v_ref, ckt_ref, cvt_ref, x_ref, wo_ref, bias_ref, y_ref,
                        obuf):
    n_pairs = q_ref.shape[1] // LANES
    ca = ckt_ref.shape[2]
    cols = [slice(p * LANES, (p + 1) * LANES) for p in range(n_pairs)]

    def pair_rows(ref, p):
        return jnp.concatenate([ref[HEADS_PER_VREG * p + i] for i in range(HEADS_PER_VREG)],
                               axis=0).astype(BF16)

    def scores(p):
        q_bd = _block_diag_queries(q_ref[:, cols[p]])
        s_cache = jnp.dot(q_bd, pair_rows(ckt_ref, p), preferred_element_type=F32)
        s_new = lax.dot_general(q_bd, k_ref[:, cols[p]], _CONTRACT_LANES,
                                preferred_element_type=F32)
        return jnp.concatenate([s_cache, s_new], axis=1)

    def softmax(p, s):
        return _pair_softmax(s, bias_ref[p], None)

    def output(p, e, inv_l):
        pv = lax.dot_general(e[:, :ca], pair_rows(cvt_ref, p), _CONTRACT_LANES,
                             preferred_element_type=F32)
        pv = pv + jnp.dot(e[:, ca:], v_ref[:, cols[p]], preferred_element_type=F32)
        obuf[:, cols[p]] = _select_heads(pv * inv_l).astype(BF16)

    _pipelined_pairs(n_pairs, scores, softmax, output)
    y_ref[...] = x_ref[...] + jnp.dot(obuf[...], wo_ref[...].astype(BF16),
                                      preferred_element_type=F32)


def _attn_sample(q, k, v, cache_kt, cache_vt, x, wo, bias_pairs):
    b, tn, d = x.shape
    new = pl.BlockSpec((None, tn, d), lambda i: (i, 0, 0))
    old = pl.BlockSpec((None,) + cache_kt.shape[1:], lambda i: (i, 0, 0, 0))
    return pl.pallas_call(
        _attn_sample_kernel,
        grid=(b,),
        in_specs=[new, new, new, old, old, new, _const_spec((d, d)),
                  _const_spec(bias_pairs.shape)],
        out_specs=new,
        out_shape=jax.ShapeDtypeStruct((b, tn, d), F32),
        scratch_shapes=[pltpu.VMEM((tn, d), BF16)],
        compiler_params=_params(1),
        name="attn_sample",
    )(q, k, v, cache_kt, cache_vt, x, wo, bias_pairs)


def _gmlp_kernel(x_ref, g_ref, win_ref, vg_ref, ws_ref, bs_ref, wout_ref, *rest,
                 blk, emit_v):
    if emit_v:
        y_ref, vout_ref, gated = rest
    else:
        y_ref, gated = rest
    rows, d = x_ref.shape
    n_pairs = d // LANES
    x = x_ref[...]
    h = _rms(x, g_ref[...]).astype(BF16)
    def gelu(z):
        return 0.5 * z * (1.0 + lax.erf(z * np.float32(np.sqrt(0.5))))

    v = _rms(gelu(jnp.dot(h, win_ref[:, d:].astype(BF16), preferred_element_type=F32)),
             vg_ref[...])
    u = gelu(jnp.dot(h, win_ref[:, :d].astype(BF16), preferred_element_type=F32))
    if emit_v:
        vout_ref[...] = v
    n_blk = rows // blk
    lane = lax.broadcasted_iota(jnp.int32, (blk, n_blk * LANES), 1) & (LANES - 1)
    zero = jnp.zeros((blk, n_blk * LANES), F32)
    for p in range(n_pairs):
        cols = slice(p * LANES, (p + 1) * LANES)
        vcat = jnp.concatenate([v[r * blk:(r + 1) * blk, cols] for r in range(n_blk)], axis=1)
        v_stack = jnp.concatenate(
            [jnp.where(lane < HEAD_DIM, vcat, zero), jnp.where(lane >= HEAD_DIM, vcat, zero)],
            axis=0).astype(BF16)
        m = jnp.dot(ws_ref[p], v_stack, preferred_element_type=F32)
        for r in range(n_blk):
            rs = slice(r * blk, (r + 1) * blk)
            mixed = m[:, r * LANES:(r + 1) * LANES] + bs_ref[:, cols]
            gated[rs, cols] = (u[rs, cols] * mixed).astype(BF16)
    y_ref[...] = x + jnp.dot(gated[...], wout_ref[...].astype(BF16), preferred_element_type=F32)


def _gmlp(x, gain, w_in, v_gain, ws_pairs, bs_rows, w_out, rows, blk, emit_v):
    b, s, d = x.shape
    w = w_in.shape[1] - d
    assert s % rows == 0 and rows % blk == 0 and w == d
    row = pl.BlockSpec((None, rows, d), lambda i, j: (i, j, 0))
    out_specs = [row]
    out_shape = [jax.ShapeDtypeStruct((b, s, d), F32)]
    if emit_v:
        out_specs.append(row)
        out_shape.append(jax.ShapeDtypeStruct((b, s, w), F32))
    return pl.pallas_call(
        functools.partial(_gmlp_kernel, blk=blk, emit_v=emit_v),
        grid=(b, s // rows),
        in_specs=[row, _const_spec((1, d)), _const_spec(w_in.shape), _const_spec((1, w)),
                  _const_spec(ws_pairs.shape), _const_spec(bs_rows.shape),
                  _const_spec(w_out.shape)],
        out_specs=out_specs,
        out_shape=out_shape,
        scratch_shapes=[pltpu.VMEM((rows, w), BF16)],
        compiler_params=_params(2),
        name="gmlp",
    )(x, gain.reshape(1, d), w_in, v_gain.reshape(1, w), ws_pairs, bs_rows, w_out)


def _conv_taps(stage, n, w, cb):
    c = cb + stage[SUBLANES:SUBLANES + n, :] * w[2:3]
    c = c + stage[SUBLANES - 1:SUBLANES - 1 + n, :] * w[1:2]
    return c + stage[SUBLANES - 2:SUBLANES - 2 + n, :] * w[0:1]


def _ffn_kernel(x_ref, g_ref, wg_ref, wv_ref, cw_ref, cb_ref, wdn_ref, gf_ref,
                y_ref, carry_ref, h_scr, a_scr, act_scr, *, final_norm):
    t = pl.program_id(1)
    rows = x_ref.shape[0]
    ff = wdn_ref.shape[0]
    fc = FF_CHUNK

    @pl.when(t == 0)
    def _():
        carry_ref[...] = jnp.zeros_like(carry_ref)

    x = x_ref[...]
    h_scr[...] = _rms(x, g_ref[...]).astype(BF16)

    for j in range(ff // fc):
        chunk = slice(j * fc, (j + 1) * fc)
        halves = []
        for half, w_ref in enumerate((wg_ref, wv_ref)):
            cols = slice(half * ff + j * fc, half * ff + (j + 1) * fc)
            stage = a_scr.at[2 * (j % 2) + half]
            a = jnp.dot(h_scr[...], w_ref[j], preferred_element_type=F32)
            stage[0:SUBLANES, :] = carry_ref[:, cols]
            stage[SUBLANES:SUBLANES + rows, :] = a
            carry_ref[:, cols] = a[rows - SUBLANES:rows]
            halves.append(_conv_taps(stage, rows, cw_ref[:, cols], cb_ref[:, cols]))
        act_scr[:, chunk] = (jax.nn.silu(halves[0]) * halves[1]).astype(BF16)

    y = x + jnp.dot(act_scr[...], wdn_ref[...], preferred_element_type=F32)
    if final_norm:
        y = _rms(y, gf_ref[...])
    y_ref[...] = y


def _ffn_stream_kernel(x_ref, g_ref, hg_ref, hv_ref, wg_ref, wv_ref, cwg_ref, cwv_ref, cbg_ref,
                       cbv_ref, wdn_ref, gf_ref, y_ref, cg_ref, cv_ref, wg_out, wv_out, wdn_out,
                       h_scr, acc_scr, a_scr, *, n_sub, final_norm):
    j = pl.program_id(0)
    rows = x_ref.shape[0]
    seq = rows // n_sub

    @pl.when(j == 0)
    def _():
        h_scr[...] = _rms(x_ref[...], g_ref[...]).astype(BF16)
        acc_scr[...] = jnp.zeros_like(acc_scr)

    def branch(slot, w_ref, w_out, hist_ref, carry_out, cw_ref, cb_ref):
        w_bf = w_ref[...].astype(BF16)
        w_out[...] = w_bf
        a = jnp.dot(h_scr[...], w_bf, preferred_element_type=F32)
        outs = []
        for s in range(n_sub):
            stage = a_scr.at[slot, s]
            stage[0:SUBLANES, :] = hist_ref[s]
            stage[SUBLANES:SUBLANES + seq, :] = a[s * seq:(s + 1) * seq]
            carry_out[s] = a[(s + 1) * seq - SUBLANES:(s + 1) * seq]
            outs.append(_conv_taps(stage, seq, cw_ref[...], cb_ref[...]))
        return jnp.concatenate(outs, axis=0)

    gate = branch(0, wg_ref, wg_out, hg_ref, cg_ref, cwg_ref, cbg_ref)
    val = branch(1, wv_ref, wv_out, hv_ref, cv_ref, cwv_ref, cbv_ref)
    wd_bf = wdn_ref[...].astype(BF16)
    wdn_out[...] = wd_bf
    acc_scr[...] += jnp.dot((jax.nn.silu(gate) * val).astype(BF16), wd_bf,
                            preferred_element_type=F32)

    @pl.when(j == pl.num_programs(0) - 1)
    def _():
        y = x_ref[...] + acc_scr[...]
        if final_norm:
            y = _rms(y, gf_ref[...])
        y_ref[...] = y


def _layer_spec(shape, layer):
    nd = len(shape)
    return pl.BlockSpec((None,) + tuple(shape[1:]), lambda *_: (layer,) + (0,) * (nd - 1),
                        pipeline_mode=pl.Buffered(1))


def _conv_ffn(x, gain, wg_bf, wv_bf, conv_w, conv_b, wdn_bf, layer, final_gain, rows):
    b, s, d = x.shape
    ff = wdn_bf.shape[0]
    assert s % rows == 0 and rows % SUBLANES == 0
    assert wg_bf.shape == wv_bf.shape == (ff // FF_CHUNK, d, FF_CHUNK)
    final_norm = final_gain is not None
    gf = (final_gain if final_norm else jnp.ones((d,), F32)).reshape(1, d)
    row = pl.BlockSpec((None, rows, d), lambda i, j: (i, j, 0))
    carry_spec = pl.BlockSpec((None, SUBLANES, 2 * ff), lambda i, j: (i, 0, 0))
    return pl.pallas_call(
        functools.partial(_ffn_kernel, final_norm=final_norm),
        grid=(b, s // rows),
        in_specs=[row, _const_spec((1, d)), _const_spec(wg_bf.shape), _const_spec(wv_bf.shape),
                  _layer_spec(conv_w.shape, layer), _layer_spec(conv_b.shape, layer),
                  _const_spec(wdn_bf.shape), _const_spec((1, d))],
        out_specs=[row, carry_spec],
        out_shape=[jax.ShapeDtypeStruct((b, s, d), F32),
                   jax.ShapeDtypeStruct((b, SUBLANES, 2 * ff), F32)],
        scratch_shapes=[pltpu.VMEM((rows, d), BF16),
                        pltpu.VMEM((4, rows + SUBLANES, FF_CHUNK), F32),
                        pltpu.VMEM((rows, ff), BF16)],
        compiler_params=_params(2),
        name="conv_ffn",
    )(x, gain.reshape(1, d), wg_bf, wv_bf, conv_w, conv_b, wdn_bf, gf)


def _conv_ffn_stream(x, gain, hist, w_up, conv_w, conv_b, w_down, layer, final_gain, n_sub):
    r, d = x.shape
    ff = w_down.shape[1]
    fc = FF_CHUNK
    c = ff // fc
    seq = r // n_sub
    assert ff % fc == 0 and seq % SUBLANES == 0
    final_norm = final_gain is not None
    gf = (final_gain if final_norm else jnp.ones((d,), F32)).reshape(1, d)

    def cols(blk_rows, half):
        return pl.BlockSpec((None, blk_rows, fc), lambda j: (layer, 0, half * c + j))

    def hist_cols(half):
        return pl.BlockSpec((n_sub, SUBLANES, fc), lambda j: (0, 0, half * c + j))

    half_cols = pl.BlockSpec((n_sub, SUBLANES, fc), lambda j: (0, 0, j))
    up_cols = pl.BlockSpec((None, d, fc), lambda j: (j, 0, 0))
    dn_rows = pl.BlockSpec((fc, d), lambda j: (j, 0))
    return pl.pallas_call(
        functools.partial(_ffn_stream_kernel, n_sub=n_sub, final_norm=final_norm),
        grid=(c,),
        in_specs=[_const_spec((r, d)), _const_spec((1, d)), hist_cols(0), hist_cols(1),
                  cols(d, 0), cols(d, 1), cols(CONV_W, 0), cols(CONV_W, 1), cols(1, 0), cols(1, 1),
                  pl.BlockSpec((None, fc, d), lambda j: (layer, j, 0)),
                  _const_spec((1, d))],
        out_specs=[pl.BlockSpec((r, d), lambda j: (0, 0)), half_cols, half_cols,
                   up_cols, up_cols, dn_rows],
        out_shape=[jax.ShapeDtypeStruct((r, d), F32),
                   jax.ShapeDtypeStruct((n_sub, SUBLANES, ff), F32),
                   jax.ShapeDtypeStruct((n_sub, SUBLANES, ff), F32),
                   jax.ShapeDtypeStruct((c, d, fc), BF16),
                   jax.ShapeDtypeStruct((c, d, fc), BF16),
                   jax.ShapeDtypeStruct((ff, d), BF16)],
        scratch_shapes=[pltpu.VMEM((r, d), BF16), pltpu.VMEM((r, d), F32),
                        pltpu.VMEM((2, n_sub, seq + SUBLANES, fc), F32)],
        compiler_params=_params(1),
        name="conv_ffn_stream",
    )(x, gain.reshape(1, d), hist, hist, w_up, w_up, conv_w, conv_w, conv_b, conv_b, w_down, gf)


def _pair_bias(table, n_q, n_k, reach):
    diag = jnp.arange(n_q + n_k - 1) - (n_q - 1)
    idx = jnp.clip(reach - diag, -MAX_REL, MAX_REL) + MAX_REL
    r = table[:, idx].astype(F32) * LOG2_E
    h, m = r.shape
    flat = jnp.tile(jnp.pad(r, ((0, 0), (0, 1))), (1, n_q))[:, :n_q * m]
    bias = flat.reshape(h, n_q, m)[:, :, n_q - 1:n_q - 1 + n_k]
    return bias.reshape(h // HEADS_PER_VREG, HEADS_PER_VREG * n_q, n_k)


def _pad_hist(hist):
    pad = [(0, 0)] * (hist.ndim - 2) + [(SUBLANES - hist.shape[-2], 0), (0, 0)]
    return jnp.pad(hist, pad)


def kernel(x_prompt, x_sample, cache_a_k, cache_a_v, state_ffn_conv, ln_mix, ln_ffn, ln_final,
           a_w_qkv, a_rel_bias, a_w_o, b_w_in, b_v_norm, b_w_s, b_bias_s, b_w_out,
           f_w_up, f_conv_w, f_conv_b, f_w_down):
    bp, sp, d = x_prompt.shape
    bs, ts, _ = x_sample.shape
    depth = ln_mix.shape[0]
    ca = cache_a_k.shape[2]
    keep = min(REACH, sp)
    two_ff = f_w_up.shape[2]
    hist_rows = CONV_W - 1
    conv_b_all = f_conv_b.reshape(depth, 1, two_ff)

    xp = x_prompt
    xs = x_sample.reshape(1, bs * ts, d)
    kp_l, vp_l, ks_l, vs_l, gv_l, cp_l, cs_l = [], [], [], [], [], [], []

    for i in range(depth):
        j = i // 2
        if i % 2 == 0:
            wqkv = a_w_qkv[j]
            wo = a_w_o[j]
            q, k, v, kt, vt = _qkv_proj(xp, ln_mix[i], wqkv, keep, ROW_TILE, True)
            xp = _attn_prompt(q, k, v, xp, wo, a_rel_bias[j])
            kp_l.append(jnp.transpose(kt.reshape(bp, N_HEADS, HEAD_DIM, keep), (0, 3, 1, 2)))
            vp_l.append(jnp.transpose(vt.reshape(bp, N_HEADS, HEAD_DIM, keep), (0, 3, 1, 2)))
            q, k, v, kt, vt = _qkv_proj(xs, ln_mix[i], wqkv, bs * ts, bs * ts, False)
            xs = _attn_sample(
                q.reshape(bs, ts, d), k.reshape(bs, ts, d), v.reshape(bs, ts, d),
                jnp.transpose(cache_a_k[j], (0, 2, 3, 1)),
                jnp.transpose(cache_a_v[j], (0, 2, 3, 1)),
                xs.reshape(bs, ts, d), wo, _pair_bias(a_rel_bias[j], ts, ca + ts, ca),
            ).reshape(1, bs * ts, d)
            ks_l.append(kt.reshape(bs, ts, N_HEADS, HEAD_DIM))
            vs_l.append(vt.reshape(bs, ts, N_HEADS, HEAD_DIM))
        else:
            win = b_w_in[j]
            wout = b_w_out[j]
            pos = jnp.arange(MLP_BLOCK)
            causal = (pos[None, :] // CHUNK) <= (pos[:, None] // CHUNK)
            ws = jnp.where(causal, b_w_s[j], 0.0).astype(BF16)
            g = ws.shape[0]

            def pair_rows(w):
                w = w.reshape(g // HEADS_PER_VREG, HEADS_PER_VREG, w.shape[1], w.shape[2])
                return jnp.transpose(w, (0, 2, 1, 3)).reshape(
                    g // HEADS_PER_VREG, w.shape[2], HEADS_PER_VREG * w.shape[3])

            def bias_rows(bias):
                return jnp.repeat(bias.T.astype(F32), d // g, axis=1)

            xp = _gmlp(xp, ln_mix[i], win, b_v_norm[j], pair_rows(ws), bias_rows(b_bias_s[j]),
                       wout, ROW_TILE, MLP_BLOCK, False)[0]
            xs, gv = _gmlp(xs, ln_mix[i], win, b_v_norm[j], pair_rows(ws[:, :ts, :ts]),
                           bias_rows(b_bias_s[j][:, :ts]), wout, bs * ts, ts, True)
            gv_l.append(gv.reshape(bs, ts, d))

        fin = ln_final if i == depth - 1 else None
        xs2, cs_g, cs_v, wg, wv, wdn = _conv_ffn_stream(
            xs[0], ln_ffn[i], _pad_hist(state_ffn_conv[i]), f_w_up, f_conv_w, conv_b_all,
            f_w_down, i, fin, bs)
        xs = xs2[None]
        xp, cp = _conv_ffn(xp, ln_ffn[i], wg, wv, f_conv_w, conv_b_all, wdn, i, fin,
                           FFN_ROW_TILE)
        cp_l.append(cp[:, SUBLANES - hist_rows:, :])
        cs_l.append(jnp.concatenate([cs_g, cs_v], axis=-1)[:, SUBLANES - hist_rows:, :])

    return (xp, xs.reshape(bs, ts, d), jnp.stack(kp_l), jnp.stack(vp_l), jnp.stack(ks_l),
            jnp.stack(vs_l), jnp.stack(gv_l), jnp.stack(cp_l), jnp.stack(cs_l))
```

```python
import functools

import jax
import jax.numpy as jnp
import numpy as np
from jax import lax
from jax.experimental import pallas as pl
from jax.experimental.pallas import tpu as pltpu

F32 = jnp.float32
BF16 = jnp.bfloat16

CHUNK = 64
PAST_CHUNKS = 8
REACH = PAST_CHUNKS * CHUNK
BAND = REACH + CHUNK
MAX_REL = 128
N_HEADS = 16
HEAD_DIM = 64
MLP_BLOCK = 128
CONV_W = 3
EPS = 1e-6
NEG = -1e30
LOG2_E = float(np.log2(np.e))
Q_SCALE = HEAD_DIM ** -0.5 * LOG2_E

LANES = 128
SUBLANES = 8
MXU_DIM = 256
HEADS_PER_VREG = LANES // HEAD_DIM
VMEM_LIMIT = 56 * 1024 * 1024

ROW_TILE = 1024
FFN_ROW_TILE = 512
FF_CHUNK = MXU_DIM
GROUP_CHUNKS = LANES // CHUNK
GROUP_ROWS = GROUP_CHUNKS * CHUNK
GROUP_KEYS = REACH + GROUP_ROWS
VAR_COL0 = (REACH - MAX_REL) // LANES * LANES
ATTN_GROUPS_PER_TRIP = 4
PROJ_GROUPS = 2
ATTN_LEAD_SCORES = 1
ATTN_LEAD_SOFTMAX = 3


def _rms(x, g):
    return x * lax.rsqrt(jnp.mean(x * x, axis=-1, keepdims=True) + EPS) * g


def _const_spec(shape):
    nd = len(shape)
    return pl.BlockSpec(shape, lambda *_: (0,) * nd, pipeline_mode=pl.Buffered(1))


def _params(n_grid):
    return pltpu.CompilerParams(
        dimension_semantics=("arbitrary",) * n_grid, vmem_limit_bytes=VMEM_LIMIT)


def _qkv_kernel(x_ref, xs_ref, g_ref, w_ref, q_ref, k_ref, v_ref, kt_ref, vt_ref,
                qs_ref, ks_ref, vs_ref, kst_ref, vst_ref, *, d, nt, n_tiles, tail_tile0):
    i = pl.program_id(0)

    def project(x):
        h = _rms(x, g_ref[...]).astype(BF16)
        q = jnp.dot(h, w_ref[:, 0:d].astype(BF16), preferred_element_type=F32)
        k = jnp.dot(h, w_ref[:, d:2 * d].astype(BF16), preferred_element_type=F32)
        v = jnp.dot(h, w_ref[:, 2 * d:3 * d].astype(BF16), preferred_element_type=F32)
        return (q * Q_SCALE).astype(BF16), k, v

    @pl.when(i < n_tiles)
    def _():
        q, k, v = project(x_ref[...])
        q_ref[...] = q
        k_ref[...] = k.astype(BF16)
        v_ref[...] = v.astype(BF16)

        @pl.when(i % nt >= tail_tile0)
        def _():
            tail_rows = kt_ref.shape[1]
            kt_ref[...] = k[k.shape[0] - tail_rows:].T
            vt_ref[...] = v[v.shape[0] - tail_rows:].T

    @pl.when(i == n_tiles)
    def _():
        q, k, v = project(xs_ref[...])
        qs_ref[...] = q
        ks_ref[...] = k.astype(BF16)
        vs_ref[...] = v.astype(BF16)
        kst_ref[...] = k
        vst_ref[...] = v


def _qkv_proj(x, xs, gain, w_qkv, keep, tm):
    b, s, d = x.shape
    r = xs.shape[0]
    nt = s // tm
    n_tiles = b * nt
    tail_rows = min(tm, keep)
    assert s % tm == 0 and keep % tail_rows == 0
    tail_tile0 = nt - keep // tail_rows

    def tile(i):
        i = jnp.minimum(i, n_tiles - 1)
        return i // nt, i % nt

    row = pl.BlockSpec((None, tm, d), lambda i: (*tile(i), 0))
    tail = pl.BlockSpec((None, d, tail_rows),
                        lambda i: (tile(i)[0], 0, jnp.maximum(tile(i)[1] - tail_tile0, 0)))
    few = pl.BlockSpec((r, d), lambda i: (0, 0))
    return pl.pallas_call(
        functools.partial(_qkv_kernel, d=d, nt=nt, n_tiles=n_tiles, tail_tile0=tail_tile0),
        grid=(n_tiles + 1,),
        in_specs=[row, _const_spec((r, d)), _const_spec((1, d)), _const_spec((d, 3 * d))],
        out_specs=[row, row, row, tail, tail, few, few, few, few, few],
        out_shape=[jax.ShapeDtypeStruct((b, s, d), BF16)] * 3
        + [jax.ShapeDtypeStruct((b, d, keep), F32)] * 2
        + [jax.ShapeDtypeStruct((r, d), BF16)] * 3 + [jax.ShapeDtypeStruct((r, d), F32)] * 2,
        compiler_params=_params(1),
        name="qkv_proj",
    )(x, xs, gain.reshape(1, d), w_qkv)


_CONTRACT_LANES = (((1,), (1,)), ((), ()))


def _block_diag_queries(qp):
    lane = lax.broadcasted_iota(jnp.int32, qp.shape, 1)
    qf = qp.astype(F32)
    zero = jnp.zeros_like(qf)
    return jnp.concatenate(
        [jnp.where(lane < HEAD_DIM, qf, zero), jnp.where(lane >= HEAD_DIM, qf, zero)],
        axis=0).astype(BF16)


def _pair_scores(qp, kb):
    return lax.dot_general(_block_diag_queries(qp), kb, _CONTRACT_LANES,
                           preferred_element_type=F32)


def _select_heads(pv):
    nq = pv.shape[0] // HEADS_PER_VREG
    lane = lax.broadcasted_iota(jnp.int32, (nq, LANES), 1)
    return jnp.where(lane < HEAD_DIM, pv[:nq], pv[nq:])


def _pair_softmax(s, bias, lim):
    if bias is not None:
        s = s + bias
    if lim is not None:
        col = lax.broadcasted_iota(jnp.int32, s.shape, 1)
        s = jnp.where(col >= lim, s, NEG)
    m = jnp.max(s, axis=-1, keepdims=True)
    e = jnp.exp2(s - m)
    return e.astype(BF16), 1.0 / jnp.sum(e, axis=-1, keepdims=True)


def _pair_output(e, inv_l, vb):
    return _select_heads(jnp.dot(e, vb, preferred_element_type=F32) * inv_l)


def _pipelined_pairs(n, scores, softmax, output, lead_scores=1, lead_softmax=1, done=None):
    s, e = {}, {}
    for step in range(n + lead_scores + lead_softmax):
        i_sm = step - lead_scores
        i_out = i_sm - lead_softmax
        if step < n:
            s[step] = scores(step)
        if 0 <= i_sm < n:
            e[i_sm] = softmax(i_sm, s.pop(i_sm))
        if i_out >= 0:
            output(i_out, *e.pop(i_out))
            if done is not None:
                done(i_out)


def _band_bias_init(u_ref, bias_scr, edge_scr):
    n_var = GROUP_KEYS - VAR_COL0
    ulen = u_ref.shape[1]
    row = lax.broadcasted_iota(jnp.int32, (GROUP_ROWS, n_var), 0)
    col = lax.broadcasted_iota(jnp.int32, (GROUP_ROWS, n_var), 1) + VAR_COL0
    past_band = col >= row // CHUNK * CHUNK + BAND
    for h in range(u_ref.shape[0]):
        per_diag = jnp.broadcast_to(u_ref[h:h + 1, :], (GROUP_ROWS, ulen))
        skew = pltpu.roll(per_diag, ulen - (GROUP_ROWS - 1), 1, stride=1, stride_axis=0)
        p, half = divmod(h, HEADS_PER_VREG)
        bias_scr[p, half * GROUP_ROWS:(half + 1) * GROUP_ROWS, :] = jnp.where(
            past_band, NEG, skew[:, :n_var])
    row = lax.broadcasted_iota(jnp.int32, edge_scr.shape, 0) % GROUP_ROWS
    col = lax.broadcasted_iota(jnp.int32, edge_scr.shape, 1)
    edge_scr[...] = jnp.where(col < row // CHUNK * CHUNK, NEG, 0.0)


def _attn_prompt_kernel(q_ref, k_ref, v_ref, x_ref, wo_ref, u_ref, y_ref,
                        kbuf, vbuf, obuf, bias_scr, edge_scr, *, tq):
    t = pl.program_id(1)
    n_pairs = q_ref.shape[1] // LANES

    @pl.when((pl.program_id(0) == 0) & (t == 0))
    def _():
        _band_bias_init(u_ref, bias_scr, edge_scr)

    @pl.when(t == 0)
    def _():
        kbuf[0:tq, :] = jnp.zeros((tq, kbuf.shape[1]), BF16)
        vbuf[0:tq, :] = jnp.zeros((tq, vbuf.shape[1]), BF16)

    @pl.when(t > 0)
    def _():
        kbuf[0:tq, :] = kbuf[tq:2 * tq, :]
        vbuf[0:tq, :] = vbuf[tq:2 * tq, :]

    kbuf[tq:2 * tq, :] = k_ref[...]
    vbuf[tq:2 * tq, :] = v_ref[...]

    def chunk_loop(masked):
        span = ATTN_GROUPS_PER_TRIP * GROUP_ROWS

        def trip_body(i, carry):
            base = pl.multiple_of(i * span, span)

            def item(n):
                g, p = divmod(n, n_pairs)
                return base + g * GROUP_ROWS, slice(p * LANES, (p + 1) * LANES), p

            def scores(n):
                r0, cols, _ = item(n)
                return _pair_scores(q_ref[pl.ds(r0, GROUP_ROWS), cols],
                                    kbuf[pl.ds(r0, GROUP_KEYS), cols])

            def softmax(n, s):
                r0, _, p = item(n)
                s = jnp.concatenate(
                    [s[:, :LANES] + edge_scr[...], s[:, LANES:VAR_COL0],
                     s[:, VAR_COL0:] + bias_scr[p]], axis=1)
                lim = (REACH - (t * tq + r0)) if masked else None
                return _pair_softmax(s, None, lim)

            def output(n, e, inv_l):
                r0, cols, _ = item(n)
                o = _pair_output(e, inv_l, vbuf[pl.ds(r0, GROUP_KEYS), cols])
                obuf[pl.ds(r0, GROUP_ROWS), cols] = o.astype(BF16)

            def done(n):
                if (n + 1) % (PROJ_GROUPS * n_pairs) == 0:
                    rows = PROJ_GROUPS * GROUP_ROWS
                    r0 = pl.multiple_of(base + (n + 1) // n_pairs * GROUP_ROWS - rows, rows)
                    y_ref[pl.ds(r0, rows), :] = x_ref[pl.ds(r0, rows), :] + jnp.dot(
                        obuf[pl.ds(r0, rows), :], wo_ref[...].astype(BF16),
                        preferred_element_type=F32)

            _pipelined_pairs(ATTN_GROUPS_PER_TRIP * n_pairs, scores, softmax, output,
                             ATTN_LEAD_SCORES, ATTN_LEAD_SOFTMAX, done)
            return carry

        lax.fori_loop(0, tq // span, trip_body, 0)

    @pl.when(t == 0)
    def _():
        chunk_loop(True)

    @pl.when(t > 0)
    def _():
        chunk_loop(False)


def _attn_prompt(q, k, v, x, wo, table):
    b, s, d = x.shape
    tq = REACH
    assert s % tq == 0 and tq % (ATTN_GROUPS_PER_TRIP * GROUP_ROWS) == 0
    assert ATTN_GROUPS_PER_TRIP % PROJ_GROUPS == 0
    h = table.shape[0]
    n_diag = GROUP_ROWS - 1 + GROUP_KEYS - VAR_COL0
    diag = jnp.arange(n_diag) + (VAR_COL0 - GROUP_ROWS + 1)
    idx = jnp.clip(REACH - diag, -MAX_REL, MAX_REL) + MAX_REL
    u = (table[:, idx] - table[:, 2 * MAX_REL:]).astype(F32) * LOG2_E
    u = jnp.pad(u, ((0, 0), (0, pl.next_power_of_2(n_diag) - n_diag)))
    row = pl.BlockSpec((None, tq, d), lambda i, j: (i, j, 0))
    rows2 = HEADS_PER_VREG * GROUP_ROWS
    return pl.pallas_call(
        functools.partial(_attn_prompt_kernel, tq=tq),
        grid=(b, s // tq),
        in_specs=[row, row, row, row, _const_spec((d, d)), _const_spec(u.shape)],
        out_specs=row,
        out_shape=jax.ShapeDtypeStruct((b, s, d), F32),
        scratch_shapes=[pltpu.VMEM((2 * tq, d), BF16), pltpu.VMEM((2 * tq, d), BF16),
                        pltpu.VMEM((tq, d), BF16),
                        pltpu.VMEM((h // HEADS_PER_VREG, rows2, GROUP_KEYS - VAR_COL0), F32),
                        pltpu.VMEM((rows2, LANES), F32)],
        compiler_params=_params(2),
        name="attn_prompt",
    )(q, k, v, x, wo, u)


def _attn_sample_kernel(q_ref, k_ref, v_ref, ckt_ref, cvt_ref, x_ref, wo_ref, bias_ref, y_ref,
                        obuf):
    n_pairs = q_ref.shape[1] // LANES
    ca = ckt_ref.shape[2]
    cols = [slice(p * LANES, (p + 1) * LANES) for p in range(n_pairs)]

    def pair_rows(ref, p):
        return jnp.concatenate([ref[HEADS_PER_VREG * p + i] for i in range(HEADS_PER_VREG)],
                               axis=0).astype(BF16)

    def scores(p):
        q_bd = _block_diag_queries(q_ref[:, cols[p]])
        s_cache = jnp.dot(q_bd, pair_rows(ckt_ref, p), preferred_element_type=F32)
        s_new = lax.dot_general(q_bd, k_ref[:, cols[p]], _CONTRACT_LANES,
                                preferred_element_type=F32)
        return jnp.concatenate([s_cache, s_new], axis=1)

    def softmax(p, s):
        return _pair_softmax(s, bias_ref[p], None)

    def output(p, e, inv_l):
        pv = lax.dot_general(e[:, :ca], pair_rows(cvt_ref, p), _CONTRACT_LANES,
                             preferred_element_type=F32)
        pv = pv + jnp.dot(e[:, ca:], v_ref[:, cols[p]], preferred_element_type=F32)
        obuf[:, cols[p]] = _select_heads(pv * inv_l).astype(BF16)

    _pipelined_pairs(n_pairs, scores, softmax, output)
    y_ref[...] = x_ref[...] + jnp.dot(obuf[...], wo_ref[...].astype(BF16),
                                      preferred_element_type=F32)


def _attn_sample(q, k, v, cache_kt, cache_vt, x, wo, bias_pairs):
    b, tn, d = x.shape
    new = pl.BlockSpec((None, tn, d), lambda i: (i, 0, 0))
    old = pl.BlockSpec((None,) + cache_kt.shape[1:], lambda i: (i, 0, 0, 0))
    return pl.pallas_call(
        _attn_sample_kernel,
        grid=(b,),
        in_specs=[new, new, new, old, old, new, _const_spec((d, d)),
                  _const_spec(bias_pairs.shape)],
        out_specs=new,
        out_shape=jax.ShapeDtypeStruct((b, tn, d), F32),
        scratch_shapes=[pltpu.VMEM((tn, d), BF16)],
        compiler_params=_params(1),
        name="attn_sample",
    )(q, k, v, cache_kt, cache_vt, x, wo, bias_pairs)


def _gmlp_kernel(x_ref, g_ref, win_ref, vg_ref, ws_ref, bs_ref, wout_ref, *rest,
                 blk, emit_v):
    if emit_v:
        y_ref, vout_ref, gated = rest
    else:
        y_ref, gated = rest
    rows, d = x_ref.shape
    n_pairs = d // LANES
    x = x_ref[...]
    h = _rms(x, g_ref[...]).astype(BF16)
    def gelu(z):
        return 0.5 * z * (1.0 + lax.erf(z * np.float32(np.sqrt(0.5))))

    v = _rms(gelu(jnp.dot(h, win_ref[:, d:].astype(BF16), preferred_element_type=F32)),
             vg_ref[...])
    u = gelu(jnp.dot(h, win_ref[:, :d].astype(BF16), preferred_element_type=F32))
    if emit_v:
        vout_ref[...] = v
    n_blk = rows // blk
    lane = lax.broadcasted_iota(jnp.int32, (blk, n_blk * LANES), 1) & (LANES - 1)
    zero = jnp.zeros((blk, n_blk * LANES), F32)
    for p in range(n_pairs):
        cols = slice(p * LANES, (p + 1) * LANES)
        vcat = jnp.concatenate([v[r * blk:(r + 1) * blk, cols] for r in range(n_blk)], axis=1)
        v_stack = jnp.concatenate(
            [jnp.where(lane < HEAD_DIM, vcat, zero), jnp.where(lane >= HEAD_DIM, vcat, zero)],
            axis=0).astype(BF16)
        m = jnp.dot(ws_ref[p], v_stack, preferred_element_type=F32)
        for r in range(n_blk):
            rs = slice(r * blk, (r + 1) * blk)
            mixed = m[:, r * LANES:(r + 1) * LANES] + bs_ref[:, cols]
            gated[rs, cols] = (u[rs, cols] * mixed).astype(BF16)
    y_ref[...] = x + jnp.dot(gated[...], wout_ref[...].astype(BF16), preferred_element_type=F32)


def _gmlp(x, gain, w_in, v_gain, ws_pairs, bs_rows, w_out, rows, blk, emit_v):
    b, s, d = x.shape
    w = w_in.shape[1] - d
    assert s % rows == 0 and rows % blk == 0 and w == d
    row = pl.BlockSpec((None, rows, d), lambda i, j: (i, j, 0))
    out_specs = [row]
    out_shape = [jax.ShapeDtypeStruct((b, s, d), F32)]
    if emit_v:
        out_specs.append(row)
        out_shape.append(jax.ShapeDtypeStruct((b, s, w), F32))
    return pl.pallas_call(
        functools.partial(_gmlp_kernel, blk=blk, emit_v=emit_v),
        grid=(b, s // rows),
        in_specs=[row, _const_spec((1, d)), _const_spec(w_in.shape), _const_spec((1, w)),
                  _const_spec(ws_pairs.shape), _const_spec(bs_rows.shape),
                  _const_spec(w_out.shape)],
        out_specs=out_specs,
        out_shape=out_shape,
        scratch_shapes=[pltpu.VMEM((rows, w), BF16)],
        compiler_params=_params(2),
        name="gmlp",
    )(x, gain.reshape(1, d), w_in, v_gain.reshape(1, w), ws_pairs, bs_rows, w_out)


def _conv_taps(stage, n, w, cb):
    c = cb + stage[SUBLANES:SUBLANES + n, :] * w[2:3]
    c = c + stage[SUBLANES - 1:SUBLANES - 1 + n, :] * w[1:2]
    return c + stage[SUBLANES - 2:SUBLANES - 2 + n, :] * w[0:1]


def _ffn_kernel(x_ref, g_ref, wg_ref, wv_ref, cw_ref, cb_ref, wdn_ref, gf_ref,
                y_ref, carry_ref, h_scr, a_scr, act_scr, *, final_norm):
    t = pl.program_id(1)
    rows = x_ref.shape[0]
    ff = wdn_ref.shape[0]
    fc = FF_CHUNK

    @pl.when(t == 0)
    def _():
        carry_ref[...] = jnp.zeros_like(carry_ref)

    x = x_ref[...]
    h_scr[...] = _rms(x, g_ref[...]).astype(BF16)

    for j in range(ff // fc):
        chunk = slice(j * fc, (j + 1) * fc)
        halves = []
        for half, w_ref in enumerate((wg_ref, wv_ref)):
            cols = slice(half * ff + j * fc, half * ff + (j + 1) * fc)
            stage = a_scr.at[2 * (j % 2) + half]
            a = jnp.dot(h_scr[...], w_ref[j], preferred_element_type=F32)
            stage[0:SUBLANES, :] = carry_ref[:, cols]
            stage[SUBLANES:SUBLANES + rows, :] = a
            carry_ref[:, cols] = a[rows - SUBLANES:rows]
            halves.append(_conv_taps(stage, rows, cw_ref[:, cols], cb_ref[:, cols]))
        act_scr[:, chunk] = (jax.nn.silu(halves[0]) * halves[1]).astype(BF16)

    y = x + jnp.dot(act_scr[...], wdn_ref[...], preferred_element_type=F32)
    if final_norm:
        y = _rms(y, gf_ref[...])
    y_ref[...] = y


def _ffn_stream_kernel(x_ref, g_ref, hg_ref, hv_ref, wg_ref, wv_ref, cwg_ref, cwv_ref, cbg_ref,
                       cbv_ref, wdn_ref, gf_ref, y_ref, cg_ref, cv_ref, wg_out, wv_out, wdn_out,
                       h_scr, acc_scr, a_scr, *, n_sub, final_norm):
    j = pl.program_id(0)
    rows = x_ref.shape[0]
    seq = rows // n_sub

    @pl.when(j == 0)
    def _():
        h_scr[...] = _rms(x_ref[...], g_ref[...]).astype(BF16)
        acc_scr[...] = jnp.zeros_like(acc_scr)

    def branch(slot, w_ref, w_out, hist_ref, carry_out, cw_ref, cb_ref):
        w_bf = w_ref[...].astype(BF16)
        w_out[...] = w_bf
        a = jnp.dot(h_scr[...], w_bf, preferred_element_type=F32)
        outs = []
        for s in range(n_sub):
            stage = a_scr.at[slot, s]
            stage[0:SUBLANES, :] = hist_ref[s]
            stage[SUBLANES:SUBLANES + seq, :] = a[s * seq:(s + 1) * seq]
            carry_out[s] = a[(s + 1) * seq - SUBLANES:(s + 1) * seq]
            outs.append(_conv_taps(stage, seq, cw_ref[...], cb_ref[...]))
        return jnp.concatenate(outs, axis=0)

    gate = branch(0, wg_ref, wg_out, hg_ref, cg_ref, cwg_ref, cbg_ref)
    val = branch(1, wv_ref, wv_out, hv_ref, cv_ref, cwv_ref, cbv_ref)
    wd_bf = wdn_ref[...].astype(BF16)
    wdn_out[...] = wd_bf
    acc_scr[...] += jnp.dot((jax.nn.silu(gate) * val).astype(BF16), wd_bf,
                            preferred_element_type=F32)

    @pl.when(j == pl.num_programs(0) - 1)
    def _():
        y = x_ref[...] + acc_scr[...]
        if final_norm:
            y = _rms(y, gf_ref[...])
        y_ref[...] = y


def _layer_spec(shape, layer):
    nd = len(shape)
    return pl.BlockSpec((None,) + tuple(shape[1:]), lambda *_: (layer,) + (0,) * (nd - 1),
                        pipeline_mode=pl.Buffered(1))


def _conv_ffn(x, gain, wg_bf, wv_bf, conv_w, conv_b, wdn_bf, layer, final_gain, rows):
    b, s, d = x.shape
    ff = wdn_bf.shape[0]
    assert s % rows == 0 and rows % SUBLANES == 0
    assert wg_bf.shape == wv_bf.shape == (ff // FF_CHUNK, d, FF_CHUNK)
    final_norm = final_gain is not None
    gf = (final_gain if final_norm else jnp.ones((d,), F32)).reshape(1, d)
    row = pl.BlockSpec((None, rows, d), lambda i, j: (i, j, 0))
    carry_spec = pl.BlockSpec((None, SUBLANES, 2 * ff), lambda i, j: (i, 0, 0))
    return pl.pallas_call(
        functools.partial(_ffn_kernel, final_norm=final_norm),
        grid=(b, s // rows),
        in_specs=[row, _const_spec((1, d)), _const_spec(wg_bf.shape), _const_spec(wv_bf.shape),
                  _layer_spec(conv_w.shape, layer), _layer_spec(conv_b.shape, layer),
                  _const_spec(wdn_bf.shape), _const_spec((1, d))],
        out_specs=[row, carry_spec],
        out_shape=[jax.ShapeDtypeStruct((b, s, d), F32),
                   jax.ShapeDtypeStruct((b, SUBLANES, 2 * ff), F32)],
        scratch_shapes=[pltpu.VMEM((rows, d), BF16),
                        pltpu.VMEM((4, rows + SUBLANES, FF_CHUNK), F32),
                        pltpu.VMEM((rows, ff), BF16)],
        compiler_params=_params(2),
        name="conv_ffn",
    )(x, gain.reshape(1, d), wg_bf, wv_bf, conv_w, conv_b, wdn_bf, gf)


def _conv_ffn_stream(x, gain, hist, w_up, conv_w, conv_b, w_down, layer, final_gain, n_sub):
    r, d = x.shape
    ff = w_down.shape[1]
    fc = FF_CHUNK
    c = ff // fc
    seq = r // n_sub
    assert ff % fc == 0 and seq % SUBLANES == 0
    final_norm = final_gain is not None
    gf = (final_gain if final_norm else jnp.ones((d,), F32)).reshape(1, d)

    def cols(blk_rows, half):
        return pl.BlockSpec((None, blk_rows, fc), lambda j: (layer, 0, half * c + j))

    def hist_cols(half):
        return pl.BlockSpec((n_sub, SUBLANES, fc), lambda j: (0, 0, half * c + j))

    half_cols = pl.BlockSpec((n_sub, SUBLANES, fc), lambda j: (0, 0, j))
    up_cols = pl.BlockSpec((None, d, fc), lambda j: (j, 0, 0))
    dn_rows = pl.BlockSpec((fc, d), lambda j: (j, 0))
    return pl.pallas_call(
        functools.partial(_ffn_stream_kernel, n_sub=n_sub, final_norm=final_norm),
        grid=(c,),
        in_specs=[_const_spec((r, d)), _const_spec((1, d)), hist_cols(0), hist_cols(1),
                  cols(d, 0), cols(d, 1), cols(CONV_W, 0), cols(CONV_W, 1), cols(1, 0), cols(1, 1),
                  pl.BlockSpec((None, fc, d), lambda j: (layer, j, 0)),
                  _const_spec((1, d))],
        out_specs=[pl.BlockSpec((r, d), lambda j: (0, 0)), half_cols, half_cols,
                   up_cols, up_cols, dn_rows],
        out_shape=[jax.ShapeDtypeStruct((r, d), F32),
                   jax.ShapeDtypeStruct((n_sub, SUBLANES, ff), F32),
                   jax.ShapeDtypeStruct((n_sub, SUBLANES, ff), F32),
                   jax.ShapeDtypeStruct((c, d, fc), BF16),
                   jax.ShapeDtypeStruct((c, d, fc), BF16),
                   jax.ShapeDtypeStruct((ff, d), BF16)],
        scratch_shapes=[pltpu.VMEM((r, d), BF16), pltpu.VMEM((r, d), F32),
                        pltpu.VMEM((2, n_sub, seq + SUBLANES, fc), F32)],
        compiler_params=_params(1),
        name="conv_ffn_stream",
    )(x, gain.reshape(1, d), hist, hist, w_up, w_up, conv_w, conv_w, conv_b, conv_b, w_down, gf)


def _pair_bias(table, n_q, n_k, reach):
    diag = jnp.arange(n_q + n_k - 1) - (n_q - 1)
    idx = jnp.clip(reach - diag, -MAX_REL, MAX_REL) + MAX_REL
    r = table[:, idx].astype(F32) * LOG2_E
    h, m = r.shape
    flat = jnp.tile(jnp.pad(r, ((0, 0), (0, 1))), (1, n_q))[:, :n_q * m]
    bias = flat.reshape(h, n_q, m)[:, :, n_q - 1:n_q - 1 + n_k]
    return bias.reshape(h // HEADS_PER_VREG, HEADS_PER_VREG * n_q, n_k)


def _pad_hist(hist):
    pad = [(0, 0)] * (hist.ndim - 2) + [(SUBLANES - hist.shape[-2], 0), (0, 0)]
    return jnp.pad(hist, pad)


def kernel(x_prompt, x_sample, cache_a_k, cache_a_v, state_ffn_conv, ln_mix, ln_ffn, ln_final,
           a_w_qkv, a_rel_bias, a_w_o, b_w_in, b_v_norm, b_w_s, b_bias_s, b_w_out,
           f_w_up, f_conv_w, f_conv_b, f_w_down):
    bp, sp, d = x_prompt.shape
    bs, ts, _ = x_sample.shape
    depth = ln_mix.shape[0]
    ca = cache_a_k.shape[2]
    keep = min(REACH, sp)
    two_ff = f_w_up.shape[2]
    hist_rows = CONV_W - 1
    conv_b_all = f_conv_b.reshape(depth, 1, two_ff)

    xp = x_prompt
    xs = x_sample.reshape(1, bs * ts, d)
    kp_l, vp_l, ks_l, vs_l, gv_l, cp_l, cs_l = [], [], [], [], [], [], []

    for i in range(depth):
        j = i // 2
        if i % 2 == 0:
            wqkv = a_w_qkv[j]
            wo = a_w_o[j]
            qp, kp, vp, ktp, vtp, q, k, v, kt, vt = _qkv_proj(
                xp, xs[0], ln_mix[i], wqkv, keep, ROW_TILE)
            xp = _attn_prompt(qp, kp, vp, xp, wo, a_rel_bias[j])
            kp_l.append(jnp.transpose(ktp.reshape(bp, N_HEADS, HEAD_DIM, keep), (0, 3, 1, 2)))
            vp_l.append(jnp.transpose(vtp.reshape(bp, N_HEADS, HEAD_DIM, keep), (0, 3, 1, 2)))
            xs = _attn_sample(
                q.reshape(bs, ts, d), k.reshape(bs, ts, d), v.reshape(bs, ts, d),
                jnp.transpose(cache_a_k[j], (0, 2, 3, 1)),
                jnp.transpose(cache_a_v[j], (0, 2, 3, 1)),
                xs.reshape(bs, ts, d), wo, _pair_bias(a_rel_bias[j], ts, ca + ts, ca),
            ).reshape(1, bs * ts, d)
            ks_l.append(kt.reshape(bs, ts, N_HEADS, HEAD_DIM))
            vs_l.append(vt.reshape(bs, ts, N_HEADS, HEAD_DIM))
        else:
            win = b_w_in[j]
            wout = b_w_out[j]
            pos = jnp.arange(MLP_BLOCK)
            causal = (pos[None, :] // CHUNK) <= (pos[:, None] // CHUNK)
            ws = jnp.where(causal, b_w_s[j], 0.0).astype(BF16)
            g = ws.shape[0]

            def pair_rows(w):
                w = w.reshape(g // HEADS_PER_VREG, HEADS_PER_VREG, w.shape[1], w.shape[2])
                return jnp.transpose(w, (0, 2, 1, 3)).reshape(
                    g // HEADS_PER_VREG, w.shape[2], HEADS_PER_VREG * w.shape[3])

            def bias_rows(bias):
                return jnp.repeat(bias.T.astype(F32), d // g, axis=1)

            xp = _gmlp(xp, ln_mix[i], win, b_v_norm[j], pair_rows(ws), bias_rows(b_bias_s[j]),
                       wout, ROW_TILE, MLP_BLOCK, False)[0]
            xs, gv = _gmlp(xs, ln_mix[i], win, b_v_norm[j], pair_rows(ws[:, :ts, :ts]),
                           bias_rows(b_bias_s[j][:, :ts]), wout, bs * ts, ts, True)
            gv_l.append(gv.reshape(bs, ts, d))

        fin = ln_final if i == depth - 1 else None
        xs2, cs_g, cs_v, wg, wv, wdn = _conv_ffn_stream(
            xs[0], ln_ffn[i], _pad_hist(state_ffn_conv[i]), f_w_up, f_conv_w, conv_b_all,
            f_w_down, i, fin, bs)
        xs = xs2[None]
        xp, cp = _conv_ffn(xp, ln_ffn[i], wg, wv, f_conv_w, conv_b_all, wdn, i, fin,
                           FFN_ROW_TILE)
        cp_l.append(cp[:, SUBLANES - hist_rows:, :])
        cs_l.append(jnp.concatenate([cs_g, cs_v], axis=-1)[:, SUBLANES - hist_rows:, :])

    return (xp, xs.reshape(bs, ts, d), jnp.stack(kp_l), jnp.stack(vp_l), jnp.stack(ks_l),
            jnp.stack(vs_l), jnp.stack(gv_l), jnp.stack(cp_l), jnp.stack(cs_l))
```

```python
import functools

import jax
import jax.numpy as jnp
import numpy as np
from jax import lax
from jax.experimental import pallas as pl
from jax.experimental.pallas import tpu as pltpu

F32 = jnp.float32
BF16 = jnp.bfloat16

CHUNK = 64
PAST_CHUNKS = 8
REACH = PAST_CHUNKS * CHUNK
BAND = REACH + CHUNK
MAX_REL = 128
N_HEADS = 16
HEAD_DIM = 64
MLP_BLOCK = 128
CONV_W = 3
EPS = 1e-6
NEG = -1e30
LOG2_E = float(np.log2(np.e))
Q_SCALE = HEAD_DIM ** -0.5 * LOG2_E

LANES = 128
SUBLANES = 8
MXU_DIM = 256
HEADS_PER_VREG = LANES // HEAD_DIM
VMEM_LIMIT = 56 * 1024 * 1024

ROW_TILE = 1024
FFN_ROW_TILE = 512
FFN_DOWN_ROW_TILE = 1024
FF_CHUNK = MXU_DIM
GROUP_CHUNKS = LANES // CHUNK
GROUP_ROWS = GROUP_CHUNKS * CHUNK
GROUP_KEYS = REACH + GROUP_ROWS
VAR_COL0 = (REACH - MAX_REL) // LANES * LANES
PROJ_GROUPS = 2
SAMPLE_SEQS_PER_STEP = 4
ATTN_LEAD_SCORES = 1
ATTN_LEAD_SOFTMAX = 3


def _rms(x, g):
    return x * lax.rsqrt(jnp.mean(x * x, axis=-1, keepdims=True) + EPS) * g


def _const_spec(shape):
    nd = len(shape)
    return pl.BlockSpec(shape, lambda *_: (0,) * nd, pipeline_mode=pl.Buffered(1))


def _params(n_grid):
    return pltpu.CompilerParams(
        dimension_semantics=("arbitrary",) * n_grid, vmem_limit_bytes=VMEM_LIMIT)


def _qkv_kernel(x_ref, xs_ref, g_ref, w_ref, q_ref, k_ref, v_ref, kt_ref, vt_ref,
                qs_ref, ks_ref, vs_ref, kst_ref, vst_ref, *, d, nt, n_tiles, tail_tile0):
    i = pl.program_id(0)

    def project(x):
        h = _rms(x, g_ref[...]).astype(BF16)
        q = jnp.dot(h, w_ref[:, 0:d].astype(BF16), preferred_element_type=F32)
        k = jnp.dot(h, w_ref[:, d:2 * d].astype(BF16), preferred_element_type=F32)
        v = jnp.dot(h, w_ref[:, 2 * d:3 * d].astype(BF16), preferred_element_type=F32)
        return (q * Q_SCALE).astype(BF16), k, v

    @pl.when(i < n_tiles)
    def _():
        q, k, v = project(x_ref[...])
        q_ref[...] = q
        k_ref[...] = k.astype(BF16)
        v_ref[...] = v.astype(BF16)

        @pl.when(i % nt >= tail_tile0)
        def _():
            tail_rows = kt_ref.shape[1]
            kt_ref[...] = k[k.shape[0] - tail_rows:].T
            vt_ref[...] = v[v.shape[0] - tail_rows:].T

    @pl.when(i == n_tiles)
    def _():
        q, k, v = project(xs_ref[...])
        qs_ref[...] = q
        ks_ref[...] = k.astype(BF16)
        vs_ref[...] = v.astype(BF16)
        kst_ref[...] = k
        vst_ref[...] = v


def _qkv_proj(x, xs, gain, w_qkv, keep, tm):
    b, s, d = x.shape
    r = xs.shape[0]
    nt = s // tm
    n_tiles = b * nt
    tail_rows = min(tm, keep)
    assert s % tm == 0 and keep % tail_rows == 0
    tail_tile0 = nt - keep // tail_rows

    def tile(i):
        i = jnp.minimum(i, n_tiles - 1)
        return i // nt, i % nt

    row = pl.BlockSpec((None, tm, d), lambda i: (*tile(i), 0))
    tail = pl.BlockSpec((None, d, tail_rows),
                        lambda i: (tile(i)[0], 0, jnp.maximum(tile(i)[1] - tail_tile0, 0)))
    few = pl.BlockSpec((r, d), lambda i: (0, 0))
    return pl.pallas_call(
        functools.partial(_qkv_kernel, d=d, nt=nt, n_tiles=n_tiles, tail_tile0=tail_tile0),
        grid=(n_tiles + 1,),
        in_specs=[row, _const_spec((r, d)), _const_spec((1, d)), _const_spec((d, 3 * d))],
        out_specs=[row, row, row, tail, tail, few, few, few, few, few],
        out_shape=[jax.ShapeDtypeStruct((b, s, d), BF16)] * 3
        + [jax.ShapeDtypeStruct((b, d, keep), F32)] * 2
        + [jax.ShapeDtypeStruct((r, d), BF16)] * 3 + [jax.ShapeDtypeStruct((r, d), F32)] * 2,
        compiler_params=_params(1),
        name="qkv_proj",
    )(x, xs, gain.reshape(1, d), w_qkv)


_CONTRACT_LANES = (((1,), (1,)), ((), ()))


def _block_diag_queries(qp):
    lane = lax.broadcasted_iota(jnp.int32, qp.shape, 1)
    qf = qp.astype(F32)
    zero = jnp.zeros_like(qf)
    return jnp.concatenate(
        [jnp.where(lane < HEAD_DIM, qf, zero), jnp.where(lane >= HEAD_DIM, qf, zero)],
        axis=0).astype(BF16)


def _pair_scores(qp, kb):
    return lax.dot_general(_block_diag_queries(qp), kb, _CONTRACT_LANES,
                           preferred_element_type=F32)


def _select_heads(pv):
    nq = pv.shape[0] // HEADS_PER_VREG
    lane = lax.broadcasted_iota(jnp.int32, (nq, LANES), 1)
    return jnp.where(lane < HEAD_DIM, pv[:nq], pv[nq:])


def _pair_softmax(s, bias, lim):
    if bias is not None:
        s = s + bias
    if lim is not None:
        col = lax.broadcasted_iota(jnp.int32, s.shape, 1)
        s = jnp.where(col >= lim, s, NEG)
    m = jnp.max(s, axis=-1, keepdims=True)
    e = jnp.exp2(s - m)
    return e.astype(BF16), 1.0 / jnp.sum(e, axis=-1, keepdims=True)


def _pair_output(e, inv_l, vb):
    return _select_heads(jnp.dot(e, vb, preferred_element_type=F32) * inv_l)


def _pipelined_pairs(n, scores, softmax, output, lead_scores=1, lead_softmax=1, done=None):
    s, e = {}, {}
    for step in range(n + lead_scores + lead_softmax):
        i_sm = step - lead_scores
        i_out = i_sm - lead_softmax
        if step < n:
            s[step] = scores(step)
        if 0 <= i_sm < n:
            e[i_sm] = softmax(i_sm, s.pop(i_sm))
        if i_out >= 0:
            output(i_out, *e.pop(i_out))
            if done is not None:
                done(i_out)


def _band_bias_init(u_ref, bias_scr, edge_scr):
    n_var = GROUP_KEYS - VAR_COL0
    ulen = u_ref.shape[1]
    row = lax.broadcasted_iota(jnp.int32, (GROUP_ROWS, n_var), 0)
    col = lax.broadcasted_iota(jnp.int32, (GROUP_ROWS, n_var), 1) + VAR_COL0
    past_band = col >= row // CHUNK * CHUNK + BAND
    for h in range(u_ref.shape[0]):
        per_diag = jnp.broadcast_to(u_ref[h:h + 1, :], (GROUP_ROWS, ulen))
        skew = pltpu.roll(per_diag, ulen - (GROUP_ROWS - 1), 1, stride=1, stride_axis=0)
        p, half = divmod(h, HEADS_PER_VREG)
        bias_scr[p, half * GROUP_ROWS:(half + 1) * GROUP_ROWS, :] = jnp.where(
            past_band, NEG, skew[:, :n_var])
    row = lax.broadcasted_iota(jnp.int32, edge_scr.shape, 0) % GROUP_ROWS
    col = lax.broadcasted_iota(jnp.int32, edge_scr.shape, 1)
    edge_scr[...] = jnp.where(col < row // CHUNK * CHUNK, NEG, 0.0)


def _attn_prompt_kernel(q_ref, k_ref, v_ref, x_ref, wo_ref, u_ref, y_ref,
                        kbuf, vbuf, obuf, bias_scr, edge_scr, *, tq):
    t = pl.program_id(1)
    n_pairs = q_ref.shape[1] // LANES

    @pl.when((pl.program_id(0) == 0) & (t == 0))
    def _():
        _band_bias_init(u_ref, bias_scr, edge_scr)

    @pl.when(t == 0)
    def _():
        kbuf[...] = jnp.zeros(kbuf.shape, BF16)
        vbuf[...] = jnp.zeros(vbuf.shape, BF16)

    def window(prev, cur, r0, cols):
        return jnp.concatenate([prev[r0:tq, cols], cur[0:r0 + GROUP_KEYS - tq, cols]], axis=0)

    def tile_items(masked):
        def item(n):
            g, p = divmod(n, n_pairs)
            return g * GROUP_ROWS, slice(p * LANES, (p + 1) * LANES), p

        def scores(n):
            r0, cols, _ = item(n)
            return _pair_scores(q_ref[r0:r0 + GROUP_ROWS, cols], window(kbuf, k_ref, r0, cols))

        def softmax(n, s):
            r0, _, p = item(n)
            s = jnp.concatenate(
                [s[:, :LANES] + edge_scr[...], s[:, LANES:VAR_COL0],
                 s[:, VAR_COL0:] + bias_scr[p]], axis=1)
            lim = (REACH - (t * tq + r0)) if masked else None
            return _pair_softmax(s, None, lim)

        def output(n, e, inv_l):
            r0, cols, _ = item(n)
            o = _pair_output(e, inv_l, window(vbuf, v_ref, r0, cols))
            obuf[r0:r0 + GROUP_ROWS, cols] = o.astype(BF16)

        def done(n):
            if (n + 1) % (PROJ_GROUPS * n_pairs) == 0:
                rows = PROJ_GROUPS * GROUP_ROWS
                r0 = (n + 1) // n_pairs * GROUP_ROWS - rows
                y_ref[r0:r0 + rows, :] = x_ref[r0:r0 + rows, :] + jnp.dot(
                    obuf[r0:r0 + rows, :], wo_ref[...].astype(BF16),
                    preferred_element_type=F32)

        _pipelined_pairs(tq // GROUP_ROWS * n_pairs, scores, softmax, output,
                         ATTN_LEAD_SCORES, ATTN_LEAD_SOFTMAX, done)

    @pl.when(t == 0)
    def _():
        tile_items(True)

    @pl.when(t > 0)
    def _():
        tile_items(False)

    kbuf[...] = k_ref[...]
    vbuf[...] = v_ref[...]


def _attn_prompt(q, k, v, x, wo, table):
    b, s, d = x.shape
    tq = REACH
    assert s % tq == 0
    assert tq % (PROJ_GROUPS * GROUP_ROWS) == 0
    h = table.shape[0]
    n_diag = GROUP_ROWS - 1 + GROUP_KEYS - VAR_COL0
    diag = jnp.arange(n_diag) + (VAR_COL0 - GROUP_ROWS + 1)
    idx = jnp.clip(REACH - diag, -MAX_REL, MAX_REL) + MAX_REL
    u = (table[:, idx] - table[:, 2 * MAX_REL:]).astype(F32) * LOG2_E
    u = jnp.pad(u, ((0, 0), (0, pl.next_power_of_2(n_diag) - n_diag)))
    row = pl.BlockSpec((None, tq, d), lambda i, j: (i, j, 0))
    rows2 = HEADS_PER_VREG * GROUP_ROWS
    return pl.pallas_call(
        functools.partial(_attn_prompt_kernel, tq=tq),
        grid=(b, s // tq),
        in_specs=[row, row, row, row, _const_spec((d, d)), _const_spec(u.shape)],
        out_specs=row,
        out_shape=jax.ShapeDtypeStruct((b, s, d), F32),
        scratch_shapes=[pltpu.VMEM((tq, d), BF16), pltpu.VMEM((tq, d), BF16),
                        pltpu.VMEM((tq, d), BF16),
                        pltpu.VMEM((h // HEADS_PER_VREG, rows2, GROUP_KEYS - VAR_COL0), F32),
                        pltpu.VMEM((rows2, LANES), F32)],
        compiler_params=_params(2),
        name="attn_prompt",
    )(q, k, v, x, wo, u)


def _attn_sample_kernel(q_ref, k_ref, v_ref, ckt_ref, cvt_ref, x_ref, wo_ref, bias_ref, y_ref,
                        obuf):
    n_seq, tn, d = q_ref.shape
    n_pairs = d // LANES
    ca = ckt_ref.shape[3]

    def item(n):
        e, p = divmod(n, n_pairs)
        return e, p, slice(p * LANES, (p + 1) * LANES)

    def pair_rows(ref, e, p):
        return jnp.concatenate(
            [ref[e, HEADS_PER_VREG * p + i] for i in range(HEADS_PER_VREG)], axis=0).astype(BF16)

    def scores(n):
        e, p, cols = item(n)
        q_bd = _block_diag_queries(q_ref[e, :, cols])
        s_cache = jnp.dot(q_bd, pair_rows(ckt_ref, e, p), preferred_element_type=F32)
        s_new = lax.dot_general(q_bd, k_ref[e, :, cols], _CONTRACT_LANES,
                                preferred_element_type=F32)
        return jnp.concatenate([s_cache, s_new], axis=1)

    def softmax(n, s):
        return _pair_softmax(s, bias_ref[item(n)[1]], None)

    def output(n, pr, inv_l):
        e, p, cols = item(n)
        pv = lax.dot_general(pr[:, :ca], pair_rows(cvt_ref, e, p), _CONTRACT_LANES,
                             preferred_element_type=F32)
        pv = pv + jnp.dot(pr[:, ca:], v_ref[e, :, cols], preferred_element_type=F32)
        obuf[e * tn:(e + 1) * tn, cols] = _select_heads(pv * inv_l).astype(BF16)

    _pipelined_pairs(n_seq * n_pairs, scores, softmax, output)
    proj = jnp.dot(obuf[...], wo_ref[...].astype(BF16), preferred_element_type=F32)
    for e in range(n_seq):
        y_ref[e] = x_ref[e] + proj[e * tn:(e + 1) * tn]


def _attn_sample(q, k, v, cache_kt, cache_vt, x, wo, bias_pairs):
    b, tn, d = x.shape
    n_seq = SAMPLE_SEQS_PER_STEP
    assert b % n_seq == 0
    new = pl.BlockSpec((n_seq, tn, d), lambda i: (i, 0, 0))
    old = pl.BlockSpec((n_seq,) + cache_kt.shape[1:], lambda i: (i, 0, 0, 0))
    return pl.pallas_call(
        _attn_sample_kernel,
        grid=(b // n_seq,),
        in_specs=[new, new, new, old, old, new, _const_spec((d, d)),
                  _const_spec(bias_pairs.shape)],
        out_specs=new,
        out_shape=jax.ShapeDtypeStruct((b, tn, d), F32),
        scratch_shapes=[pltpu.VMEM((n_seq * tn, d), BF16)],
        compiler_params=_params(1),
        name="attn_sample",
    )(q, k, v, cache_kt, cache_vt, x, wo, bias_pairs)


def _gelu(z):
    return 0.5 * z * (1.0 + lax.erf(z * np.float32(np.sqrt(0.5))))


def _gmlp_rows(x, g_ref, win_ref, vg_ref, ws_ref, bs_ref, wout_ref, gated):
    rows, d = x.shape
    blk = bs_ref.shape[0]
    n_pairs = d // LANES
    h = _rms(x, g_ref[...]).astype(BF16)
    v = _rms(_gelu(jnp.dot(h, win_ref[:, d:].astype(BF16), preferred_element_type=F32)),
             vg_ref[...])
    u = _gelu(jnp.dot(h, win_ref[:, :d].astype(BF16), preferred_element_type=F32))
    n_blk = rows // blk
    lane = lax.broadcasted_iota(jnp.int32, (blk, n_blk * LANES), 1) & (LANES - 1)
    zero = jnp.zeros((blk, n_blk * LANES), F32)
    for p in range(n_pairs):
        cols = slice(p * LANES, (p + 1) * LANES)
        vcat = jnp.concatenate([v[r * blk:(r + 1) * blk, cols] for r in range(n_blk)], axis=1)
        v_stack = jnp.concatenate(
            [jnp.where(lane < HEAD_DIM, vcat, zero), jnp.where(lane >= HEAD_DIM, vcat, zero)],
            axis=0).astype(BF16)
        m = jnp.dot(ws_ref[p], v_stack, preferred_element_type=F32)
        for r in range(n_blk):
            rs = slice(r * blk, (r + 1) * blk)
            mixed = m[:, r * LANES:(r + 1) * LANES] + bs_ref[:, cols]
            gated[rs, cols] = (u[rs, cols] * mixed).astype(BF16)
    y = x + jnp.dot(gated[0:rows, :], wout_ref[...].astype(BF16), preferred_element_type=F32)
    return y, v


def _gmlp_kernel(x_ref, xs_ref, g_ref, win_ref, vg_ref, ws_ref, bs_ref, wss_ref, bss_ref, wout_ref,
                 y_ref, ys_ref, vs_ref, gated, *, n_tiles):
    i = pl.program_id(0)

    @pl.when(i < n_tiles)
    def _():
        y_ref[...] = _gmlp_rows(x_ref[...], g_ref, win_ref, vg_ref, ws_ref, bs_ref, wout_ref,
                                gated)[0]

    @pl.when(i == n_tiles)
    def _():
        ys_ref[...], vs_ref[...] = _gmlp_rows(xs_ref[...], g_ref, win_ref, vg_ref, wss_ref,
                                              bss_ref, wout_ref, gated)


def _gmlp(x, xs, gain, w_in, v_gain, ws_pairs, bs_rows, wss_pairs, bss_rows, w_out, rows):
    b, s, d = x.shape
    r = xs.shape[0]
    w = w_in.shape[1] - d
    nt = s // rows
    n_tiles = b * nt
    assert s % rows == 0 and rows % bs_rows.shape[0] == 0 and r % bss_rows.shape[0] == 0
    assert w == d and r <= rows

    def tile(i):
        i = jnp.minimum(i, n_tiles - 1)
        return i // nt, i % nt

    row = pl.BlockSpec((None, rows, d), lambda i: (*tile(i), 0))
    few = pl.BlockSpec((r, d), lambda i: (0, 0))
    return pl.pallas_call(
        functools.partial(_gmlp_kernel, n_tiles=n_tiles),
        grid=(n_tiles + 1,),
        in_specs=[row, _const_spec((r, d)), _const_spec((1, d)), _const_spec(w_in.shape),
                  _const_spec((1, w)), _const_spec(ws_pairs.shape), _const_spec(bs_rows.shape),
                  _const_spec(wss_pairs.shape), _const_spec(bss_rows.shape),
                  _const_spec(w_out.shape)],
        out_specs=[row, few, few],
        out_shape=[jax.ShapeDtypeStruct((b, s, d), F32), jax.ShapeDtypeStruct((r, d), F32),
                   jax.ShapeDtypeStruct((r, w), F32)],
        scratch_shapes=[pltpu.VMEM((rows, w), BF16)],
        compiler_params=_params(1),
        name="gmlp",
    )(x, xs, gain.reshape(1, d), w_in, v_gain.reshape(1, w), ws_pairs, bs_rows, wss_pairs,
      bss_rows, w_out)


def _conv_taps(stage, n, w, cb):
    c = cb + stage[SUBLANES:SUBLANES + n, :] * w[2:3]
    c = c + stage[SUBLANES - 1:SUBLANES - 1 + n, :] * w[1:2]
    return c + stage[SUBLANES - 2:SUBLANES - 2 + n, :] * w[0:1]


def _ffn_down_kernel(x_ref, act_ref, wdn_ref, gf_ref, y_ref, *, final_norm):
    y = x_ref[...] + jnp.dot(act_ref[...], wdn_ref[...], preferred_element_type=F32)
    if final_norm:
        y = _rms(y, gf_ref[...])
    y_ref[...] = y


def _ffn_kernel(x_ref, g_ref, wg_ref, wv_ref, cw_ref, cb_ref,
                act_scr, carry_ref, h_scr, a_scr):
    t = pl.program_id(1)
    rows = x_ref.shape[0]
    ff = act_scr.shape[1]
    fc = FF_CHUNK

    @pl.when(t == 0)
    def _():
        carry_ref[...] = jnp.zeros_like(carry_ref)

    x = x_ref[...]
    h_scr[...] = _rms(x, g_ref[...]).astype(BF16)

    for j in range(ff // fc):
        chunk = slice(j * fc, (j + 1) * fc)
        halves = []
        for half, w_ref in enumerate((wg_ref, wv_ref)):
            cols = slice(half * ff + j * fc, half * ff + (j + 1) * fc)
            stage = a_scr.at[2 * (j % 2) + half]
            a = jnp.dot(h_scr[...], w_ref[j], preferred_element_type=F32)
            stage[0:SUBLANES, :] = carry_ref[:, cols]
            stage[SUBLANES:SUBLANES + rows, :] = a
            carry_ref[:, cols] = a[rows - SUBLANES:rows]
            halves.append(_conv_taps(stage, rows, cw_ref[:, cols], cb_ref[:, cols]))
        act_scr[:, chunk] = (jax.nn.silu(halves[0]) * halves[1]).astype(BF16)


def _ffn_stream_kernel(x_ref, g_ref, hg_ref, hv_ref, wg_ref, wv_ref, cwg_ref, cwv_ref, cbg_ref,
                       cbv_ref, wdn_ref, gf_ref, y_ref, cg_ref, cv_ref, wg_out, wv_out, wdn_out,
                       h_scr, acc_scr, a_scr, *, n_sub, final_norm):
    j = pl.program_id(0)
    rows = x_ref.shape[0]
    seq = rows // n_sub

    @pl.when(j == 0)
    def _():
        h_scr[...] = _rms(x_ref[...], g_ref[...]).astype(BF16)
        acc_scr[...] = jnp.zeros_like(acc_scr)

    def branch(slot, w_ref, w_out, hist_ref, carry_out, cw_ref, cb_ref):
        w_bf = w_ref[...].astype(BF16)
        w_out[...] = w_bf
        a = jnp.dot(h_scr[...], w_bf, preferred_element_type=F32)
        outs = []
        for s in range(n_sub):
            stage = a_scr.at[slot, s]
            stage[0:SUBLANES, :] = hist_ref[s]
            stage[SUBLANES:SUBLANES + seq, :] = a[s * seq:(s + 1) * seq]
            carry_out[s] = a[(s + 1) * seq - SUBLANES:(s + 1) * seq]
            outs.append(_conv_taps(stage, seq, cw_ref[...], cb_ref[...]))
        return jnp.concatenate(outs, axis=0)

    gate = branch(0, wg_ref, wg_out, hg_ref, cg_ref, cwg_ref, cbg_ref)
    val = branch(1, wv_ref, wv_out, hv_ref, cv_ref, cwv_ref, cbv_ref)
    wd_bf = wdn_ref[...].astype(BF16)
    wdn_out[...] = wd_bf
    acc_scr[...] += jnp.dot((jax.nn.silu(gate) * val).astype(BF16), wd_bf,
                            preferred_element_type=F32)

    @pl.when(j == pl.num_programs(0) - 1)
    def _():
        y = x_ref[...] + acc_scr[...]
        if final_norm:
            y = _rms(y, gf_ref[...])
        y_ref[...] = y


def _layer_spec(shape, layer):
    nd = len(shape)
    return pl.BlockSpec((None,) + tuple(shape[1:]), lambda *_: (layer,) + (0,) * (nd - 1),
                        pipeline_mode=pl.Buffered(1))


def _conv_ffn(x, gain, wg_bf, wv_bf, conv_w, conv_b, wdn_bf, layer, final_gain, rows):
    b, s, d = x.shape
    ff = wdn_bf.shape[0]
    assert s % rows == 0 and rows % SUBLANES == 0
    assert wg_bf.shape == wv_bf.shape == (ff // FF_CHUNK, d, FF_CHUNK)
    final_norm = final_gain is not None
    gf = (final_gain if final_norm else jnp.ones((d,), F32)).reshape(1, d)
    row = pl.BlockSpec((None, rows, d), lambda i, j: (i, j, 0))
    act_row = pl.BlockSpec((None, rows, ff), lambda i, j: (i, j, 0))
    carry_spec = pl.BlockSpec((None, SUBLANES, 2 * ff), lambda i, j: (i, 0, 0))
    act, carry = pl.pallas_call(
        _ffn_kernel,
        grid=(b, s // rows),
        in_specs=[row, _const_spec((1, d)), _const_spec(wg_bf.shape), _const_spec(wv_bf.shape),
                  _layer_spec(conv_w.shape, layer), _layer_spec(conv_b.shape, layer)],
        out_specs=[act_row, carry_spec],
        out_shape=[jax.ShapeDtypeStruct((b, s, ff), BF16),
                   jax.ShapeDtypeStruct((b, SUBLANES, 2 * ff), F32)],
        scratch_shapes=[pltpu.VMEM((rows, d), BF16),
                        pltpu.VMEM((4, rows + SUBLANES, FF_CHUNK), F32)],
        compiler_params=_params(2),
        name="conv_ffn_up",
    )(x, gain.reshape(1, d), wg_bf, wv_bf, conv_w, conv_b)
    drows = FFN_DOWN_ROW_TILE
    assert s % drows == 0
    drow = pl.BlockSpec((None, drows, d), lambda i, j: (i, j, 0))
    y = pl.pallas_call(
        functools.partial(_ffn_down_kernel, final_norm=final_norm),
        grid=(b, s // drows),
        in_specs=[drow, pl.BlockSpec((None, drows, ff), lambda i, j: (i, j, 0)),
                  _const_spec(wdn_bf.shape), _const_spec((1, d))],
        out_specs=drow,
        out_shape=jax.ShapeDtypeStruct((b, s, d), F32),
        compiler_params=_params(2),
        name="conv_ffn_down",
    )(x, act, wdn_bf, gf)
    return y, carry


def _conv_ffn_stream(x, gain, hist, w_up, conv_w, conv_b, w_down, layer, final_gain, n_sub):
    r, d = x.shape
    ff = w_down.shape[1]
    fc = FF_CHUNK
    c = ff // fc
    seq = r // n_sub
    assert ff % fc == 0 and seq % SUBLANES == 0
    final_norm = final_gain is not None
    gf = (final_gain if final_norm else jnp.ones((d,), F32)).reshape(1, d)

    def cols(blk_rows, half):
        return pl.BlockSpec((None, blk_rows, fc), lambda j: (layer, 0, half * c + j))

    def hist_cols(half):
        return pl.BlockSpec((n_sub, SUBLANES, fc), lambda j: (0, 0, half * c + j))

    half_cols = pl.BlockSpec((n_sub, SUBLANES, fc), lambda j: (0, 0, j))
    up_cols = pl.BlockSpec((None, d, fc), lambda j: (j, 0, 0))
    dn_rows = pl.BlockSpec((fc, d), lambda j: (j, 0))
    return pl.pallas_call(
        functools.partial(_ffn_stream_kernel, n_sub=n_sub, final_norm=final_norm),
        grid=(c,),
        in_specs=[_const_spec((r, d)), _const_spec((1, d)), hist_cols(0), hist_cols(1),
                  cols(d, 0), cols(d, 1), cols(CONV_W, 0), cols(CONV_W, 1), cols(1, 0), cols(1, 1),
                  pl.BlockSpec((None, fc, d), lambda j: (layer, j, 0)),
                  _const_spec((1, d))],
        out_specs=[pl.BlockSpec((r, d), lambda j: (0, 0)), half_cols, half_cols,
                   up_cols, up_cols, dn_rows],
        out_shape=[jax.ShapeDtypeStruct((r, d), F32),
                   jax.ShapeDtypeStruct((n_sub, SUBLANES, ff), F32),
                   jax.ShapeDtypeStruct((n_sub, SUBLANES, ff), F32),
                   jax.ShapeDtypeStruct((c, d, fc), BF16),
                   jax.ShapeDtypeStruct((c, d, fc), BF16),
                   jax.ShapeDtypeStruct((ff, d), BF16)],
        scratch_shapes=[pltpu.VMEM((r, d), BF16), pltpu.VMEM((r, d), F32),
                        pltpu.VMEM((2, n_sub, seq + SUBLANES, fc), F32)],
        compiler_params=_params(1),
        name="conv_ffn_stream",
    )(x, gain.reshape(1, d), hist, hist, w_up, w_up, conv_w, conv_w, conv_b, conv_b, w_down, gf)


def _pair_bias(table, n_q, n_k, reach):
    diag = jnp.arange(n_q + n_k - 1) - (n_q - 1)
    idx = jnp.clip(reach - diag, -MAX_REL, MAX_REL) + MAX_REL
    r = table[:, idx].astype(F32) * LOG2_E
    h, m = r.shape
    flat = jnp.tile(jnp.pad(r, ((0, 0), (0, 1))), (1, n_q))[:, :n_q * m]
    bias = flat.reshape(h, n_q, m)[:, :, n_q - 1:n_q - 1 + n_k]
    return bias.reshape(h // HEADS_PER_VREG, HEADS_PER_VREG * n_q, n_k)


def _pad_hist(hist):
    pad = [(0, 0)] * (hist.ndim - 2) + [(SUBLANES - hist.shape[-2], 0), (0, 0)]
    return jnp.pad(hist, pad)


def kernel(x_prompt, x_sample, cache_a_k, cache_a_v, state_ffn_conv, ln_mix, ln_ffn, ln_final,
           a_w_qkv, a_rel_bias, a_w_o, b_w_in, b_v_norm, b_w_s, b_bias_s, b_w_out,
           f_w_up, f_conv_w, f_conv_b, f_w_down):
    bp, sp, d = x_prompt.shape
    bs, ts, _ = x_sample.shape
    depth = ln_mix.shape[0]
    ca = cache_a_k.shape[2]
    keep = min(REACH, sp)
    two_ff = f_w_up.shape[2]
    hist_rows = CONV_W - 1
    conv_b_all = f_conv_b.reshape(depth, 1, two_ff)

    xp = x_prompt
    xs = x_sample.reshape(bs * ts, d)
    kp_l, vp_l, ks_l, vs_l, gv_l, cp_l, cs_l = [], [], [], [], [], [], []

    for i in range(depth):
        j = i // 2
        if i % 2 == 0:
            wqkv = a_w_qkv[j]
            wo = a_w_o[j]
            qp, kp, vp, ktp, vtp, q, k, v, kt, vt = _qkv_proj(
                xp, xs, ln_mix[i], wqkv, keep, ROW_TILE)
            xp = _attn_prompt(qp, kp, vp, xp, wo, a_rel_bias[j])
            kp_l.append(jnp.transpose(ktp.reshape(bp, N_HEADS, HEAD_DIM, keep), (0, 3, 1, 2)))
            vp_l.append(jnp.transpose(vtp.reshape(bp, N_HEADS, HEAD_DIM, keep), (0, 3, 1, 2)))
            xs = _attn_sample(
                q.reshape(bs, ts, d), k.reshape(bs, ts, d), v.reshape(bs, ts, d),
                jnp.transpose(cache_a_k[j], (0, 2, 3, 1)),
                jnp.transpose(cache_a_v[j], (0, 2, 3, 1)),
                xs.reshape(bs, ts, d), wo, _pair_bias(a_rel_bias[j], ts, ca + ts, ca),
            ).reshape(bs * ts, d)
            ks_l.append(kt.reshape(bs, ts, N_HEADS, HEAD_DIM))
            vs_l.append(vt.reshape(bs, ts, N_HEADS, HEAD_DIM))
        else:
            win = b_w_in[j]
            wout = b_w_out[j]
            pos = jnp.arange(MLP_BLOCK)
            causal = (pos[None, :] // CHUNK) <= (pos[:, None] // CHUNK)
            ws = jnp.where(causal, b_w_s[j], 0.0).astype(BF16)
            g = ws.shape[0]

            def pair_rows(w):
                w = w.reshape(g // HEADS_PER_VREG, HEADS_PER_VREG, w.shape[1], w.shape[2])
                return jnp.transpose(w, (0, 2, 1, 3)).reshape(
                    g // HEADS_PER_VREG, w.shape[2], HEADS_PER_VREG * w.shape[3])

            def bias_rows(bias):
                return jnp.repeat(bias.T.astype(F32), d // g, axis=1)

            xp, xs, gv = _gmlp(xp, xs, ln_mix[i], win, b_v_norm[j], pair_rows(ws),
                               bias_rows(b_bias_s[j]), pair_rows(ws[:, :ts, :ts]),
                               bias_rows(b_bias_s[j][:, :ts]), wout, ROW_TILE)
            gv_l.append(gv.reshape(bs, ts, d))

        fin = ln_final if i == depth - 1 else None
        xs, cs_g, cs_v, wg, wv, wdn = _conv_ffn_stream(
            xs, ln_ffn[i], _pad_hist(state_ffn_conv[i]), f_w_up, f_conv_w, conv_b_all,
            f_w_down, i, fin, bs)
        xp, cp = _conv_ffn(xp, ln_ffn[i], wg, wv, f_conv_w, conv_b_all, wdn, i, fin,
                           FFN_ROW_TILE)
        cp_l.append(cp[:, SUBLANES - hist_rows:, :])
        cs_l.append(jnp.concatenate([cs_g, cs_v], axis=-1)[:, SUBLANES - hist_rows:, :])

    return (xp, xs.reshape(bs, ts, d), jnp.stack(kp_l), jnp.stack(vp_l), jnp.stack(ks_l),
            jnp.stack(vs_l), jnp.stack(gv_l), jnp.stack(cp_l), jnp.stack(cs_l))
```

```python
import functools

import jax
import jax.numpy as jnp
import numpy as np
from jax import lax
from jax.experimental import pallas as pl
from jax.experimental.pallas import tpu as pltpu

F32 = jnp.float32
BF16 = jnp.bfloat16

CHUNK = 64
PAST_CHUNKS = 8
REACH = PAST_CHUNKS * CHUNK
BAND = REACH + CHUNK
MAX_REL = 128
N_HEADS = 16
HEAD_DIM = 64
MLP_BLOCK = 128
CONV_W = 3
EPS = 1e-6
NEG = -1e30
LOG2_E = float(np.log2(np.e))
Q_SCALE = HEAD_DIM ** -0.5 * LOG2_E

LANES = 128
SUBLANES = 8
MXU_DIM = 256
HEADS_PER_VREG = LANES // HEAD_DIM
VMEM_LIMIT = 56 * 1024 * 1024

ROW_TILE = 1024
FFN_ROW_TILE = 512
FF_CHUNK = MXU_DIM
GROUP_CHUNKS = LANES // CHUNK
GROUP_ROWS = GROUP_CHUNKS * CHUNK
GROUP_KEYS = REACH + GROUP_ROWS
VAR_COL0 = (REACH - MAX_REL) // LANES * LANES
PROJ_GROUPS = 2
SAMPLE_SEQS_PER_STEP = 4
ATTN_LEAD_SCORES = 1
ATTN_LEAD_SOFTMAX = 3


def _rms(x, g):
    return x * lax.rsqrt(jnp.mean(x * x, axis=-1, keepdims=True) + EPS) * g


def _const_spec(shape):
    nd = len(shape)
    return pl.BlockSpec(shape, lambda *_: (0,) * nd, pipeline_mode=pl.Buffered(1))


def _params(n_grid):
    return pltpu.CompilerParams(
        dimension_semantics=("arbitrary",) * n_grid, vmem_limit_bytes=VMEM_LIMIT)


def _qkv_kernel(x_ref, xs_ref, g_ref, w_ref, q_ref, k_ref, v_ref, kt_ref, vt_ref,
                qs_ref, ks_ref, vs_ref, kst_ref, vst_ref, *, d, nt, n_tiles, tail_tile0):
    i = pl.program_id(0)

    def project(x):
        h = _rms(x, g_ref[...]).astype(BF16)
        q = jnp.dot(h, w_ref[:, 0:d].astype(BF16), preferred_element_type=F32)
        k = jnp.dot(h, w_ref[:, d:2 * d].astype(BF16), preferred_element_type=F32)
        v = jnp.dot(h, w_ref[:, 2 * d:3 * d].astype(BF16), preferred_element_type=F32)
        return (q * Q_SCALE).astype(BF16), k, v

    @pl.when(i < n_tiles)
    def _():
        q, k, v = project(x_ref[...])
        q_ref[...] = q
        k_ref[...] = k.astype(BF16)
        v_ref[...] = v.astype(BF16)

        @pl.when(i % nt >= tail_tile0)
        def _():
            tail_rows = kt_ref.shape[1]
            kt_ref[...] = k[k.shape[0] - tail_rows:].T
            vt_ref[...] = v[v.shape[0] - tail_rows:].T

    @pl.when(i == n_tiles)
    def _():
        q, k, v = project(xs_ref[...])
        qs_ref[...] = q
        ks_ref[...] = k.astype(BF16)
        vs_ref[...] = v.astype(BF16)
        kst_ref[...] = k
        vst_ref[...] = v


def _qkv_proj(x, xs, gain, w_qkv, keep, tm):
    b, s, d = x.shape
    r = xs.shape[0]
    nt = s // tm
    n_tiles = b * nt
    tail_rows = min(tm, keep)
    assert s % tm == 0 and keep % tail_rows == 0
    tail_tile0 = nt - keep // tail_rows

    def tile(i):
        i = jnp.minimum(i, n_tiles - 1)
        return i // nt, i % nt

    row = pl.BlockSpec((None, tm, d), lambda i: (*tile(i), 0))
    tail = pl.BlockSpec((None, d, tail_rows),
                        lambda i: (tile(i)[0], 0, jnp.maximum(tile(i)[1] - tail_tile0, 0)))
    few = pl.BlockSpec((r, d), lambda i: (0, 0))
    return pl.pallas_call(
        functools.partial(_qkv_kernel, d=d, nt=nt, n_tiles=n_tiles, tail_tile0=tail_tile0),
        grid=(n_tiles + 1,),
        in_specs=[row, _const_spec((r, d)), _const_spec((1, d)), _const_spec((d, 3 * d))],
        out_specs=[row, row, row, tail, tail, few, few, few, few, few],
        out_shape=[jax.ShapeDtypeStruct((b, s, d), BF16)] * 3
        + [jax.ShapeDtypeStruct((b, d, keep), F32)] * 2
        + [jax.ShapeDtypeStruct((r, d), BF16)] * 3 + [jax.ShapeDtypeStruct((r, d), F32)] * 2,
        compiler_params=_params(1),
        name="qkv_proj",
    )(x, xs, gain.reshape(1, d), w_qkv)


_CONTRACT_LANES = (((1,), (1,)), ((), ()))


def _block_diag_queries(qp):
    lane = lax.broadcasted_iota(jnp.int32, qp.shape, 1)
    qf = qp.astype(F32)
    zero = jnp.zeros_like(qf)
    return jnp.concatenate(
        [jnp.where(lane < HEAD_DIM, qf, zero), jnp.where(lane >= HEAD_DIM, qf, zero)],
        axis=0).astype(BF16)


def _pair_scores(qp, kb):
    return lax.dot_general(_block_diag_queries(qp), kb, _CONTRACT_LANES,
                           preferred_element_type=F32)


def _select_heads(pv):
    nq = pv.shape[0] // HEADS_PER_VREG
    lane = lax.broadcasted_iota(jnp.int32, (nq, LANES), 1)
    return jnp.where(lane < HEAD_DIM, pv[:nq], pv[nq:])


def _pair_softmax(s, bias, lim):
    if bias is not None:
        s = s + bias
    if lim is not None:
        col = lax.broadcasted_iota(jnp.int32, s.shape, 1)
        s = jnp.where(col >= lim, s, NEG)
    m = jnp.max(s, axis=-1, keepdims=True)
    e = jnp.exp2(s - m)
    return e.astype(BF16), 1.0 / jnp.sum(e, axis=-1, keepdims=True)


def _pair_output(e, inv_l, vb):
    return _select_heads(jnp.dot(e, vb, preferred_element_type=F32) * inv_l)


def _pipelined_pairs(n, scores, softmax, output, lead_scores=1, lead_softmax=1, done=None):
    s, e = {}, {}
    for step in range(n + lead_scores + lead_softmax):
        i_sm = step - lead_scores
        i_out = i_sm - lead_softmax
        if step < n:
            s[step] = scores(step)
        if 0 <= i_sm < n:
            e[i_sm] = softmax(i_sm, s.pop(i_sm))
        if i_out >= 0:
            output(i_out, *e.pop(i_out))
            if done is not None:
                done(i_out)


def _band_bias_init(u_ref, bias_scr, edge_scr):
    n_var = GROUP_KEYS - VAR_COL0
    ulen = u_ref.shape[1]
    row = lax.broadcasted_iota(jnp.int32, (GROUP_ROWS, n_var), 0)
    col = lax.broadcasted_iota(jnp.int32, (GROUP_ROWS, n_var), 1) + VAR_COL0
    past_band = col >= row // CHUNK * CHUNK + BAND
    for h in range(u_ref.shape[0]):
        per_diag = jnp.broadcast_to(u_ref[h:h + 1, :], (GROUP_ROWS, ulen))
        skew = pltpu.roll(per_diag, ulen - (GROUP_ROWS - 1), 1, stride=1, stride_axis=0)
        p, half = divmod(h, HEADS_PER_VREG)
        bias_scr[p, half * GROUP_ROWS:(half + 1) * GROUP_ROWS, :] = jnp.where(
            past_band, NEG, skew[:, :n_var])
    row = lax.broadcasted_iota(jnp.int32, edge_scr.shape, 0) % GROUP_ROWS
    col = lax.broadcasted_iota(jnp.int32, edge_scr.shape, 1)
    edge_scr[...] = jnp.where(col < row // CHUNK * CHUNK, NEG, 0.0)


def _out_proj_kernel(o_ref, x_ref, wo_ref, y_ref):
    y_ref[...] = x_ref[...] + jnp.dot(o_ref[...], wo_ref[...].astype(BF16),
                                      preferred_element_type=F32)


def _attn_prompt_kernel(q_ref, k_ref, v_ref, u_ref, y_ref,
                        kbuf, vbuf, obuf, bias_scr, edge_scr, *, tq):
    t = pl.program_id(1)
    n_pairs = q_ref.shape[1] // LANES

    @pl.when((pl.program_id(0) == 0) & (t == 0))
    def _():
        _band_bias_init(u_ref, bias_scr, edge_scr)

    @pl.when(t == 0)
    def _():
        kbuf[...] = jnp.zeros(kbuf.shape, BF16)
        vbuf[...] = jnp.zeros(vbuf.shape, BF16)

    def window(prev, cur, r0, cols):
        return jnp.concatenate([prev[r0:tq, cols], cur[0:r0 + GROUP_KEYS - tq, cols]], axis=0)

    def tile_items(masked):
        def item(n):
            g, p = divmod(n, n_pairs)
            return g * GROUP_ROWS, slice(p * LANES, (p + 1) * LANES), p

        def scores(n):
            r0, cols, _ = item(n)
            return _pair_scores(q_ref[r0:r0 + GROUP_ROWS, cols], window(kbuf, k_ref, r0, cols))

        def softmax(n, s):
            r0, _, p = item(n)
            s = jnp.concatenate(
                [s[:, :LANES] + edge_scr[...], s[:, LANES:VAR_COL0],
                 s[:, VAR_COL0:] + bias_scr[p]], axis=1)
            lim = (REACH - (t * tq + r0)) if masked else None
            return _pair_softmax(s, None, lim)

        def output(n, e, inv_l):
            r0, cols, _ = item(n)
            o = _pair_output(e, inv_l, window(vbuf, v_ref, r0, cols))
            obuf[r0:r0 + GROUP_ROWS, cols] = o.astype(BF16)

        def done(n):
            if (n + 1) % (PROJ_GROUPS * n_pairs) == 0:
                rows = PROJ_GROUPS * GROUP_ROWS
                r0 = (n + 1) // n_pairs * GROUP_ROWS - rows
                y_ref[r0:r0 + rows, :] = obuf[r0:r0 + rows, :]

        _pipelined_pairs(tq // GROUP_ROWS * n_pairs, scores, softmax, output,
                         ATTN_LEAD_SCORES, ATTN_LEAD_SOFTMAX, done)

    @pl.when(t == 0)
    def _():
        tile_items(True)

    @pl.when(t > 0)
    def _():
        tile_items(False)

    kbuf[...] = k_ref[...]
    vbuf[...] = v_ref[...]


def _attn_prompt(q, k, v, x, wo, table):
    b, s, d = x.shape
    tq = REACH
    assert s % tq == 0
    assert tq % (PROJ_GROUPS * GROUP_ROWS) == 0
    h = table.shape[0]
    n_diag = GROUP_ROWS - 1 + GROUP_KEYS - VAR_COL0
    diag = jnp.arange(n_diag) + (VAR_COL0 - GROUP_ROWS + 1)
    idx = jnp.clip(REACH - diag, -MAX_REL, MAX_REL) + MAX_REL
    u = (table[:, idx] - table[:, 2 * MAX_REL:]).astype(F32) * LOG2_E
    u = jnp.pad(u, ((0, 0), (0, pl.next_power_of_2(n_diag) - n_diag)))
    row = pl.BlockSpec((None, tq, d), lambda i, j: (i, j, 0))
    rows2 = HEADS_PER_VREG * GROUP_ROWS
    o = pl.pallas_call(
        functools.partial(_attn_prompt_kernel, tq=tq),
        grid=(b, s // tq),
        in_specs=[row, row, row, _const_spec(u.shape)],
        out_specs=row,
        out_shape=jax.ShapeDtypeStruct((b, s, d), BF16),
        scratch_shapes=[pltpu.VMEM((tq, d), BF16), pltpu.VMEM((tq, d), BF16),
                        pltpu.VMEM((tq, d), BF16),
                        pltpu.VMEM((h // HEADS_PER_VREG, rows2, GROUP_KEYS - VAR_COL0), F32),
                        pltpu.VMEM((rows2, LANES), F32)],
        compiler_params=_params(2),
        name="attn_prompt",
    )(q, k, v, u)
    prow = pl.BlockSpec((None, ROW_TILE, d), lambda i, j: (i, j, 0))
    return pl.pallas_call(
        _out_proj_kernel,
        grid=(b, s // ROW_TILE),
        in_specs=[prow, prow, _const_spec((d, d))],
        out_specs=prow,
        out_shape=jax.ShapeDtypeStruct((b, s, d), F32),
        compiler_params=_params(2),
        name="attn_out_proj",
    )(o, x, wo)


def _attn_sample_kernel(q_ref, k_ref, v_ref, ckt_ref, cvt_ref, x_ref, wo_ref, bias_ref, y_ref,
                        obuf):
    n_seq, tn, d = q_ref.shape
    n_pairs = d // LANES
    ca = ckt_ref.shape[3]

    def item(n):
        e, p = divmod(n, n_pairs)
        return e, p, slice(p * LANES, (p + 1) * LANES)

    def pair_rows(ref, e, p):
        return jnp.concatenate(
            [ref[e, HEADS_PER_VREG * p + i] for i in range(HEADS_PER_VREG)], axis=0).astype(BF16)

    def scores(n):
        e, p, cols = item(n)
        q_bd = _block_diag_queries(q_ref[e, :, cols])
        s_cache = jnp.dot(q_bd, pair_rows(ckt_ref, e, p), preferred_element_type=F32)
        s_new = lax.dot_general(q_bd, k_ref[e, :, cols], _CONTRACT_LANES,
                                preferred_element_type=F32)
        return jnp.concatenate([s_cache, s_new], axis=1)

    def softmax(n, s):
        return _pair_softmax(s, bias_ref[item(n)[1]], None)

    def output(n, pr, inv_l):
        e, p, cols = item(n)
        pv = lax.dot_general(pr[:, :ca], pair_rows(cvt_ref, e, p), _CONTRACT_LANES,
                             preferred_element_type=F32)
        pv = pv + jnp.dot(pr[:, ca:], v_ref[e, :, cols], preferred_element_type=F32)
        obuf[e * tn:(e + 1) * tn, cols] = _select_heads(pv * inv_l).astype(BF16)

    _pipelined_pairs(n_seq * n_pairs, scores, softmax, output)
    proj = jnp.dot(obuf[...], wo_ref[...].astype(BF16), preferred_element_type=F32)
    for e in range(n_seq):
        y_ref[e] = x_ref[e] + proj[e * tn:(e + 1) * tn]


def _attn_sample(q, k, v, cache_kt, cache_vt, x, wo, bias_pairs):
    b, tn, d = x.shape
    n_seq = SAMPLE_SEQS_PER_STEP
    assert b % n_seq == 0
    new = pl.BlockSpec((n_seq, tn, d), lambda i: (i, 0, 0))
    old = pl.BlockSpec((n_seq,) + cache_kt.shape[1:], lambda i: (i, 0, 0, 0))
    return pl.pallas_call(
        _attn_sample_kernel,
        grid=(b // n_seq,),
        in_specs=[new, new, new, old, old, new, _const_spec((d, d)),
                  _const_spec(bias_pairs.shape)],
        out_specs=new,
        out_shape=jax.ShapeDtypeStruct((b, tn, d), F32),
        scratch_shapes=[pltpu.VMEM((n_seq * tn, d), BF16)],
        compiler_params=_params(1),
        name="attn_sample",
    )(q, k, v, cache_kt, cache_vt, x, wo, bias_pairs)


def _gelu(z):
    return 0.5 * z * (1.0 + lax.erf(z * np.float32(np.sqrt(0.5))))


def _gmlp_rows(x, g_ref, win_ref, vg_ref, ws_ref, bs_ref, wout_ref, gated):
    rows, d = x.shape
    blk = bs_ref.shape[0]
    n_pairs = d // LANES
    h = _rms(x, g_ref[...]).astype(BF16)
    v = _rms(_gelu(jnp.dot(h, win_ref[:, d:].astype(BF16), preferred_element_type=F32)),
             vg_ref[...])
    u = _gelu(jnp.dot(h, win_ref[:, :d].astype(BF16), preferred_element_type=F32))
    n_blk = rows // blk
    lane = lax.broadcasted_iota(jnp.int32, (blk, n_blk * LANES), 1) & (LANES - 1)
    zero = jnp.zeros((blk, n_blk * LANES), F32)
    for p in range(n_pairs):
        cols = slice(p * LANES, (p + 1) * LANES)
        vcat = jnp.concatenate([v[r * blk:(r + 1) * blk, cols] for r in range(n_blk)], axis=1)
        v_stack = jnp.concatenate(
            [jnp.where(lane < HEAD_DIM, vcat, zero), jnp.where(lane >= HEAD_DIM, vcat, zero)],
            axis=0).astype(BF16)
        m = jnp.dot(ws_ref[p], v_stack, preferred_element_type=F32)
        for r in range(n_blk):
            rs = slice(r * blk, (r + 1) * blk)
            mixed = m[:, r * LANES:(r + 1) * LANES] + bs_ref[:, cols]
            gated[rs, cols] = (u[rs, cols] * mixed).astype(BF16)
    y = x + jnp.dot(gated[0:rows, :], wout_ref[...].astype(BF16), preferred_element_type=F32)
    return y, v


def _gmlp_kernel(x_ref, xs_ref, g_ref, win_ref, vg_ref, ws_ref, bs_ref, wss_ref, bss_ref, wout_ref,
                 y_ref, ys_ref, vs_ref, gated, *, n_tiles):
    i = pl.program_id(0)

    @pl.when(i < n_tiles)
    def _():
        y_ref[...] = _gmlp_rows(x_ref[...], g_ref, win_ref, vg_ref, ws_ref, bs_ref, wout_ref,
                                gated)[0]

    @pl.when(i == n_tiles)
    def _():
        ys_ref[...], vs_ref[...] = _gmlp_rows(xs_ref[...], g_ref, win_ref, vg_ref, wss_ref,
                                              bss_ref, wout_ref, gated)


def _gmlp(x, xs, gain, w_in, v_gain, ws_pairs, bs_rows, wss_pairs, bss_rows, w_out, rows):
    b, s, d = x.shape
    r = xs.shape[0]
    w = w_in.shape[1] - d
    nt = s // rows
    n_tiles = b * nt
    assert s % rows == 0 and rows % bs_rows.shape[0] == 0 and r % bss_rows.shape[0] == 0
    assert w == d and r <= rows

    def tile(i):
        i = jnp.minimum(i, n_tiles - 1)
        return i // nt, i % nt

    row = pl.BlockSpec((None, rows, d), lambda i: (*tile(i), 0))
    few = pl.BlockSpec((r, d), lambda i: (0, 0))
    return pl.pallas_call(
        functools.partial(_gmlp_kernel, n_tiles=n_tiles),
        grid=(n_tiles + 1,),
        in_specs=[row, _const_spec((r, d)), _const_spec((1, d)), _const_spec(w_in.shape),
                  _const_spec((1, w)), _const_spec(ws_pairs.shape), _const_spec(bs_rows.shape),
                  _const_spec(wss_pairs.shape), _const_spec(bss_rows.shape),
                  _const_spec(w_out.shape)],
        out_specs=[row, few, few],
        out_shape=[jax.ShapeDtypeStruct((b, s, d), F32), jax.ShapeDtypeStruct((r, d), F32),
                   jax.ShapeDtypeStruct((r, w), F32)],
        scratch_shapes=[pltpu.VMEM((rows, w), BF16)],
        compiler_params=_params(1),
        name="gmlp",
    )(x, xs, gain.reshape(1, d), w_in, v_gain.reshape(1, w), ws_pairs, bs_rows, wss_pairs,
      bss_rows, w_out)


def _conv_taps(stage, n, w, cb):
    c = cb + stage[SUBLANES:SUBLANES + n, :] * w[2:3]
    c = c + stage[SUBLANES - 1:SUBLANES - 1 + n, :] * w[1:2]
    return c + stage[SUBLANES - 2:SUBLANES - 2 + n, :] * w[0:1]


def _ffn_kernel(x_ref, g_ref, wg_ref, wv_ref, cw_ref, cb_ref, wdn_ref, gf_ref,
                y_ref, carry_ref, h_scr, a_scr, act_scr, *, final_norm):
    t = pl.program_id(1)
    rows = x_ref.shape[0]
    ff = wdn_ref.shape[0]
    fc = FF_CHUNK

    @pl.when(t == 0)
    def _():
        carry_ref[...] = jnp.zeros_like(carry_ref)

    x = x_ref[...]
    h_scr[...] = _rms(x, g_ref[...]).astype(BF16)

    for j in range(ff // fc):
        chunk = slice(j * fc, (j + 1) * fc)
        halves = []
        for half, w_ref in enumerate((wg_ref, wv_ref)):
            cols = slice(half * ff + j * fc, half * ff + (j + 1) * fc)
            stage = a_scr.at[2 * (j % 2) + half]
            a = jnp.dot(h_scr[...], w_ref[j], preferred_element_type=F32)
            stage[0:SUBLANES, :] = carry_ref[:, cols]
            stage[SUBLANES:SUBLANES + rows, :] = a
            carry_ref[:, cols] = a[rows - SUBLANES:rows]
            halves.append(_conv_taps(stage, rows, cw_ref[:, cols], cb_ref[:, cols]))
        act_scr[:, chunk] = (jax.nn.silu(halves[0]) * halves[1]).astype(BF16)

    y = x + jnp.dot(act_scr[...], wdn_ref[...], preferred_element_type=F32)
    if final_norm:
        y = _rms(y, gf_ref[...])
    y_ref[...] = y


def _ffn_stream_kernel(x_ref, g_ref, hg_ref, hv_ref, wg_ref, wv_ref, cwg_ref, cwv_ref, cbg_ref,
                       cbv_ref, wdn_ref, gf_ref, y_ref, cg_ref, cv_ref, wg_out, wv_out, wdn_out,
                       h_scr, acc_scr, a_scr, *, n_sub, final_norm):
    j = pl.program_id(0)
    rows = x_ref.shape[0]
    seq = rows // n_sub

    @pl.when(j == 0)
    def _():
        h_scr[...] = _rms(x_ref[...], g_ref[...]).astype(BF16)
        acc_scr[...] = jnp.zeros_like(acc_scr)

    def branch(slot, w_ref, w_out, hist_ref, carry_out, cw_ref, cb_ref):
        w_bf = w_ref[...].astype(BF16)
        w_out[...] = w_bf
        a = jnp.dot(h_scr[...], w_bf, preferred_element_type=F32)
        outs = []
        for s in range(n_sub):
            stage = a_scr.at[slot, s]
            stage[0:SUBLANES, :] = hist_ref[s]
            stage[SUBLANES:SUBLANES + seq, :] = a[s * seq:(s + 1) * seq]
            carry_out[s] = a[(s + 1) * seq - SUBLANES:(s + 1) * seq]
            outs.append(_conv_taps(stage, seq, cw_ref[...], cb_ref[...]))
        return jnp.concatenate(outs, axis=0)

    gate = branch(0, wg_ref, wg_out, hg_ref, cg_ref, cwg_ref, cbg_ref)
    val = branch(1, wv_ref, wv_out, hv_ref, cv_ref, cwv_ref, cbv_ref)
    wd_bf = wdn_ref[...].astype(BF16)
    wdn_out[...] = wd_bf
    acc_scr[...] += jnp.dot((jax.nn.silu(gate) * val).astype(BF16), wd_bf,
                            preferred_element_type=F32)

    @pl.when(j == pl.num_programs(0) - 1)
    def _():
        y = x_ref[...] + acc_scr[...]
        if final_norm:
            y = _rms(y, gf_ref[...])
        y_ref[...] = y


def _layer_spec(shape, layer):
    nd = len(shape)
    return pl.BlockSpec((None,) + tuple(shape[1:]), lambda *_: (layer,) + (0,) * (nd - 1),
                        pipeline_mode=pl.Buffered(1))


def _conv_ffn(x, gain, wg_bf, wv_bf, conv_w, conv_b, wdn_bf, layer, final_gain, rows):
    b, s, d = x.shape
    ff = wdn_bf.shape[0]
    assert s % rows == 0 and rows % SUBLANES == 0
    assert wg_bf.shape == wv_bf.shape == (ff // FF_CHUNK, d, FF_CHUNK)
    final_norm = final_gain is not None
    gf = (final_gain if final_norm else jnp.ones((d,), F32)).reshape(1, d)
    row = pl.BlockSpec((None, rows, d), lambda i, j: (i, j, 0))
    carry_spec = pl.BlockSpec((None, SUBLANES, 2 * ff), lambda i, j: (i, 0, 0))
    return pl.pallas_call(
        functools.partial(_ffn_kernel, final_norm=final_norm),
        grid=(b, s // rows),
        in_specs=[row, _const_spec((1, d)), _const_spec(wg_bf.shape), _const_spec(wv_bf.shape),
                  _layer_spec(conv_w.shape, layer), _layer_spec(conv_b.shape, layer),
                  _const_spec(wdn_bf.shape), _const_spec((1, d))],
        out_specs=[row, carry_spec],
        out_shape=[jax.ShapeDtypeStruct((b, s, d), F32),
                   jax.ShapeDtypeStruct((b, SUBLANES, 2 * ff), F32)],
        scratch_shapes=[pltpu.VMEM((rows, d), BF16),
                        pltpu.VMEM((4, rows + SUBLANES, FF_CHUNK), F32),
                        pltpu.VMEM((rows, ff), BF16)],
        compiler_params=_params(2),
        name="conv_ffn",
    )(x, gain.reshape(1, d), wg_bf, wv_bf, conv_w, conv_b, wdn_bf, gf)


def _conv_ffn_stream(x, gain, hist, w_up, conv_w, conv_b, w_down, layer, final_gain, n_sub):
    r, d = x.shape
    ff = w_down.shape[1]
    fc = FF_CHUNK
    c = ff // fc
    seq = r // n_sub
    assert ff % fc == 0 and seq % SUBLANES == 0
    final_norm = final_gain is not None
    gf = (final_gain if final_norm else jnp.ones((d,), F32)).reshape(1, d)

    def cols(blk_rows, half):
        return pl.BlockSpec((None, blk_rows, fc), lambda j: (layer, 0, half * c + j))

    def hist_cols(half):
        return pl.BlockSpec((n_sub, SUBLANES, fc), lambda j: (0, 0, half * c + j))

    half_cols = pl.BlockSpec((n_sub, SUBLANES, fc), lambda j: (0, 0, j))
    up_cols = pl.BlockSpec((None, d, fc), lambda j: (j, 0, 0))
    dn_rows = pl.BlockSpec((fc, d), lambda j: (j, 0))
    return pl.pallas_call(
        functools.partial(_ffn_stream_kernel, n_sub=n_sub, final_norm=final_norm),
        grid=(c,),
        in_specs=[_const_spec((r, d)), _const_spec((1, d)), hist_cols(0), hist_cols(1),
                  cols(d, 0), cols(d, 1), cols(CONV_W, 0), cols(CONV_W, 1), cols(1, 0), cols(1, 1),
                  pl.BlockSpec((None, fc, d), lambda j: (layer, j, 0)),
                  _const_spec((1, d))],
        out_specs=[pl.BlockSpec((r, d), lambda j: (0, 0)), half_cols, half_cols,
                   up_cols, up_cols, dn_rows],
        out_shape=[jax.ShapeDtypeStruct((r, d), F32),
                   jax.ShapeDtypeStruct((n_sub, SUBLANES, ff), F32),
                   jax.ShapeDtypeStruct((n_sub, SUBLANES, ff), F32),
                   jax.ShapeDtypeStruct((c, d, fc), BF16),
                   jax.ShapeDtypeStruct((c, d, fc), BF16),
                   jax.ShapeDtypeStruct((ff, d), BF16)],
        scratch_shapes=[pltpu.VMEM((r, d), BF16), pltpu.VMEM((r, d), F32),
                        pltpu.VMEM((2, n_sub, seq + SUBLANES, fc), F32)],
        compiler_params=_params(1),
        name="conv_ffn_stream",
    )(x, gain.reshape(1, d), hist, hist, w_up, w_up, conv_w, conv_w, conv_b, conv_b, w_down, gf)


def _pair_bias(table, n_q, n_k, reach):
    diag = jnp.arange(n_q + n_k - 1) - (n_q - 1)
    idx = jnp.clip(reach - diag, -MAX_REL, MAX_REL) + MAX_REL
    r = table[:, idx].astype(F32) * LOG2_E
    h, m = r.shape
    flat = jnp.tile(jnp.pad(r, ((0, 0), (0, 1))), (1, n_q))[:, :n_q * m]
    bias = flat.reshape(h, n_q, m)[:, :, n_q - 1:n_q - 1 + n_k]
    return bias.reshape(h // HEADS_PER_VREG, HEADS_PER_VREG * n_q, n_k)


def _pad_hist(hist):
    pad = [(0, 0)] * (hist.ndim - 2) + [(SUBLANES - hist.shape[-2], 0), (0, 0)]
    return jnp.pad(hist, pad)


def kernel(x_prompt, x_sample, cache_a_k, cache_a_v, state_ffn_conv, ln_mix, ln_ffn, ln_final,
           a_w_qkv, a_rel_bias, a_w_o, b_w_in, b_v_norm, b_w_s, b_bias_s, b_w_out,
           f_w_up, f_conv_w, f_conv_b, f_w_down):
    bp, sp, d = x_prompt.shape
    bs, ts, _ = x_sample.shape
    depth = ln_mix.shape[0]
    ca = cache_a_k.shape[2]
    keep = min(REACH, sp)
    two_ff = f_w_up.shape[2]
    hist_rows = CONV_W - 1
    conv_b_all = f_conv_b.reshape(depth, 1, two_ff)

    xp = x_prompt
    xs = x_sample.reshape(bs * ts, d)
    kp_l, vp_l, ks_l, vs_l, gv_l, cp_l, cs_l = [], [], [], [], [], [], []

    for i in range(depth):
        j = i // 2
        if i % 2 == 0:
            wqkv = a_w_qkv[j]
            wo = a_w_o[j]
            qp, kp, vp, ktp, vtp, q, k, v, kt, vt = _qkv_proj(
                xp, xs, ln_mix[i], wqkv, keep, ROW_TILE)
            xp = _attn_prompt(qp, kp, vp, xp, wo, a_rel_bias[j])
            kp_l.append(jnp.transpose(ktp.reshape(bp, N_HEADS, HEAD_DIM, keep), (0, 3, 1, 2)))
            vp_l.append(jnp.transpose(vtp.reshape(bp, N_HEADS, HEAD_DIM, keep), (0, 3, 1, 2)))
            xs = _attn_sample(
                q.reshape(bs, ts, d), k.reshape(bs, ts, d), v.reshape(bs, ts, d),
                jnp.transpose(cache_a_k[j], (0, 2, 3, 1)),
                jnp.transpose(cache_a_v[j], (0, 2, 3, 1)),
                xs.reshape(bs, ts, d), wo, _pair_bias(a_rel_bias[j], ts, ca + ts, ca),
            ).reshape(bs * ts, d)
            ks_l.append(kt.reshape(bs, ts, N_HEADS, HEAD_DIM))
            vs_l.append(vt.reshape(bs, ts, N_HEADS, HEAD_DIM))
        else:
            win = b_w_in[j]
            wout = b_w_out[j]
            pos = jnp.arange(MLP_BLOCK)
            causal = (pos[None, :] // CHUNK) <= (pos[:, None] // CHUNK)
            ws = jnp.where(causal, b_w_s[j], 0.0).astype(BF16)
            g = ws.shape[0]

            def pair_rows(w):
                w = w.reshape(g // HEADS_PER_VREG, HEADS_PER_VREG, w.shape[1], w.shape[2])
                return jnp.transpose(w, (0, 2, 1, 3)).reshape(
                    g // HEADS_PER_VREG, w.shape[2], HEADS_PER_VREG * w.shape[3])

            def bias_rows(bias):
                return jnp.repeat(bias.T.astype(F32), d // g, axis=1)

            xp, xs, gv = _gmlp(xp, xs, ln_mix[i], win, b_v_norm[j], pair_rows(ws),
                               bias_rows(b_bias_s[j]), pair_rows(ws[:, :ts, :ts]),
                               bias_rows(b_bias_s[j][:, :ts]), wout, ROW_TILE)
            gv_l.append(gv.reshape(bs, ts, d))

        fin = ln_final if i == depth - 1 else None
        xs, cs_g, cs_v, wg, wv, wdn = _conv_ffn_stream(
            xs, ln_ffn[i], _pad_hist(state_ffn_conv[i]), f_w_up, f_conv_w, conv_b_all,
            f_w_down, i, fin, bs)
        xp, cp = _conv_ffn(xp, ln_ffn[i], wg, wv, f_conv_w, conv_b_all, wdn, i, fin,
                           FFN_ROW_TILE)
        cp_l.append(cp[:, SUBLANES - hist_rows:, :])
        cs_l.append(jnp.concatenate([cs_g, cs_v], axis=-1)[:, SUBLANES - hist_rows:, :])

    return (xp, xs.reshape(bs, ts, d), jnp.stack(kp_l), jnp.stack(vp_l), jnp.stack(ks_l),
            jnp.stack(vs_l), jnp.stack(gv_l), jnp.stack(cp_l), jnp.stack(cs_l))
```
